```python
import math
import jax
import jax.numpy as jnp
from jax import lax
import numpy as np

D_MODEL = 1024
BATCH = 8
SEQ = 2048
DEPTH = 1
DEC_BATCH = 16
DEC_SEQ = 16
PAST_LEN = 2048

CHUNK = 64
Q_BLOCK = 128
HGRN_DK = 128
HGRN_DV = 128
HGRN_HEADS = D_MODEL // HGRN_DK
DIFF_DH = 64
DIFF_HEADS = D_MODEL // (2 * DIFF_DH)
DIFF_SCALE = DIFF_DH ** -0.5
MEM_LEN = 256
MEM_HEADS = 4
MEM_DH = D_MODEL // MEM_HEADS
REL_BUCKETS = 32
REL_MAX_DIST = 128
D_FF = 256 * ((8 * D_MODEL // 3 + 255) // 256)
FFN_CONV = 3
N_BRANCH = 3
BRANCH_W = D_MODEL
RMS_EPS = 1e-6
IN_WIDTHS = (HGRN_HEADS * HGRN_DK, HGRN_HEADS * HGRN_DK, HGRN_HEADS * HGRN_DV, HGRN_HEADS * HGRN_DV,
             DIFF_HEADS * 2 * DIFF_DH, DIFF_HEADS * 2 * DIFF_DH, DIFF_HEADS * 2 * DIFF_DH,
             MEM_HEADS * MEM_DH, N_BRANCH * D_MODEL)
IN_COLS = sum(IN_WIDTHS)

kernel_name = "hybrid_stream_hgrn2_diffattn_step"


def _rmsnorm(x, g):
    xf = x.astype(jnp.float32)
    y = xf * lax.rsqrt(jnp.mean(xf * xf, axis=-1, keepdims=True) + RMS_EPS)
    return (y * g.astype(jnp.float32)).astype(x.dtype)


def _split_in(z):
    idx = [int(i) for i in np.cumsum(IN_WIDTHS)[:-1]]
    return jnp.split(z, idx, axis=-1)


def _lambda_init(layer):
    return 0.8 - 0.6 * math.exp(-0.3 * layer)


def _hgrn_inputs(q, f, i, lb):
    B, L, _ = q.shape
    fg = lb + (1.0 - lb) * jax.nn.sigmoid(f.astype(jnp.float32))
    logf = jnp.log(fg)
    k = 1.0 - fg

    def heads(t, d):
        return jnp.transpose(t.reshape(B, L, HGRN_HEADS, d), (0, 2, 1, 3))

    return (heads(q.astype(jnp.float32), HGRN_DK), heads(k, HGRN_DK),
            heads(logf, HGRN_DK), heads(i.astype(jnp.float32), HGRN_DV))


def _hgrn_chunk(s0, q, k, logf, v):
    L = q.shape[2]
    b = jnp.cumsum(logf, axis=2)
    causal = jnp.tril(jnp.ones((L, L), dtype=bool))[:, :, None]
    decay = jnp.exp(jnp.where(causal, b[:, :, :, None, :] - b[:, :, None, :, :], -jnp.inf))
    scores = jnp.einsum('bhtsd,bhsd->bhts', decay * q[:, :, :, None, :], k)
    o = (jnp.einsum('bhts,bhsv->bhtv', scores, v)
         + jnp.einsum('bhtd,bhdv->bhtv', q * jnp.exp(b), s0))
    b_last = b[:, :, -1:, :]
    s_new = (jnp.exp(b_last[:, :, 0])[..., None] * s0
             + jnp.einsum('bhsd,bhsv->bhdv', k * jnp.exp(b_last - b), v))
    return o, s_new


def _hgrn_scan(s0, q, k, logf, v, chunk_len):
    B, H, L, _ = q.shape
    nc = L // chunk_len

    def to_chunks(t):
        return jnp.moveaxis(t.reshape(B, H, nc, chunk_len, t.shape[-1]), 2, 0)

    def step(s, xs):
        o, s = _hgrn_chunk(s, *xs)
        return s, o

    s, o = lax.scan(step, s0, (to_chunks(q), to_chunks(k), to_chunks(logf), to_chunks(v)))
    return jnp.moveaxis(o, 0, 2).reshape(B, H, L, v.shape[-1]), s


def _hgrn_out(o, g, onorm):
    B, H, L, DV = o.shape
    o = _rmsnorm(o, onorm)
    o = jnp.transpose(o, (0, 2, 1, 3)).reshape(B, L, H * DV)
    return o.astype(g.dtype) * jax.nn.silu(g)


def _t5_bucket(rel):
    nb = REL_BUCKETS // 2
    ret = jnp.where(rel > 0, nb, 0)
    n = jnp.abs(rel)
    max_exact = nb // 2
    large = max_exact + (jnp.log(jnp.maximum(n, 1).astype(jnp.float32) / max_exact)
                         / math.log(REL_MAX_DIST / max_exact) * (nb - max_exact)).astype(jnp.int32)
    large = jnp.minimum(large, nb - 1)
    return ret + jnp.where(n < max_exact, n, large)


def _t5_bias(table, pos_q, pos_k):
    bucket = _t5_bucket(pos_k[None, :] - pos_q[:, None])
    return jnp.transpose(table[bucket].astype(jnp.float32), (2, 0, 1))


def _diff_heads(q, k, v):
    B, L, _ = q.shape
    q = jnp.transpose(q.reshape(B, L, DIFF_HEADS, 2, DIFF_DH), (0, 2, 1, 3, 4))
    k = jnp.transpose(k.reshape(B, L, DIFF_HEADS, 2 * DIFF_DH), (0, 2, 1, 3))
    v = jnp.transpose(v.reshape(B, L, DIFF_HEADS, 2 * DIFF_DH), (0, 2, 1, 3))
    return q, k, v


def _diff_lambda(lp, layer):
    lpf = lp.astype(jnp.float32)
    return (jnp.exp(jnp.sum(lpf[0] * lpf[1])) - jnp.exp(jnp.sum(lpf[2] * lpf[3]))
            + _lambda_init(layer))


def _diff_attention(q, k, v, pos_q, pos_k, rel_bias, lam):
    B, H, Lk, _ = k.shape
    k = k.reshape(B, H, Lk, 2, DIFF_DH)
    logits = jnp.einsum('bhqmd,bhkmd->bhmqk', q, k).astype(jnp.float32) * DIFF_SCALE
    logits = logits + _t5_bias(rel_bias, pos_q, pos_k)[None, :, None]
    visible = (pos_k[None, :] // CHUNK) <= (pos_q[:, None] // CHUNK)
    p = jax.nn.softmax(jnp.where(visible, logits, -jnp.inf), axis=-1)
    w = p[:, :, 0] - lam * p[:, :, 1]
    return jnp.einsum('bhqk,bhkv->bhqv', w.astype(v.dtype), v)


def _diff_out(o, subln, layer):
    B, H, L, E = o.shape
    o = _rmsnorm(o, subln) * (1.0 - _lambda_init(layer))
    return jnp.transpose(o, (0, 2, 1, 3)).reshape(B, L, H * E)


def _mem_kv(mem, g, w):
    B, M, _ = mem.shape
    kv = (_rmsnorm(mem, g) @ w).reshape(B, M, 2, MEM_HEADS, MEM_DH)
    return (jnp.transpose(kv[:, :, 0], (0, 2, 1, 3)), jnp.transpose(kv[:, :, 1], (0, 2, 1, 3)))


def _mem_attention(q, mk, mv):
    B, L, _ = q.shape
    q = q.reshape(B, L, MEM_HEADS, MEM_DH)
    logits = jnp.einsum('blhd,bhmd->bhlm', q, mk.astype(q.dtype)).astype(jnp.float32) * (MEM_DH ** -0.5)
    p = jax.nn.softmax(logits, axis=-1).astype(q.dtype)
    return jnp.einsum('bhlm,bhmd->blhd', p, mv.astype(q.dtype)).reshape(B, L, MEM_HEADS * MEM_DH)


def _merge(ya, yb, yc, gate_logits, w_branch, w_out):
    B, L, _ = ya.shape
    y = jnp.stack([ya, yb, yc], axis=0)
    proj = jnp.einsum('nblc,ncd->nbld', y, w_branch)
    gates = jax.nn.sigmoid(gate_logits.reshape(B, L, N_BRANCH, D_MODEL))
    merged = jnp.einsum('nbld,blnd->bld', proj, gates)
    return merged @ w_out


def _conv_ffn(h, conv_prev, w_up, conv_w, conv_b, w_down):
    u = h @ w_up
    L = u.shape[1]
    ext = jnp.concatenate([conv_prev.astype(u.dtype), u], axis=1)
    c = conv_b + sum(conv_w[j] * ext[:, j:j + L] for j in range(FFN_CONV))
    gate, val = jnp.split(c, 2, axis=-1)
    return (jax.nn.silu(gate) * val) @ w_down, ext[:, -(FFN_CONV - 1):]


def _layer(x, layer, lw, rel_bias, lb, hgrn_s0, conv_prev, past_k, past_v, past_len, mem_k, mem_v):
    (norm_attn, w_in, hgrn_onorm, diff_lambda, diff_subln, w_branch, w_out,
     norm_ffn, w_up, conv_w, conv_b, w_down) = lw
    B, L, _ = x.shape
    h = _rmsnorm(x, norm_attn)
    hq, hf, hi, hg, dq, dk, dv, mq, gl = _split_in(h @ w_in)
    qa, ka, lfa, va = _hgrn_inputs(hq, hf, hi, lb)
    oa, s_new = _hgrn_scan(hgrn_s0.astype(jnp.float32), qa, ka, lfa, va, min(CHUNK, L))
    ya = _hgrn_out(oa, hg, hgrn_onorm)
    qb, kb, vb = _diff_heads(dq, dk, dv)
    if past_k is None:
        k_all, v_all = kb, vb
    else:
        k_all = jnp.concatenate([past_k.astype(kb.dtype), kb], axis=2)
        v_all = jnp.concatenate([past_v.astype(vb.dtype), vb], axis=2)
    lam = _diff_lambda(diff_lambda, layer)
    pos_q = past_len + jnp.arange(L, dtype=jnp.int32)
    pos_k = jnp.arange(past_len + L, dtype=jnp.int32)
    blocks = []
    for s in range(0, L, Q_BLOCK):
        e = min(s + Q_BLOCK, L)
        n_k = past_len + e
        blocks.append(_diff_attention(qb[:, :, s:e], k_all[:, :, :n_k], v_all[:, :, :n_k],
                                      pos_q[s:e], pos_k[:n_k], rel_bias, lam))
    yb = _diff_out(jnp.concatenate(blocks, axis=2), diff_subln, layer)
    yc = _mem_attention(mq, mem_k, mem_v)
    x = x + _merge(ya, yb, yc, gl, w_branch, w_out)
    f, conv_new = _conv_ffn(_rmsnorm(x, norm_ffn), conv_prev, w_up, conv_w, conv_b, w_down)
    x = x + f
    return x, kb, vb, s_new.astype(x.dtype), conv_new


def setup_inputs(seed: int = 0) -> dict:
    key = jax.random.key(seed)
    ks = jax.random.split(key, 32)

    def nrm(k, shape, scale):
        return scale * jax.random.normal(k, shape, jnp.float32)

    return {
        "x_prompt": nrm(ks[0], (BATCH, SEQ, D_MODEL), 1.0),
        "x_sample": nrm(ks[1], (DEC_BATCH, DEC_SEQ, D_MODEL), 1.0),
        "mem_prompt": nrm(ks[2], (BATCH, MEM_LEN, D_MODEL), 1.0),
        "cache_diff_k": nrm(ks[3], (DEPTH, DEC_BATCH, DIFF_HEADS, PAST_LEN, 2 * DIFF_DH), 1.0),
        "cache_diff_v": nrm(ks[4], (DEPTH, DEC_BATCH, DIFF_HEADS, PAST_LEN, 2 * DIFF_DH), 1.0),
        "cache_mem_k": nrm(ks[5], (DEPTH, DEC_BATCH, MEM_HEADS, MEM_LEN, MEM_DH), 1.0),
        "cache_mem_v": nrm(ks[6], (DEPTH, DEC_BATCH, MEM_HEADS, MEM_LEN, MEM_DH), 1.0),
        "state_hgrn": nrm(ks[7], (DEPTH, DEC_BATCH, HGRN_HEADS, HGRN_DK, HGRN_DV), 0.5),
        "state_ffn_conv": nrm(ks[8], (DEPTH, DEC_BATCH, FFN_CONV - 1, 2 * D_FF), 1.0),
        "rel_bias": nrm(ks[9], (REL_BUCKETS, DIFF_HEADS), 0.5),
        "hgrn_lb_logits": nrm(ks[10], (DEPTH + 1, HGRN_HEADS * HGRN_DK), 1.0),
        "norm_attn": 1.0 + nrm(ks[11], (DEPTH, D_MODEL), 0.05),
        "w_in": nrm(ks[12], (DEPTH, D_MODEL, IN_COLS), D_MODEL ** -0.5),
        "hgrn_onorm": 1.0 + nrm(ks[13], (DEPTH, HGRN_DV), 0.05),
        "diff_lambda": nrm(ks[14], (DEPTH, 4, DIFF_DH), 0.1),
        "diff_subln": 1.0 + nrm(ks[15], (DEPTH, 2 * DIFF_DH), 0.05),
        "mem_norm": 1.0 + nrm(ks[16], (DEPTH, D_MODEL), 0.05),
        "w_mem_kv": nrm(ks[17], (DEPTH, D_MODEL, 2 * MEM_HEADS * MEM_DH), D_MODEL ** -0.5),
        "w_branch": nrm(ks[18], (DEPTH, N_BRANCH, BRANCH_W, D_MODEL), BRANCH_W ** -0.5),
        "w_out": nrm(ks[19], (DEPTH, D_MODEL, D_MODEL), D_MODEL ** -0.5),
        "norm_ffn": 1.0 + nrm(ks[20], (DEPTH, D_MODEL), 0.05),
        "w_up": nrm(ks[21], (DEPTH, D_MODEL, 2 * D_FF), D_MODEL ** -0.5),
        "conv_w": nrm(ks[22], (DEPTH, FFN_CONV, 2 * D_FF), FFN_CONV ** -0.5),
        "conv_b": nrm(ks[23], (DEPTH, 2 * D_FF), 0.02),
        "w_down": nrm(ks[24], (DEPTH, D_FF, D_MODEL), D_FF ** -0.5),
        "norm_final": 1.0 + nrm(ks[25], (D_MODEL,), 0.05),
    }


def reference(x_prompt, x_sample, mem_prompt, cache_diff_k, cache_diff_v, cache_mem_k, cache_mem_v,
              state_hgrn, state_ffn_conv, rel_bias, hgrn_lb_logits, norm_attn, w_in, hgrn_onorm,
              diff_lambda, diff_subln, mem_norm, w_mem_kv, w_branch, w_out, norm_ffn, w_up,
              conv_w, conv_b, w_down, norm_final):
    lb_all = jnp.cumsum(jax.nn.softmax(hgrn_lb_logits.astype(jnp.float32), axis=0), axis=0)
    past_len = cache_diff_k.shape[3]
    bp = x_prompt.shape[0]
    xp, xs = x_prompt, x_sample
    pk, pv, ps, pc, pmk, pmv = [], [], [], [], [], []
    sk, sv, ss, sc = [], [], [], []
    for l in range(DEPTH):
        lw = (norm_attn[l], w_in[l], hgrn_onorm[l], diff_lambda[l], diff_subln[l], w_branch[l],
              w_out[l], norm_ffn[l], w_up[l], conv_w[l], conv_b[l], w_down[l])
        mk, mv = _mem_kv(mem_prompt, mem_norm[l], w_mem_kv[l])
        s0 = jnp.zeros((bp, HGRN_HEADS, HGRN_DK, HGRN_DV), jnp.float32)
        c0 = jnp.zeros((bp, FFN_CONV - 1, 2 * D_FF), xp.dtype)
        xp, k_new, v_new, s_new, c_new = _layer(xp, l, lw, rel_bias, lb_all[l], s0, c0,
                                                None, None, 0, mk, mv)
        pk.append(k_new); pv.append(v_new); ps.append(s_new); pc.append(c_new)
        pmk.append(mk); pmv.append(mv)
        xs, k_new, v_new, s_new, c_new = _layer(xs, l, lw, rel_bias, lb_all[l], state_hgrn[l],
                                                state_ffn_conv[l], cache_diff_k[l], cache_diff_v[l],
                                                past_len, cache_mem_k[l], cache_mem_v[l])
        sk.append(k_new); sv.append(v_new); ss.append(s_new); sc.append(c_new)
    y_prompt = _rmsnorm(xp, norm_final)
    y_sample = _rmsnorm(xs, norm_final)
    return (y_prompt, y_sample,
            jnp.stack(pk), jnp.stack(pv), jnp.stack(ps), jnp.stack(pc), jnp.stack(pmk), jnp.stack(pmv),
            jnp.stack(sk), jnp.stack(sv), jnp.stack(ss), jnp.stack(sc))
```

```python
import functools
import math

import numpy as np
import jax
import jax.numpy as jnp
from jax import lax
from jax.experimental import pallas as pl
from jax.experimental.pallas import tpu as pltpu

F32 = jnp.float32
BF16 = jnp.bfloat16

D_MODEL = 1024
CHUNK = 64
SUB = 16
HEAD_W = 128
N_HEADS = D_MODEL // HEAD_W
DIFF_DH = 64
DIFF_SCALE = DIFF_DH ** -0.5
MEM_HEADS = 4
MEM_DH = D_MODEL // MEM_HEADS
REL_BUCKETS = 32
REL_MAX_DIST = 128
N_BRANCH = 3
FFN_CONV = 3
RMS_EPS = 1e-6
LAMBDA_INIT = 0.8 - 0.6 * math.exp(-0.3 * 0)
NEG = -1e30

SEC_HQ, SEC_HF, SEC_HI, SEC_HG, SEC_DQ, SEC_DK, SEC_DV, SEC_MQ, SEC_G0 = range(9)
N_SEC = 11
Z_HQ, Z_HI, Z_HG, Z_DQ, Z_MQ, Z_G0 = 0, 1, 2, 3, 4, 5
N_ZSLOT = 8

V7X_VMEM_LIMIT = 56 * 1024 * 1024


def _cparams(sem, vmem=V7X_VMEM_LIMIT):
    return pltpu.CompilerParams(dimension_semantics=sem, vmem_limit_bytes=vmem)


def _sigmoid(x):
    return 1.0 / (1.0 + jnp.exp(-x))


def _rms(x, g):
    ms = jnp.mean(x * x, axis=-1, keepdims=True)
    return x * lax.rsqrt(ms + RMS_EPS) * g


def _np_bucket(rel):
    nb = REL_BUCKETS // 2
    ret = np.where(rel > 0, nb, 0)
    n = np.abs(rel)
    max_exact = nb // 2
    large = max_exact + (np.log(np.maximum(n, 1).astype(np.float32) / max_exact)
                         / math.log(REL_MAX_DIST / max_exact) * (nb - max_exact)).astype(np.int32)
    large = np.minimum(large, nb - 1)
    return ret + np.where(n < max_exact, n, large)


def _bucket_segments(lo, hi):
    rel = np.arange(lo, hi + 1, dtype=np.int32)
    b = _np_bucket(rel)
    change = np.nonzero(np.diff(b))[0]
    return int(b[0]), [(int(rel[i + 1]), int(b[i + 1])) for i in change]


def _bias_from_rel(rel, rb_ref, h, first_bucket, segs):
    val = jnp.full(rel.shape, rb_ref[first_bucket, h], F32)
    for lo, bk in segs:
        val = jnp.where(rel >= lo, rb_ref[bk, h], val)
    return val


def _bias_prompt_kernel(rb_ref, o_ref, *, T, first_bucket, segs):
    h = pl.program_id(0)
    kk = lax.broadcasted_iota(jnp.int32, (T, T), 0)
    qq = lax.broadcasted_iota(jnp.int32, (T, T), 1)
    o_ref[0, 0] = jnp.full((T, T), rb_ref[first_bucket, h], F32)
    o_ref[0, 1] = _bias_from_rel(kk - qq - T, rb_ref, h, first_bucket, segs)
    diag = _bias_from_rel(kk - qq, rb_ref, h, first_bucket, segs)
    o_ref[0, 2] = jnp.where(jnp.bitwise_and(kk, -CHUNK) <= qq, diag, NEG)


def _bias_prompt(rel_bias, T):
    H = rel_bias.shape[1]
    first_bucket, segs = _bucket_segments(-2 * T, T)
    return pl.pallas_call(
        functools.partial(_bias_prompt_kernel, T=T, first_bucket=first_bucket, segs=segs),
        grid=(H,),
        in_specs=[pl.BlockSpec(memory_space=pltpu.SMEM)],
        out_specs=pl.BlockSpec((1, 3, T, T), lambda h: (h, 0, 0, 0)),
        out_shape=jax.ShapeDtypeStruct((H, 3, T, T), F32),
        compiler_params=_cparams(("arbitrary",)),
        name="bias_prompt",
    )(rel_bias)


def _bias_sample_kernel(rb_ref, bp_ref, bn_ref, *, Ls, P, first_bucket, segs):
    h = pl.program_id(0)
    qq = jnp.bitwise_and(lax.broadcasted_iota(jnp.int32, (2 * Ls, P), 0), Ls - 1)
    kk = lax.broadcasted_iota(jnp.int32, (2 * Ls, P), 1)
    bp_ref[0] = _bias_from_rel(kk - P - qq, rb_ref, h, first_bucket, segs)
    qn = jnp.bitwise_and(lax.broadcasted_iota(jnp.int32, (2 * Ls, Ls), 0), Ls - 1)
    kn = lax.broadcasted_iota(jnp.int32, (2 * Ls, Ls), 1)
    bn_ref[0] = _bias_from_rel(kn - qn, rb_ref, h, first_bucket, segs)


def _bias_sample(rel_bias, Ls, P):
    H = rel_bias.shape[1]
    first_bucket, segs = _bucket_segments(-(P + Ls), Ls)
    return pl.pallas_call(
        functools.partial(_bias_sample_kernel, Ls=Ls, P=P, first_bucket=first_bucket, segs=segs),
        grid=(H,),
        in_specs=[pl.BlockSpec(memory_space=pltpu.SMEM)],
        out_specs=[pl.BlockSpec((1, 2 * Ls, P), lambda h: (h, 0, 0)),
                   pl.BlockSpec((1, 2 * Ls, Ls), lambda h: (h, 0, 0))],
        out_shape=[jax.ShapeDtypeStruct((H, 2 * Ls, P), F32),
                   jax.ShapeDtypeStruct((H, 2 * Ls, Ls), F32)],
        compiler_params=_cparams(("arbitrary",)),
        name="bias_sample",
    )(rel_bias)


def _in_proj_kernel(x_ref, g_ref, w_ref, z_ref, f_ref, k_ref, v_ref, h_scr, *, head_layout):
    j = pl.program_id(1)

    @pl.when(j == 0)
    def _():
        h_scr[...] = _rms(x_ref[...], g_ref[...]).astype(BF16)

    def section():
        return jnp.dot(h_scr[...], w_ref[...], preferred_element_type=F32)

    def store_heads(ref, acc):
        if head_layout:
            for hh in range(N_HEADS):
                ref[0, hh] = acc[:, hh * HEAD_W:(hh + 1) * HEAD_W]
        else:
            ref[...] = acc

    @pl.when(j == SEC_HF)
    def _():
        f_ref[...] = section()

    @pl.when(j == SEC_DK)
    def _():
        store_heads(k_ref, section())

    @pl.when(j == SEC_DV)
    def _():
        store_heads(v_ref, section())

    @pl.when((j != SEC_HF) & (j != SEC_DK) & (j != SEC_DV))
    def _():
        z_ref[0] = section().astype(BF16)


def _zslot(j):
    return (j - (j >= SEC_HF).astype(jnp.int32) - (j >= SEC_DK).astype(jnp.int32)
            - (j >= SEC_DV).astype(jnp.int32))


def _in_proj(x2d, g, w_bf, *, tm, rows_per_batch, head_layout):
    N = x2d.shape[0]
    nt = N // tm
    if head_layout:
        B = N // rows_per_batch
        nlt = rows_per_batch // tm
        kv_shape = jax.ShapeDtypeStruct((B, N_HEADS, rows_per_batch, HEAD_W), F32)
        kv_spec = pl.BlockSpec((1, N_HEADS, tm, HEAD_W), lambda i, j: (i // nlt, 0, i % nlt, 0))
    else:
        kv_shape = jax.ShapeDtypeStruct((N, D_MODEL), F32)
        kv_spec = pl.BlockSpec((tm, D_MODEL), lambda i, j: (i, 0))
    return pl.pallas_call(
        functools.partial(_in_proj_kernel, head_layout=head_layout),
        grid=(nt, N_SEC),
        in_specs=[pl.BlockSpec((tm, D_MODEL), lambda i, j: (i, 0)),
                  pl.BlockSpec((1, D_MODEL), lambda i, j: (0, 0)),
                  pl.BlockSpec((D_MODEL, D_MODEL), lambda i, j: (0, j))],
        out_specs=[pl.BlockSpec((1, tm, D_MODEL), lambda i, j: (_zslot(j), i, 0)),
                   pl.BlockSpec((tm, D_MODEL), lambda i, j: (i, 0)),
                   kv_spec, kv_spec],
        out_shape=[jax.ShapeDtypeStruct((N_ZSLOT, N, D_MODEL), BF16),
                   jax.ShapeDtypeStruct((N, D_MODEL), F32),
                   kv_shape, kv_shape],
        scratch_shapes=[pltpu.VMEM((tm, D_MODEL), BF16)],
        compiler_params=_cparams(("arbitrary", "arbitrary")),
        name="in_proj",
    )(x2d, g, w_bf)


def _hgrn_kernel(q_ref, f_ref, i_ref, g_ref, lbl_ref, on_ref, s0_ref, ya_ref, sn_ref, *, L, C):
    nsub = C // SUB
    nchunks = L // C
    lbl = lbl_ref[...]
    e = jnp.exp(lbl - jnp.max(lbl, axis=0, keepdims=True))
    lb = e[0:1] / jnp.sum(e, axis=0, keepdims=True)

    t2 = lax.broadcasted_iota(jnp.int32, (2 * C, C), 0)
    s2 = lax.broadcasted_iota(jnp.int32, (2 * C, C), 1)
    limit = jnp.where(t2 < C, t2 + 1, jnp.bitwise_and(t2 - C, -SUB))
    tril2 = jnp.where(s2 < limit, 1.0, 0.0).astype(BF16)
    tq = lax.broadcasted_iota(jnp.int32, (C, C), 0)
    ts = lax.broadcasted_iota(jnp.int32, (C, C), 1)
    causal = ts <= tq
    srow = lax.broadcasted_iota(jnp.int32, (C, HEAD_W), 0)
    pad_rows = HEAD_W - C
    onorm = on_ref[...]

    def chunk(c, st):
        r0 = pl.multiple_of(c * C, C)
        f = f_ref[0, pl.ds(r0, C), :]
        fg = lb + (1.0 - lb) * _sigmoid(f)
        logf = jnp.log(fg)
        kk = 1.0 - fg
        hi = logf.astype(BF16)
        r1 = logf - hi.astype(F32)
        mid = r1.astype(BF16)
        lo = (r1 - mid.astype(F32)).astype(BF16)
        cum = jnp.dot(tril2, jnp.concatenate([hi, mid, lo], axis=1), preferred_element_type=F32)
        bcum = cum[:, :HEAD_W] + cum[:, HEAD_W:2 * HEAD_W] + cum[:, 2 * HEAD_W:]
        b = bcum[:C]
        r = bcum[C:]
        b_last = b[C - 1:C, :]
        q = q_ref[0, 0, pl.ds(r0, C), :].astype(F32)
        v = i_ref[0, 0, pl.ds(r0, C), :]
        q_inter = (q * jnp.exp(b)).astype(BF16)
        q_intra = (q * jnp.exp(b - r)).astype(BF16)
        k_state = kk * jnp.exp(b_last - b)
        rows = []
        for i in range(nsub):
            dec = jnp.exp(r[SUB * i:SUB * i + 1, :] - b)
            if i < nsub - 1:
                dec = jnp.where(srow < SUB * (i + 1), dec, 0.0)
            k_i = (kk * dec).astype(BF16)
            rows.append(lax.dot_general(q_intra[SUB * i:SUB * (i + 1)], k_i,
                                        (((1,), (1,)), ((), ())), preferred_element_type=F32))
        scores = rows[0] if nsub == 1 else jnp.concatenate(rows, axis=0)
        scores = jnp.where(causal, scores, 0.0).astype(BF16)
        o = jnp.dot(scores, v, preferred_element_type=F32)
        o = o + lax.dot_general(q_inter, st.astype(BF16), (((1,), (1,)), ((), ())),
                                preferred_element_type=F32)
        vpad = v.astype(F32)
        kpad = k_state
        if pad_rows:
            zpad = jnp.zeros((pad_rows, HEAD_W), F32)
            vpad = jnp.concatenate([vpad, zpad], axis=0)
            kpad = jnp.concatenate([kpad, zpad], axis=0)
        st = st * jnp.exp(b_last) + jnp.dot(vpad.T.astype(BF16), kpad.astype(BF16),
                                            preferred_element_type=F32)
        g = g_ref[0, 0, pl.ds(r0, C), :].astype(F32)
        ya_ref[0, pl.ds(r0, C), :] = (_rms(o, onorm) * (g * _sigmoid(g))).astype(BF16)
        return st

    st = lax.fori_loop(0, nchunks, chunk, s0_ref[0, 0].T)
    sn_ref[0, 0] = st.T


def _hgrn(z4, f3, lb_logits, onorm, s0, *, L):
    B = z4.shape[1]
    C = min(CHUNK, L)
    R = lb_logits.shape[0]
    slot = lambda s: pl.BlockSpec((1, 1, L, HEAD_W), lambda b, h: (s, b, 0, h))
    return pl.pallas_call(
        functools.partial(_hgrn_kernel, L=L, C=C),
        grid=(B, N_HEADS),
        in_specs=[slot(Z_HQ),
                  pl.BlockSpec((1, L, HEAD_W), lambda b, h: (b, 0, h)),
                  slot(Z_HI), slot(Z_HG),
                  pl.BlockSpec((R, HEAD_W), lambda b, h: (0, h)),
                  pl.BlockSpec((1, HEAD_W), lambda b, h: (0, 0)),
                  pl.BlockSpec((1, 1, HEAD_W, HEAD_W), lambda b, h: (b, h, 0, 0))],
        out_specs=[pl.BlockSpec((1, L, HEAD_W), lambda b, h: (b, 0, h)),
                   pl.BlockSpec((1, 1, HEAD_W, HEAD_W), lambda b, h: (b, h, 0, 0))],
        out_shape=[jax.ShapeDtypeStruct((B, L, D_MODEL), BF16),
                   jax.ShapeDtypeStruct((B, N_HEADS, HEAD_W, HEAD_W), F32)],
        compiler_params=_cparams(("arbitrary", "arbitrary")),
        name="hgrn",
    )(z4, f3, z4, z4, lb_logits, onorm, s0)


def _diff_lambda(lp_ref):
    lp = lp_ref[...]
    a = jnp.sum(lp[0:1] * lp[1:2], axis=-1, keepdims=True)
    b = jnp.sum(lp[2:3] * lp[3:4], axis=-1, keepdims=True)
    return jnp.exp(a) - jnp.exp(b) + LAMBDA_INIT


def _split_maps(q):
    lane = lax.broadcasted_iota(jnp.int32, q.shape, 1)
    return jnp.where(lane < DIFF_DH, q, 0.0), jnp.where(lane >= DIFF_DH, q, 0.0)


def _diff_attn_prompt_kernel(q_ref, k_ref, v_ref, bias_ref, lp_ref, sub_ref, o_ref, kb, vT, acc, *, T, nk):
    qi = pl.program_id(2)

    @pl.when(qi == 0)
    def _():
        for j in range(nk):
            kb[j] = k_ref[0, 0, j * T:(j + 1) * T, :].astype(BF16)
            vT[j] = v_ref[0, 0, j * T:(j + 1) * T, :].T.astype(BF16)

    qa, qb = _split_maps(q_ref[0, 0].astype(F32) * DIFF_SCALE)
    qT = (qa.T.astype(BF16), qb.T.astype(BF16))
    acc[...] = jnp.zeros(acc.shape, F32)

    def tile(j, carry):
        kt = kb[j]
        vt = vT[j]
        bias = bias_ref[0, jnp.maximum(j - qi + 2, 0)]
        out = []
        for mp in range(2):
            m, l = carry[2 * mp], carry[2 * mp + 1]
            s = jnp.dot(kt, qT[mp], preferred_element_type=F32) + bias
            mn = jnp.maximum(m, jnp.max(s, axis=0, keepdims=True))
            p = jnp.exp(s - mn)
            a = jnp.exp(m - mn)
            l = a * l + jnp.sum(p, axis=0, keepdims=True)
            acc[mp] = a * acc[mp] + jnp.dot(vt, p.astype(BF16), preferred_element_type=F32)
            out += [mn, l]
        return tuple(out)

    m0 = jnp.full((1, T), NEG, F32)
    l0 = jnp.zeros((1, T), F32)
    m1, l1, m2, l2 = lax.fori_loop(0, qi + 1, tile, (m0, l0, m0, l0))
    lam = _diff_lambda(lp_ref)
    o = (acc[0] / l1 - lam * (acc[1] / l2)).T
    o_ref[0] = (_rms(o, sub_ref[...]) * (1.0 - LAMBDA_INIT)).astype(BF16)


def _diff_attn_prompt(z4, k4, v4, bias, lp, subln, *, T):
    B, H, L, _ = k4.shape
    nk = L // T
    kv_spec = pl.BlockSpec((1, 1, L, HEAD_W), lambda b, h, qi: (b, h, 0, 0))
    return pl.pallas_call(
        functools.partial(_diff_attn_prompt_kernel, T=T, nk=nk),
        grid=(B, H, nk),
        in_specs=[pl.BlockSpec((1, 1, T, HEAD_W), lambda b, h, qi: (Z_DQ, b, qi, h)),
                  kv_spec, kv_spec,
                  pl.BlockSpec((1, 3, T, T), lambda b, h, qi: (h, 0, 0, 0)),
                  pl.BlockSpec(lp.shape, lambda b, h, qi: (0, 0)),
                  pl.BlockSpec((1, HEAD_W), lambda b, h, qi: (0, 0))],
        out_specs=pl.BlockSpec((1, T, HEAD_W), lambda b, h, qi: (b, qi, h)),
        out_shape=jax.ShapeDtypeStruct((B, L, D_MODEL), BF16),
        scratch_shapes=[pltpu.VMEM((nk, T, HEAD_W), BF16),
                        pltpu.VMEM((nk, HEAD_W, T), BF16),
                        pltpu.VMEM((2, HEAD_W, T), F32)],
        compiler_params=_cparams(("arbitrary", "arbitrary", "arbitrary")),
        name="diff_attn_prompt",
    )(z4, k4, v4, bias, lp, subln)


def _diff_attn_sample_kernel(q_ref, kn_ref, vn_ref, kc_ref, vc_ref, bp_ref, bn_ref, lp_ref, sub_ref, o_ref,
                             *, Ls, hb):
    lam = _diff_lambda(lp_ref)
    nt = (((1,), (1,)), ((), ()))
    for hh in range(hb):
        cols = slice(hh * HEAD_W, (hh + 1) * HEAD_W)
        qa, qb = _split_maps(q_ref[0, 0, :, cols].astype(F32) * DIFF_SCALE)
        q2 = jnp.concatenate([qa, qb], axis=0).astype(BF16)
        kp = kc_ref[0, hh].astype(BF16)
        vp = vc_ref[0, hh].astype(BF16)
        kn = kn_ref[0, :, cols].astype(BF16)
        vn = vn_ref[0, :, cols].astype(BF16)
        s = lax.dot_general(q2, kp, nt, preferred_element_type=F32) + bp_ref[hh]
        sn = lax.dot_general(q2, kn, nt, preferred_element_type=F32) + bn_ref[hh]
        m = jnp.maximum(jnp.max(s, axis=-1, keepdims=True), jnp.max(sn, axis=-1, keepdims=True))
        p = jnp.exp(s - m)
        pn = jnp.exp(sn - m)
        l = jnp.sum(p, axis=-1, keepdims=True) + jnp.sum(pn, axis=-1, keepdims=True)
        o2 = (jnp.dot(p.astype(BF16), vp, preferred_element_type=F32)
              + jnp.dot(pn.astype(BF16), vn, preferred_element_type=F32)) / l
        o = o2[:Ls] - lam * o2[Ls:]
        o_ref[0, :, cols] = (_rms(o, sub_ref[...]) * (1.0 - LAMBDA_INIT)).astype(BF16)


def _diff_attn_sample(z4, kn3, vn3, kc, vc, bias_p, bias_n, lp, subln, *, hb):
    B, H, P, _ = kc.shape
    Ls = kn3.shape[1]
    W = hb * HEAD_W
    cache_spec = pl.BlockSpec((1, hb, P, HEAD_W), lambda b, g: (b, g, 0, 0))
    new_spec = pl.BlockSpec((1, Ls, W), lambda b, g: (b, 0, g))
    return pl.pallas_call(
        functools.partial(_diff_attn_sample_kernel, Ls=Ls, hb=hb),
        grid=(B, H // hb),
        in_specs=[pl.BlockSpec((1, 1, Ls, W), lambda b, g: (Z_DQ, b, 0, g)),
                  new_spec, new_spec, cache_spec, cache_spec,
                  pl.BlockSpec((hb, 2 * Ls, P), lambda b, g: (g, 0, 0)),
                  pl.BlockSpec((hb, 2 * Ls, Ls), lambda b, g: (g, 0, 0)),
                  pl.BlockSpec(lp.shape, lambda b, g: (0, 0)),
                  pl.BlockSpec((1, HEAD_W), lambda b, g: (0, 0))],
        out_specs=pl.BlockSpec((1, Ls, W), lambda b, g: (b, 0, g)),
        out_shape=jax.ShapeDtypeStruct((B, Ls, D_MODEL), BF16),
        compiler_params=_cparams(("arbitrary", "arbitrary")),
        name="diff_attn_sample",
    )(z4, kn3, vn3, kc, vc, bias_p, bias_n, lp, subln)


def _mem_kv_kernel(m_ref, g_ref, w_ref, k_ref, v_ref):
    h = _rms(m_ref[0], g_ref[...]).astype(BF16)
    kv = jnp.dot(h, w_ref[...], preferred_element_type=F32)
    for hh in range(MEM_HEADS):
        k_ref[0, hh] = kv[:, hh * MEM_DH:(hh + 1) * MEM_DH]
        v_ref[0, hh] = kv[:, D_MODEL + hh * MEM_DH:D_MODEL + (hh + 1) * MEM_DH]


def _mem_kv(mem, g, w_bf):
    B, M, _ = mem.shape
    out = jax.ShapeDtypeStruct((B, MEM_HEADS, M, MEM_DH), F32)
    spec = pl.BlockSpec((1, MEM_HEADS, M, MEM_DH), lambda b: (b, 0, 0, 0))
    return pl.pallas_call(
        _mem_kv_kernel,
        grid=(B,),
        in_specs=[pl.BlockSpec((1, M, D_MODEL), lambda b: (b, 0, 0)),
                  pl.BlockSpec((1, D_MODEL), lambda b: (0, 0)),
                  pl.BlockSpec(w_bf.shape, lambda b: (0, 0))],
        out_specs=[spec, spec],
        out_shape=[out, out],
        compiler_params=_cparams(("arbitrary",)),
        name="mem_kv",
    )(mem, g, w_bf)


def _mem_attn_kernel(q_ref, k_ref, v_ref, o_ref):
    nt = (((1,), (1,)), ((), ()))
    for hh in range(MEM_HEADS):
        cols = slice(hh * MEM_DH, (hh + 1) * MEM_DH)
        q = q_ref[0, 0, :, cols]
        s = lax.dot_general(q, k_ref[0, hh].astype(BF16), nt, preferred_element_type=F32) * (MEM_DH ** -0.5)
        p = jnp.exp(s - jnp.max(s, axis=-1, keepdims=True))
        p = p / jnp.sum(p, axis=-1, keepdims=True)
        o = jnp.dot(p.astype(BF16), v_ref[0, hh].astype(BF16), preferred_element_type=F32)
        o_ref[0, :, cols] = o.astype(BF16)


def _mem_attn(z4, mk, mv, *, tl):
    _, B, L, _ = z4.shape
    M = mk.shape[2]
    kv_spec = pl.BlockSpec((1, MEM_HEADS, M, MEM_DH), lambda b, t: (b, 0, 0, 0))
    return pl.pallas_call(
        _mem_attn_kernel,
        grid=(B, L // tl),
        in_specs=[pl.BlockSpec((1, 1, tl, D_MODEL), lambda b, t: (Z_MQ, b, t, 0)), kv_spec, kv_spec],
        out_specs=pl.BlockSpec((1, tl, D_MODEL), lambda b, t: (b, t, 0)),
        out_shape=jax.ShapeDtypeStruct((B, L, D_MODEL), BF16),
        compiler_params=_cparams(("arbitrary", "arbitrary")),
        name="mem_attn",
    )(z4, mk, mv)


def _merge_kernel(x_ref, ya_ref, yb_ref, yc_ref, g0_ref, g1_ref, g2_ref, wb_ref, wo_ref, nf_ref,
                  x1_ref, h2_ref):
    merged = None
    for n, (y_ref, gate_ref) in enumerate(((ya_ref, g0_ref), (yb_ref, g1_ref), (yc_ref, g2_ref))):
        proj = jnp.dot(y_ref[...], wb_ref[n], preferred_element_type=F32)
        term = proj * _sigmoid(gate_ref[0].astype(F32))
        merged = term if merged is None else merged + term
    x1 = x_ref[...] + jnp.dot(merged.astype(BF16), wo_ref[...], preferred_element_type=F32)
    x1_ref[...] = x1
    h2_ref[...] = _rms(x1, nf_ref[...]).astype(BF16)


def _merge(x2d, ya, yb, yc, z3, wb_bf, wo_bf, nf, *, tm):
    N = x2d.shape[0]
    row = pl.BlockSpec((tm, D_MODEL), lambda i: (i, 0))
    gate = lambda n: pl.BlockSpec((1, tm, D_MODEL), lambda i: (Z_G0 + n, i, 0))
    return pl.pallas_call(
        _merge_kernel,
        grid=(N // tm,),
        in_specs=[row, row, row, row, gate(0), gate(1), gate(2),
                  pl.BlockSpec(wb_bf.shape, lambda i: (0, 0, 0)),
                  pl.BlockSpec(wo_bf.shape, lambda i: (0, 0)),
                  pl.BlockSpec((1, D_MODEL), lambda i: (0, 0))],
        out_specs=[row, row],
        out_shape=[jax.ShapeDtypeStruct((N, D_MODEL), F32), jax.ShapeDtypeStruct((N, D_MODEL), BF16)],
        compiler_params=_cparams(("arbitrary",)),
        name="merge",
    )(x2d, ya, yb, yc, z3, z3, z3, wb_bf, wo_bf, nf)


def _ffn_kernel(h_ref, x_ref, cp_ref, wu_ref, cw_ref, cb_ref, wd_ref, nf_ref, y_ref, cn_ref, carry,
                *, tm, cw, d_ff):
    lt = pl.program_id(1)

    @pl.when(lt == 0)
    def _():
        carry[...] = cp_ref[0]

    h = h_ref[0]
    row = lax.broadcasted_iota(jnp.int32, (tm, cw), 0)
    acc = x_ref[0]
    for jc in range(d_ff // cw):
        halves = []
        for base in (jc * cw, d_ff + jc * cw):
            cols = slice(base, base + cw)
            u = jnp.dot(h, wu_ref[:, cols], preferred_element_type=F32)
            p2 = carry[0:1, cols]
            p1 = carry[1:2, cols]
            u1 = jnp.where(row == 0, p1, pltpu.roll(u, 1, 0))
            u2 = jnp.where(row == 0, p2, jnp.where(row == 1, p1, pltpu.roll(u, 2, 0)))
            tail = u[tm - (FFN_CONV - 1):, :]
            carry[:, cols] = tail
            cn_ref[0, :, cols] = tail
            halves.append(cb_ref[:, cols] + cw_ref[0:1, cols] * u2 + cw_ref[1:2, cols] * u1
                          + cw_ref[2:3, cols] * u)
        gate, val = halves
        act = (gate * _sigmoid(gate) * val).astype(BF16)
        acc = acc + jnp.dot(act, wd_ref[jc * cw:(jc + 1) * cw, :], preferred_element_type=F32)
    y_ref[0] = _rms(acc, nf_ref[...])


def _ffn(h2, x1, conv_prev, wu_bf, conv_w, conv_b, wd_bf, nf, *, tm, cw=256):
    B, L, _ = x1.shape
    d_ff = wd_bf.shape[0]
    row = pl.BlockSpec((1, tm, D_MODEL), lambda b, t: (b, t, 0))
    state = pl.BlockSpec((1, FFN_CONV - 1, 2 * d_ff), lambda b, t: (b, 0, 0))
    const = lambda a: pl.BlockSpec(a.shape, lambda b, t: (0,) * a.ndim)
    return pl.pallas_call(
        functools.partial(_ffn_kernel, tm=tm, cw=cw, d_ff=d_ff),
        grid=(B, L // tm),
        in_specs=[row, row, state, const(wu_bf), const(conv_w), const(conv_b), const(wd_bf), const(nf)],
        out_specs=[row, state],
        out_shape=[jax.ShapeDtypeStruct((B, L, D_MODEL), F32),
                   jax.ShapeDtypeStruct((B, FFN_CONV - 1, 2 * d_ff), F32)],
        scratch_shapes=[pltpu.VMEM((FFN_CONV - 1, 2 * d_ff), F32)],
        compiler_params=_cparams(("arbitrary", "arbitrary")),
        name="ffn",
    )(h2, x1, conv_prev, wu_bf, conv_w, conv_b, wd_bf, nf)


def _layer(x, W, *, hgrn_s0, conv_prev, mem_k, mem_v, attn, tm_proj, tm_merge, tm_ffn, tl_mem, head_layout):
    B, L, _ = x.shape
    N = B * L
    x2d = x.reshape(N, D_MODEL)
    z3, f2d, k_new, v_new = _in_proj(x2d, W["norm_attn"], W["w_in"], tm=tm_proj, rows_per_batch=L,
                                     head_layout=head_layout)
    z4 = z3.reshape(N_ZSLOT, B, L, D_MODEL)
    ya, s_new = _hgrn(z4, f2d.reshape(B, L, D_MODEL), W["lb_logits"], W["hgrn_onorm"], hgrn_s0, L=L)
    yb, k_new, v_new = attn(z4, k_new, v_new)
    yc = _mem_attn(z4, mem_k, mem_v, tl=tl_mem)
    x1, h2 = _merge(x2d, ya.reshape(N, D_MODEL), yb.reshape(N, D_MODEL), yc.reshape(N, D_MODEL), z3,
                    W["w_branch"], W["w_out"], W["norm_ffn"], tm=tm_merge)
    y, conv_new = _ffn(h2.reshape(B, L, D_MODEL), x1.reshape(B, L, D_MODEL), conv_prev, W["w_up"],
                       W["conv_w"], W["conv_b"], W["w_down"], W["norm_final"], tm=tm_ffn)
    return y, k_new, v_new, s_new, conv_new


def kernel(x_prompt, x_sample, mem_prompt, cache_diff_k, cache_diff_v, cache_mem_k, cache_mem_v, state_hgrn, state_ffn_conv, rel_bias, hgrn_lb_logits, norm_attn, w_in, hgrn_onorm, diff_lambda, diff_subln, mem_norm, w_mem_kv, w_branch, w_out, norm_ffn, w_up, conv_w, conv_b, w_down, norm_final):
    assert w_in.shape[0] == 1, "single-layer trunk"
    Bp, Lp, _ = x_prompt.shape
    Bs, Ls, _ = x_sample.shape
    P = cache_diff_k.shape[3]
    d_ff2 = w_up.shape[2]
    row = lambda a: a.reshape(1, -1)
    W = dict(norm_attn=row(norm_attn[0]), w_in=w_in[0].astype(BF16), lb_logits=hgrn_lb_logits,
             hgrn_onorm=row(hgrn_onorm[0]), w_branch=w_branch[0].astype(BF16), w_out=w_out[0].astype(BF16),
             norm_ffn=row(norm_ffn[0]), w_up=w_up[0].astype(BF16), conv_w=conv_w[0], conv_b=row(conv_b[0]),
             w_down=w_down[0].astype(BF16), norm_final=row(norm_final))
    lp = diff_lambda[0]
    subln = row(diff_subln[0])

    T = 256
    bias_p = _bias_prompt(rel_bias, T)
    mk, mv = _mem_kv(mem_prompt, row(mem_norm[0]), w_mem_kv[0].astype(BF16))

    def attn_prompt(z4, k4, v4):
        return _diff_attn_prompt(z4, k4, v4, bias_p, lp, subln, T=T), k4, v4

    yp, pk, pv, ps, pc = _layer(
        x_prompt, W, hgrn_s0=jnp.zeros((Bp, N_HEADS, HEAD_W, HEAD_W), F32),
        conv_prev=jnp.zeros((Bp, FFN_CONV - 1, d_ff2), F32), mem_k=mk, mem_v=mv, attn=attn_prompt,
        tm_proj=512, tm_merge=512, tm_ffn=512, tl_mem=512, head_layout=True)

    bias_sp, bias_sn = _bias_sample(rel_bias, Ls, P)

    def attn_sample(z4, k2d, v2d):
        yb = _diff_attn_sample(z4, k2d.reshape(Bs, Ls, D_MODEL), v2d.reshape(Bs, Ls, D_MODEL),
                               cache_diff_k[0], cache_diff_v[0], bias_sp, bias_sn, lp, subln, hb=4)
        heads = lambda a: jnp.transpose(a.reshape(Bs, Ls, N_HEADS, HEAD_W), (0, 2, 1, 3))
        return yb, heads(k2d), heads(v2d)

    ys, sk, sv, ss, sc = _layer(
        x_sample, W, hgrn_s0=state_hgrn[0], conv_prev=state_ffn_conv[0], mem_k=cache_mem_k[0],
        mem_v=cache_mem_v[0], attn=attn_sample,
        tm_proj=Bs * Ls, tm_merge=Bs * Ls, tm_ffn=Ls, tl_mem=Ls, head_layout=False)

    return (yp, ys, pk[None], pv[None], ps[None], pc[None], mk[None], mv[None],
            sk[None], sv[None], ss[None], sc[None])
```

```python
import functools
import math

import numpy as np
import jax
import jax.numpy as jnp
from jax import lax
from jax.experimental import pallas as pl
from jax.experimental.pallas import tpu as pltpu

F32 = jnp.float32
BF16 = jnp.bfloat16

D_MODEL = 1024
CHUNK = 64
SUB = 16
HEAD_W = 128
N_HEADS = D_MODEL // HEAD_W
DIFF_DH = 64
DIFF_SCALE = DIFF_DH ** -0.5
MEM_HEADS = 4
MEM_DH = D_MODEL // MEM_HEADS
REL_BUCKETS = 32
REL_MAX_DIST = 128
N_BRANCH = 3
FFN_CONV = 3
RMS_EPS = 1e-6
LAMBDA_INIT = 0.8 - 0.6 * math.exp(-0.3 * 0)
NEG = -1e30
LOG2E = math.log2(math.e)
ONES_ROWS = 16

SEC_HQ, SEC_HF, SEC_HI, SEC_HG, SEC_DQ, SEC_DK, SEC_DV, SEC_MQ, SEC_G0 = range(9)
N_SEC = 11
Z_HQ, Z_HI, Z_HG, Z_DQ, Z_MQ, Z_G0 = 0, 1, 2, 3, 4, 5
N_ZSLOT = 8

V7X_VMEM_LIMIT = 56 * 1024 * 1024


def _cparams(sem, vmem=V7X_VMEM_LIMIT):
    return pltpu.CompilerParams(dimension_semantics=sem, vmem_limit_bytes=vmem)


def _sigmoid(x):
    return 1.0 / (1.0 + jnp.exp(-x))


def _rms(x, g):
    ms = jnp.mean(x * x, axis=-1, keepdims=True)
    return x * lax.rsqrt(ms + RMS_EPS) * g


def _np_bucket(rel):
    nb = REL_BUCKETS // 2
    ret = np.where(rel > 0, nb, 0)
    n = np.abs(rel)
    max_exact = nb // 2
    large = max_exact + (np.log(np.maximum(n, 1).astype(np.float32) / max_exact)
                         / math.log(REL_MAX_DIST / max_exact) * (nb - max_exact)).astype(np.int32)
    large = np.minimum(large, nb - 1)
    return ret + np.where(n < max_exact, n, large)


def _bucket_segments(lo, hi):
    rel = np.arange(lo, hi + 1, dtype=np.int32)
    b = _np_bucket(rel)
    change = np.nonzero(np.diff(b))[0]
    return int(b[0]), [(int(rel[i + 1]), int(b[i + 1])) for i in change]


def _bias_from_rel(rel, rb_ref, h, first_bucket, segs):
    val = jnp.full(rel.shape, rb_ref[first_bucket, h], F32)
    for lo, bk in segs:
        val = jnp.where(rel >= lo, rb_ref[bk, h], val)
    return val


def _bias_prompt_kernel(rb_ref, o_ref, *, T, first_bucket, segs):
    h = pl.program_id(0)
    kk = lax.broadcasted_iota(jnp.int32, (T, T), 0)
    qq = lax.broadcasted_iota(jnp.int32, (T, T), 1)
    o_ref[0, 0] = _bias_from_rel(kk - qq - T, rb_ref, h, first_bucket, segs) * LOG2E
    diag = _bias_from_rel(kk - qq, rb_ref, h, first_bucket, segs) * LOG2E
    o_ref[0, 1] = jnp.where(jnp.bitwise_and(kk, -CHUNK) <= qq, diag, NEG)
    o_ref[0, 2] = jnp.full((T, T), NEG, F32)


def _bias_prompt(rel_bias, T):
    H = rel_bias.shape[1]
    first_bucket, segs = _bucket_segments(-2 * T, T)
    assert all(lo > -T for lo, _ in segs), "bias must be constant beyond one tile"
    tiles = pl.pallas_call(
        functools.partial(_bias_prompt_kernel, T=T, first_bucket=first_bucket, segs=segs),
        grid=(H,),
        in_specs=[pl.BlockSpec(memory_space=pltpu.SMEM)],
        out_specs=pl.BlockSpec((1, 3, T, T), lambda h: (h, 0, 0, 0)),
        out_shape=jax.ShapeDtypeStruct((H, 3, T, T), F32),
        compiler_params=_cparams(("arbitrary",)),
        name="bias_prompt",
    )(rel_bias)
    return tiles, first_bucket


def _bias_sample_kernel(rb_ref, bp_ref, bn_ref, *, Ls, P, first_bucket, segs):
    h = pl.program_id(0)
    qq = jnp.bitwise_and(lax.broadcasted_iota(jnp.int32, (2 * Ls, P), 0), Ls - 1)
    kk = lax.broadcasted_iota(jnp.int32, (2 * Ls, P), 1)
    bp_ref[0] = _bias_from_rel(kk - P - qq, rb_ref, h, first_bucket, segs)
    qn = jnp.bitwise_and(lax.broadcasted_iota(jnp.int32, (2 * Ls, Ls), 0), Ls - 1)
    kn = lax.broadcasted_iota(jnp.int32, (2 * Ls, Ls), 1)
    bn_ref[0] = _bias_from_rel(kn - qn, rb_ref, h, first_bucket, segs)


def _bias_sample(rel_bias, Ls, P):
    H = rel_bias.shape[1]
    first_bucket, segs = _bucket_segments(-(P + Ls), Ls)
    return pl.pallas_call(
        functools.partial(_bias_sample_kernel, Ls=Ls, P=P, first_bucket=first_bucket, segs=segs),
        grid=(H,),
        in_specs=[pl.BlockSpec(memory_space=pltpu.SMEM)],
        out_specs=[pl.BlockSpec((1, 2 * Ls, P), lambda h: (h, 0, 0)),
                   pl.BlockSpec((1, 2 * Ls, Ls), lambda h: (h, 0, 0))],
        out_shape=[jax.ShapeDtypeStruct((H, 2 * Ls, P), F32),
                   jax.ShapeDtypeStruct((H, 2 * Ls, Ls), F32)],
        compiler_params=_cparams(("arbitrary",)),
        name="bias_sample",
    )(rel_bias)


def _in_proj_kernel(x_ref, g_ref, w_ref, z_ref, f_ref, k_ref, v_ref, h_scr, *, head_layout):
    j = pl.program_id(1)

    @pl.when(j == 0)
    def _():
        h_scr[...] = _rms(x_ref[...], g_ref[...]).astype(BF16)

    def section():
        return jnp.dot(h_scr[...], w_ref[...], preferred_element_type=F32)

    def store_heads(ref, acc):
        if head_layout:
            for hh in range(N_HEADS):
                ref[0, hh] = acc[:, hh * HEAD_W:(hh + 1) * HEAD_W]
        else:
            ref[...] = acc

    @pl.when(j == SEC_HF)
    def _():
        f_ref[...] = section()

    @pl.when(j == SEC_DK)
    def _():
        store_heads(k_ref, section())

    @pl.when(j == SEC_DV)
    def _():
        store_heads(v_ref, section())

    @pl.when((j != SEC_HF) & (j != SEC_DK) & (j != SEC_DV))
    def _():
        z_ref[0] = section().astype(BF16)


def _zslot(j):
    return (j - (j >= SEC_HF).astype(jnp.int32) - (j >= SEC_DK).astype(jnp.int32)
            - (j >= SEC_DV).astype(jnp.int32))


def _in_proj(x2d, g, w_bf, *, tm, rows_per_batch, head_layout):
    N = x2d.shape[0]
    nt = N // tm
    if head_layout:
        B = N // rows_per_batch
        nlt = rows_per_batch // tm
        kv_shape = jax.ShapeDtypeStruct((B, N_HEADS, rows_per_batch, HEAD_W), F32)
        kv_spec = pl.BlockSpec((1, N_HEADS, tm, HEAD_W), lambda i, j: (i // nlt, 0, i % nlt, 0))
    else:
        kv_shape = jax.ShapeDtypeStruct((N, D_MODEL), F32)
        kv_spec = pl.BlockSpec((tm, D_MODEL), lambda i, j: (i, 0))
    return pl.pallas_call(
        functools.partial(_in_proj_kernel, head_layout=head_layout),
        grid=(nt, N_SEC),
        in_specs=[pl.BlockSpec((tm, D_MODEL), lambda i, j: (i, 0)),
                  pl.BlockSpec((1, D_MODEL), lambda i, j: (0, 0)),
                  pl.BlockSpec((D_MODEL, D_MODEL), lambda i, j: (0, j))],
        out_specs=[pl.BlockSpec((1, tm, D_MODEL), lambda i, j: (_zslot(j), i, 0)),
                   pl.BlockSpec((tm, D_MODEL), lambda i, j: (i, 0)),
                   kv_spec, kv_spec],
        out_shape=[jax.ShapeDtypeStruct((N_ZSLOT, N, D_MODEL), BF16),
                   jax.ShapeDtypeStruct((N, D_MODEL), F32),
                   kv_shape, kv_shape],
        scratch_shapes=[pltpu.VMEM((tm, D_MODEL), BF16)],
        compiler_params=_cparams(("arbitrary", "arbitrary")),
        name="in_proj",
    )(x2d, g, w_bf)


def _hgrn_kernel(q_ref, f_ref, i_ref, g_ref, lbl_ref, on_ref, s0_ref, ya_ref, sn_ref, *, L, C):
    nsub = C // SUB
    nchunks = L // C
    lbl = lbl_ref[...]
    e = jnp.exp(lbl - jnp.max(lbl, axis=0, keepdims=True))
    lb = e[0:1] / jnp.sum(e, axis=0, keepdims=True)

    t2 = lax.broadcasted_iota(jnp.int32, (2 * C, C), 0)
    s2 = lax.broadcasted_iota(jnp.int32, (2 * C, C), 1)
    limit = jnp.where(t2 < C, t2 + 1, jnp.bitwise_and(t2 - C, -SUB))
    tril2 = jnp.where(s2 < limit, 1.0, 0.0).astype(BF16)
    tq = lax.broadcasted_iota(jnp.int32, (C, C), 0)
    ts = lax.broadcasted_iota(jnp.int32, (C, C), 1)
    causal = ts <= tq
    srow = lax.broadcasted_iota(jnp.int32, (C, HEAD_W), 0)
    pad_rows = HEAD_W - C
    onorm = on_ref[...]

    def chunk(c, st):
        r0 = pl.multiple_of(c * C, C)
        f = f_ref[0, pl.ds(r0, C), :]
        fg = lb + (1.0 - lb) * _sigmoid(f)
        logf = jnp.log(fg)
        kk = 1.0 - fg
        hi = logf.astype(BF16)
        r1 = logf - hi.astype(F32)
        mid = r1.astype(BF16)
        lo = (r1 - mid.astype(F32)).astype(BF16)
        cum = jnp.dot(tril2, jnp.concatenate([hi, mid, lo], axis=1), preferred_element_type=F32)
        bcum = cum[:, :HEAD_W] + cum[:, HEAD_W:2 * HEAD_W] + cum[:, 2 * HEAD_W:]
        b = bcum[:C]
        r = bcum[C:]
        b_last = b[C - 1:C, :]
        q = q_ref[0, 0, pl.ds(r0, C), :].astype(F32)
        v = i_ref[0, 0, pl.ds(r0, C), :]
        q_inter = (q * jnp.exp(b)).astype(BF16)
        q_intra = (q * jnp.exp(b - r)).astype(BF16)
        k_state = kk * jnp.exp(b_last - b)
        rows = []
        for i in range(nsub):
            dec = jnp.exp(r[SUB * i:SUB * i + 1, :] - b)
            if i < nsub - 1:
                dec = jnp.where(srow < SUB * (i + 1), dec, 0.0)
            k_i = (kk * dec).astype(BF16)
            rows.append(lax.dot_general(q_intra[SUB * i:SUB * (i + 1)], k_i,
                                        (((1,), (1,)), ((), ())), preferred_element_type=F32))
        scores = rows[0] if nsub == 1 else jnp.concatenate(rows, axis=0)
        scores = jnp.where(causal, scores, 0.0).astype(BF16)
        o = jnp.dot(scores, v, preferred_element_type=F32)
        o = o + lax.dot_general(q_inter, st.astype(BF16), (((1,), (1,)), ((), ())),
                                preferred_element_type=F32)
        vpad = v.astype(F32)
        kpad = k_state
        if pad_rows:
            zpad = jnp.zeros((pad_rows, HEAD_W), F32)
            vpad = jnp.concatenate([vpad, zpad], axis=0)
            kpad = jnp.concatenate([kpad, zpad], axis=0)
        st = st * jnp.exp(b_last) + jnp.dot(vpad.T.astype(BF16), kpad.astype(BF16),
                                            preferred_element_type=F32)
        g = g_ref[0, 0, pl.ds(r0, C), :].astype(F32)
        ya_ref[0, pl.ds(r0, C), :] = (_rms(o, onorm) * (g * _sigmoid(g))).astype(BF16)
        return st

    st = lax.fori_loop(0, nchunks, chunk, s0_ref[0, 0].T)
    sn_ref[0, 0] = st.T


def _hgrn(z4, f3, lb_logits, onorm, s0, *, L):
    B = z4.shape[1]
    C = min(CHUNK, L)
    R = lb_logits.shape[0]
    slot = lambda s: pl.BlockSpec((1, 1, L, HEAD_W), lambda b, h: (s, b, 0, h))
    return pl.pallas_call(
        functools.partial(_hgrn_kernel, L=L, C=C),
        grid=(B, N_HEADS),
        in_specs=[slot(Z_HQ),
                  pl.BlockSpec((1, L, HEAD_W), lambda b, h: (b, 0, h)),
                  slot(Z_HI), slot(Z_HG),
                  pl.BlockSpec((R, HEAD_W), lambda b, h: (0, h)),
                  pl.BlockSpec((1, HEAD_W), lambda b, h: (0, 0)),
                  pl.BlockSpec((1, 1, HEAD_W, HEAD_W), lambda b, h: (b, h, 0, 0))],
        out_specs=[pl.BlockSpec((1, L, HEAD_W), lambda b, h: (b, 0, h)),
                   pl.BlockSpec((1, 1, HEAD_W, HEAD_W), lambda b, h: (b, h, 0, 0))],
        out_shape=[jax.ShapeDtypeStruct((B, L, D_MODEL), BF16),
                   jax.ShapeDtypeStruct((B, N_HEADS, HEAD_W, HEAD_W), F32)],
        compiler_params=_cparams(("arbitrary", "arbitrary")),
        name="hgrn",
    )(z4, f3, z4, z4, lb_logits, onorm, s0)


def _diff_lambda(lp_ref):
    lp = lp_ref[...]
    a = jnp.sum(lp[0:1] * lp[1:2], axis=-1, keepdims=True)
    b = jnp.sum(lp[2:3] * lp[3:4], axis=-1, keepdims=True)
    return jnp.exp(a) - jnp.exp(b) + LAMBDA_INIT


def _split_maps(q):
    lane = lax.broadcasted_iota(jnp.int32, q.shape, 1)
    return jnp.where(lane < DIFF_DH, q, 0.0), jnp.where(lane >= DIFF_DH, q, 0.0)


def _diff_attn_prompt_kernel(rb_ref, q_ref, k_ref, v_ref, bias_ref, lp_ref, sub_ref, o_ref, kb, vT, acc, m_scr,
                             *, T, nk, G, far_bucket):
    hg = pl.program_id(1)
    qi = pl.program_id(2)
    nt = (((1,), (1,)), ((), ()))

    @pl.when(qi == 0)
    def _():
        ones = jnp.ones((ONES_ROWS, T), BF16)
        for g in range(G):
            for j in range(nk):
                kb[g, j] = (k_ref[0, g, j * T:(j + 1) * T, :] * (DIFF_SCALE * LOG2E)).astype(BF16)
                vT[g, j, :HEAD_W, :] = v_ref[0, g, j * T:(j + 1) * T, :].T.astype(BF16)
                vT[g, j, HEAD_W:, :] = ones

    qs = []
    for g in range(G):
        qa, qb = _split_maps(q_ref[0, 0, :, g * HEAD_W:(g + 1) * HEAD_W])
        qs.append((qa, qb))
    acc[...] = jnp.zeros(acc.shape, F32)
    m_scr[...] = jnp.full(m_scr.shape, NEG, F32)

    def tile(j, bias_slot):
        for g in range(G):
            kt = kb[g, j]
            vt = vT[g, j]
            if bias_slot is None:
                shift = rb_ref[far_bucket, hg * G + g] * LOG2E
            else:
                bias = bias_ref[g, bias_slot]
            for mp in range(2):
                idx = 2 * g + mp
                m = m_scr[idx]
                s = lax.dot_general(kt, qs[g][mp], nt, preferred_element_type=F32)
                if bias_slot is None:
                    mn = jnp.maximum(m, jnp.max(s, axis=0, keepdims=True) + shift)
                    p = jnp.exp2(s - (mn - shift))
                else:
                    s = s + bias
                    mn = jnp.maximum(m, jnp.max(s, axis=0, keepdims=True))
                    p = jnp.exp2(s - mn)
                m_scr[idx] = mn
                acc[idx] = jnp.exp2(m - mn) * acc[idx] + jnp.dot(vt, p.astype(BF16),
                                                                  preferred_element_type=F32)

    def far(j, c):
        tile(j, None)
        return c

    lax.fori_loop(0, qi - 1, far, 0)

    @pl.when(qi >= 1)
    def _():
        tile(qi - 1, 0)

    tile(qi, 1)
    lam = _diff_lambda(lp_ref)
    for g in range(G):
        a1 = acc[2 * g]
        a2 = acc[2 * g + 1]
        o = (a1[:HEAD_W] / a1[HEAD_W:HEAD_W + 1] - lam * (a2[:HEAD_W] / a2[HEAD_W:HEAD_W + 1])).T
        o_ref[0, :, g * HEAD_W:(g + 1) * HEAD_W] = (_rms(o, sub_ref[...]) * (1.0 - LAMBDA_INIT)).astype(BF16)


def _diff_attn_prompt(z4, k4, v4, rel_bias, bias, lp, subln, *, T, G, far_bucket):
    B, H, L, _ = k4.shape
    nk = L // T
    W = G * HEAD_W
    kv_spec = pl.BlockSpec((1, G, L, HEAD_W), lambda b, h, qi: (b, h, 0, 0))
    return pl.pallas_call(
        functools.partial(_diff_attn_prompt_kernel, T=T, nk=nk, G=G, far_bucket=far_bucket),
        grid=(B, H // G, nk),
        in_specs=[pl.BlockSpec(memory_space=pltpu.SMEM),
                  pl.BlockSpec((1, 1, T, W), lambda b, h, qi: (Z_DQ, b, qi, h)),
                  kv_spec, kv_spec,
                  pl.BlockSpec((G, 3, T, T), lambda b, h, qi: (h, 0, 0, 0)),
                  pl.BlockSpec(lp.shape, lambda b, h, qi: (0, 0)),
                  pl.BlockSpec((1, HEAD_W), lambda b, h, qi: (0, 0))],
        out_specs=pl.BlockSpec((1, T, W), lambda b, h, qi: (b, qi, h)),
        out_shape=jax.ShapeDtypeStruct((B, L, D_MODEL), BF16),
        scratch_shapes=[pltpu.VMEM((G, nk, T, HEAD_W), BF16),
                        pltpu.VMEM((G, nk, HEAD_W + ONES_ROWS, T), BF16),
                        pltpu.VMEM((2 * G, HEAD_W + ONES_ROWS, T), F32),
                        pltpu.VMEM((2 * G, 1, T), F32)],
        compiler_params=_cparams(("arbitrary", "arbitrary", "arbitrary")),
        name="diff_attn_prompt",
    )(rel_bias, z4, k4, v4, bias, lp, subln)


def _diff_attn_sample_kernel(q_ref, kn_ref, vn_ref, kc_ref, vc_ref, bp_ref, bn_ref, lp_ref, sub_ref, o_ref,
                             *, Ls, hb):
    lam = _diff_lambda(lp_ref)
    nt = (((1,), (1,)), ((), ()))
    for hh in range(hb):
        cols = slice(hh * HEAD_W, (hh + 1) * HEAD_W)
        qa, qb = _split_maps(q_ref[0, 0, :, cols].astype(F32) * DIFF_SCALE)
        q2 = jnp.concatenate([qa, qb], axis=0).astype(BF16)
        kp = kc_ref[0, hh].astype(BF16)
        vp = vc_ref[0, hh].astype(BF16)
        kn = kn_ref[0, :, cols].astype(BF16)
        vn = vn_ref[0, :, cols].astype(BF16)
        s = lax.dot_general(q2, kp, nt, preferred_element_type=F32) + bp_ref[hh]
        sn = lax.dot_general(q2, kn, nt, preferred_element_type=F32) + bn_ref[hh]
        m = jnp.maximum(jnp.max(s, axis=-1, keepdims=True), jnp.max(sn, axis=-1, keepdims=True))
        p = jnp.exp(s - m)
        pn = jnp.exp(sn - m)
        l = jnp.sum(p, axis=-1, keepdims=True) + jnp.sum(pn, axis=-1, keepdims=True)
        o2 = (jnp.dot(p.astype(BF16), vp, preferred_element_type=F32)
              + jnp.dot(pn.astype(BF16), vn, preferred_element_type=F32)) / l
        o = o2[:Ls] - lam * o2[Ls:]
        o_ref[0, :, cols] = (_rms(o, sub_ref[...]) * (1.0 - LAMBDA_INIT)).astype(BF16)


def _diff_attn_sample(z4, kn3, vn3, kc, vc, bias_p, bias_n, lp, subln, *, hb):
    B, H, P, _ = kc.shape
    Ls = kn3.shape[1]
    W = hb * HEAD_W
    cache_spec = pl.BlockSpec((1, hb, P, HEAD_W), lambda b, g: (b, g, 0, 0))
    new_spec = pl.BlockSpec((1, Ls, W), lambda b, g: (b, 0, g))
    return pl.pallas_call(
        functools.partial(_diff_attn_sample_kernel, Ls=Ls, hb=hb),
        grid=(B, H // hb),
        in_specs=[pl.BlockSpec((1, 1, Ls, W), lambda b, g: (Z_DQ, b, 0, g)),
                  new_spec, new_spec, cache_spec, cache_spec,
                  pl.BlockSpec((hb, 2 * Ls, P), lambda b, g: (g, 0, 0)),
                  pl.BlockSpec((hb, 2 * Ls, Ls), lambda b, g: (g, 0, 0)),
                  pl.BlockSpec(lp.shape, lambda b, g: (0, 0)),
                  pl.BlockSpec((1, HEAD_W), lambda b, g: (0, 0))],
        out_specs=pl.BlockSpec((1, Ls, W), lambda b, g: (b, 0, g)),
        out_shape=jax.ShapeDtypeStruct((B, Ls, D_MODEL), BF16),
        compiler_params=_cparams(("arbitrary", "arbitrary")),
        name="diff_attn_sample",
    )(z4, kn3, vn3, kc, vc, bias_p, bias_n, lp, subln)


def _mem_kv_kernel(m_ref, g_ref, w_ref, k_ref, v_ref):
    h = _rms(m_ref[0], g_ref[...]).astype(BF16)
    kv = jnp.dot(h, w_ref[...], preferred_element_type=F32)
    for hh in range(MEM_HEADS):
        k_ref[0, hh] = kv[:, hh * MEM_DH:(hh + 1) * MEM_DH]
        v_ref[0, hh] = kv[:, D_MODEL + hh * MEM_DH:D_MODEL + (hh + 1) * MEM_DH]


def _mem_kv(mem, g, w_bf):
    B, M, _ = mem.shape
    out = jax.ShapeDtypeStruct((B, MEM_HEADS, M, MEM_DH), F32)
    spec = pl.BlockSpec((1, MEM_HEADS, M, MEM_DH), lambda b: (b, 0, 0, 0))
    return pl.pallas_call(
        _mem_kv_kernel,
        grid=(B,),
        in_specs=[pl.BlockSpec((1, M, D_MODEL), lambda b: (b, 0, 0)),
                  pl.BlockSpec((1, D_MODEL), lambda b: (0, 0)),
                  pl.BlockSpec(w_bf.shape, lambda b: (0, 0))],
        out_specs=[spec, spec],
        out_shape=[out, out],
        compiler_params=_cparams(("arbitrary",)),
        name="mem_kv",
    )(mem, g, w_bf)


def _mem_attn_kernel(q_ref, k_ref, v_ref, o_ref):
    nt = (((1,), (1,)), ((), ()))
    for hh in range(MEM_HEADS):
        cols = slice(hh * MEM_DH, (hh + 1) * MEM_DH)
        q = q_ref[0, 0, :, cols]
        s = lax.dot_general(q, k_ref[0, hh].astype(BF16), nt, preferred_element_type=F32) * (MEM_DH ** -0.5)
        p = jnp.exp(s - jnp.max(s, axis=-1, keepdims=True))
        p = p / jnp.sum(p, axis=-1, keepdims=True)
        o = jnp.dot(p.astype(BF16), v_ref[0, hh].astype(BF16), preferred_element_type=F32)
        o_ref[0, :, cols] = o.astype(BF16)


def _mem_attn(z4, mk, mv, *, tl):
    _, B, L, _ = z4.shape
    M = mk.shape[2]
    kv_spec = pl.BlockSpec((1, MEM_HEADS, M, MEM_DH), lambda b, t: (b, 0, 0, 0))
    return pl.pallas_call(
        _mem_attn_kernel,
        grid=(B, L // tl),
        in_specs=[pl.BlockSpec((1, 1, tl, D_MODEL), lambda b, t: (Z_MQ, b, t, 0)), kv_spec, kv_spec],
        out_specs=pl.BlockSpec((1, tl, D_MODEL), lambda b, t: (b, t, 0)),
        out_shape=jax.ShapeDtypeStruct((B, L, D_MODEL), BF16),
        compiler_params=_cparams(("arbitrary", "arbitrary")),
        name="mem_attn",
    )(z4, mk, mv)


def _merge_kernel(x_ref, ya_ref, yb_ref, yc_ref, g0_ref, g1_ref, g2_ref, wb_ref, wo_ref, nf_ref,
                  x1_ref, h2_ref):
    merged = None
    for n, (y_ref, gate_ref) in enumerate(((ya_ref, g0_ref), (yb_ref, g1_ref), (yc_ref, g2_ref))):
        proj = jnp.dot(y_ref[...], wb_ref[n], preferred_element_type=F32)
        term = proj * _sigmoid(gate_ref[0].astype(F32))
        merged = term if merged is None else merged + term
    x1 = x_ref[...] + jnp.dot(merged.astype(BF16), wo_ref[...], preferred_element_type=F32)
    x1_ref[...] = x1
    h2_ref[...] = _rms(x1, nf_ref[...]).astype(BF16)


def _merge(x2d, ya, yb, yc, z3, wb_bf, wo_bf, nf, *, tm):
    N = x2d.shape[0]
    row = pl.BlockSpec((tm, D_MODEL), lambda i: (i, 0))
    gate = lambda n: pl.BlockSpec((1, tm, D_MODEL), lambda i: (Z_G0 + n, i, 0))
    return pl.pallas_call(
        _merge_kernel,
        grid=(N // tm,),
        in_specs=[row, row, row, row, gate(0), gate(1), gate(2),
                  pl.BlockSpec(wb_bf.shape, lambda i: (0, 0, 0)),
                  pl.BlockSpec(wo_bf.shape, lambda i: (0, 0)),
                  pl.BlockSpec((1, D_MODEL), lambda i: (0, 0))],
        out_specs=[row, row],
        out_shape=[jax.ShapeDtypeStruct((N, D_MODEL), F32), jax.ShapeDtypeStruct((N, D_MODEL), BF16)],
        compiler_params=_cparams(("arbitrary",)),
        name="merge",
    )(x2d, ya, yb, yc, z3, z3, z3, wb_bf, wo_bf, nf)


def _ffn_kernel(h_ref, x_ref, cp_ref, wu_ref, cw_ref, cb_ref, wd_ref, nf_ref, y_ref, cn_ref, carry,
                *, tm, cw, d_ff):
    lt = pl.program_id(1)

    @pl.when(lt == 0)
    def _():
        carry[...] = cp_ref[0]

    h = h_ref[0]
    row = lax.broadcasted_iota(jnp.int32, (tm, cw), 0)
    acc = x_ref[0]
    for jc in range(d_ff // cw):
        halves = []
        for base in (jc * cw, d_ff + jc * cw):
            cols = slice(base, base + cw)
            u = jnp.dot(h, wu_ref[:, cols], preferred_element_type=F32)
            p2 = carry[0:1, cols]
            p1 = carry[1:2, cols]
            u1 = jnp.where(row == 0, p1, pltpu.roll(u, 1, 0))
            u2 = jnp.where(row == 0, p2, jnp.where(row == 1, p1, pltpu.roll(u, 2, 0)))
            tail = u[tm - (FFN_CONV - 1):, :]
            carry[:, cols] = tail
            cn_ref[0, :, cols] = tail
            halves.append(cb_ref[:, cols] + cw_ref[0:1, cols] * u2 + cw_ref[1:2, cols] * u1
                          + cw_ref[2:3, cols] * u)
        gate, val = halves
        act = (gate * _sigmoid(gate) * val).astype(BF16)
        acc = acc + jnp.dot(act, wd_ref[jc * cw:(jc + 1) * cw, :], preferred_element_type=F32)
    y_ref[0] = _rms(acc, nf_ref[...])


def _ffn(h2, x1, conv_prev, wu_bf, conv_w, conv_b, wd_bf, nf, *, tm, cw=256):
    B, L, _ = x1.shape
    d_ff = wd_bf.shape[0]
    row = pl.BlockSpec((1, tm, D_MODEL), lambda b, t: (b, t, 0))
    state = pl.BlockSpec((1, FFN_CONV - 1, 2 * d_ff), lambda b, t: (b, 0, 0))
    const = lambda a: pl.BlockSpec(a.shape, lambda b, t: (0,) * a.ndim)
    return pl.pallas_call(
        functools.partial(_ffn_kernel, tm=tm, cw=cw, d_ff=d_ff),
        grid=(B, L // tm),
        in_specs=[row, row, state, const(wu_bf), const(conv_w), const(conv_b), const(wd_bf), const(nf)],
        out_specs=[row, state],
        out_shape=[jax.ShapeDtypeStruct((B, L, D_MODEL), F32),
                   jax.ShapeDtypeStruct((B, FFN_CONV - 1, 2 * d_ff), F32)],
        scratch_shapes=[pltpu.VMEM((FFN_CONV - 1, 2 * d_ff), F32)],
        compiler_params=_cparams(("arbitrary", "arbitrary")),
        name="ffn",
    )(h2, x1, conv_prev, wu_bf, conv_w, conv_b, wd_bf, nf)


def _layer(x, W, *, hgrn_s0, conv_prev, mem_k, mem_v, attn, tm_proj, tm_merge, tm_ffn, tl_mem, head_layout):
    B, L, _ = x.shape
    N = B * L
    x2d = x.reshape(N, D_MODEL)
    z3, f2d, k_new, v_new = _in_proj(x2d, W["norm_attn"], W["w_in"], tm=tm_proj, rows_per_batch=L,
                                     head_layout=head_layout)
    z4 = z3.reshape(N_ZSLOT, B, L, D_MODEL)
    ya, s_new = _hgrn(z4, f2d.reshape(B, L, D_MODEL), W["lb_logits"], W["hgrn_onorm"], hgrn_s0, L=L)
    yb, k_new, v_new = attn(z4, k_new, v_new)
    yc = _mem_attn(z4, mem_k, mem_v, tl=tl_mem)
    x1, h2 = _merge(x2d, ya.reshape(N, D_MODEL), yb.reshape(N, D_MODEL), yc.reshape(N, D_MODEL), z3,
                    W["w_branch"], W["w_out"], W["norm_ffn"], tm=tm_merge)
    y, conv_new = _ffn(h2.reshape(B, L, D_MODEL), x1.reshape(B, L, D_MODEL), conv_prev, W["w_up"],
                       W["conv_w"], W["conv_b"], W["w_down"], W["norm_final"], tm=tm_ffn)
    return y, k_new, v_new, s_new, conv_new


def kernel(x_prompt, x_sample, mem_prompt, cache_diff_k, cache_diff_v, cache_mem_k, cache_mem_v, state_hgrn, state_ffn_conv, rel_bias, hgrn_lb_logits, norm_attn, w_in, hgrn_onorm, diff_lambda, diff_subln, mem_norm, w_mem_kv, w_branch, w_out, norm_ffn, w_up, conv_w, conv_b, w_down, norm_final):
    assert w_in.shape[0] == 1, "single-layer trunk"
    Bp, Lp, _ = x_prompt.shape
    Bs, Ls, _ = x_sample.shape
    P = cache_diff_k.shape[3]
    d_ff2 = w_up.shape[2]
    row = lambda a: a.reshape(1, -1)
    W = dict(norm_attn=row(norm_attn[0]), w_in=w_in[0].astype(BF16), lb_logits=hgrn_lb_logits,
             hgrn_onorm=row(hgrn_onorm[0]), w_branch=w_branch[0].astype(BF16), w_out=w_out[0].astype(BF16),
             norm_ffn=row(norm_ffn[0]), w_up=w_up[0].astype(BF16), conv_w=conv_w[0], conv_b=row(conv_b[0]),
             w_down=w_down[0].astype(BF16), norm_final=row(norm_final))
    lp = diff_lambda[0]
    subln = row(diff_subln[0])

    T = 512
    bias_p, far_bucket = _bias_prompt(rel_bias, T)
    mk, mv = _mem_kv(mem_prompt, row(mem_norm[0]), w_mem_kv[0].astype(BF16))

    def attn_prompt(z4, k4, v4):
        yb = _diff_attn_prompt(z4, k4, v4, rel_bias, bias_p, lp, subln, T=T, G=2, far_bucket=far_bucket)
        return yb, k4, v4

    yp, pk, pv, ps, pc = _layer(
        x_prompt, W, hgrn_s0=jnp.zeros((Bp, N_HEADS, HEAD_W, HEAD_W), F32),
        conv_prev=jnp.zeros((Bp, FFN_CONV - 1, d_ff2), F32), mem_k=mk, mem_v=mv, attn=attn_prompt,
        tm_proj=512, tm_merge=512, tm_ffn=512, tl_mem=512, head_layout=True)

    bias_sp, bias_sn = _bias_sample(rel_bias, Ls, P)

    def attn_sample(z4, k2d, v2d):
        yb = _diff_attn_sample(z4, k2d.reshape(Bs, Ls, D_MODEL), v2d.reshape(Bs, Ls, D_MODEL),
                               cache_diff_k[0], cache_diff_v[0], bias_sp, bias_sn, lp, subln, hb=4)
        heads = lambda a: jnp.transpose(a.reshape(Bs, Ls, N_HEADS, HEAD_W), (0, 2, 1, 3))
        return yb, heads(k2d), heads(v2d)

    ys, sk, sv, ss, sc = _layer(
        x_sample, W, hgrn_s0=state_hgrn[0], conv_prev=state_ffn_conv[0], mem_k=cache_mem_k[0],
        mem_v=cache_mem_v[0], attn=attn_sample,
        tm_proj=Bs * Ls, tm_merge=Bs * Ls, tm_ffn=Ls, tl_mem=Ls, head_layout=False)

    return (yp, ys, pk[None], pv[None], ps[None], pc[None], mk[None], mv[None],
            sk[None], sv[None], ss[None], sc[None])
```

```python
import functools
import math

import numpy as np
import jax
import jax.numpy as jnp
from jax import lax
from jax.experimental import pallas as pl
from jax.experimental.pallas import tpu as pltpu

F32 = jnp.float32
BF16 = jnp.bfloat16

D_MODEL = 1024
CHUNK = 64
SUB = 16
HEAD_W = 128
N_HEADS = D_MODEL // HEAD_W
PAIR_W = 2 * HEAD_W
DIFF_DH = 64
DIFF_SCALE = DIFF_DH ** -0.5
MEM_HEADS = 4
MEM_DH = D_MODEL // MEM_HEADS
REL_BUCKETS = 32
REL_MAX_DIST = 128
N_BRANCH = 3
FFN_CONV = 3
RMS_EPS = 1e-6
LAMBDA_INIT = 0.8 - 0.6 * math.exp(-0.3 * 0)
NEG = -1e30
LOG2E = math.log2(math.e)
ONES_ROWS = 16

SEC_HQ, SEC_HF, SEC_HI, SEC_HG, SEC_DQ, SEC_DK, SEC_DV, SEC_MQ, SEC_G0 = range(9)
N_SEC = 11
Z_HQ, Z_HI, Z_HG, Z_DQ, Z_MQ, Z_G0 = 0, 1, 2, 3, 4, 5
N_ZSLOT = 8

V7X_VMEM_LIMIT = 56 * 1024 * 1024


def _cparams(sem, vmem=V7X_VMEM_LIMIT):
    return pltpu.CompilerParams(dimension_semantics=sem, vmem_limit_bytes=vmem)


def _sigmoid(x):
    return 0.5 * jnp.tanh(0.5 * x) + 0.5


def _rms(x, g):
    ms = jnp.mean(x * x, axis=-1, keepdims=True)
    return x * lax.rsqrt(ms + RMS_EPS) * g


def _np_bucket(rel):
    nb = REL_BUCKETS // 2
    ret = np.where(rel > 0, nb, 0)
    n = np.abs(rel)
    max_exact = nb // 2
    large = max_exact + (np.log(np.maximum(n, 1).astype(np.float32) / max_exact)
                         / math.log(REL_MAX_DIST / max_exact) * (nb - max_exact)).astype(np.int32)
    large = np.minimum(large, nb - 1)
    return ret + np.where(n < max_exact, n, large)


def _bucket_segments(lo, hi):
    rel = np.arange(lo, hi + 1, dtype=np.int32)
    b = _np_bucket(rel)
    change = np.nonzero(np.diff(b))[0]
    return int(b[0]), [(int(rel[i + 1]), int(b[i + 1])) for i in change]


def _bias_from_rel(rel, rb_ref, h, first_bucket, segs):
    val = jnp.full(rel.shape, rb_ref[first_bucket, h], F32)
    for lo, bk in segs:
        val = jnp.where(rel >= lo, rb_ref[bk, h], val)
    return val


def _bias_prompt_kernel(rb_ref, o_ref, *, T, first_bucket, segs):
    h = pl.program_id(0)
    kk = lax.broadcasted_iota(jnp.int32, (T, T), 0)
    qq = lax.broadcasted_iota(jnp.int32, (T, T), 1)
    o_ref[0, 0] = _bias_from_rel(kk - qq - T, rb_ref, h, first_bucket, segs) * LOG2E
    diag = _bias_from_rel(kk - qq, rb_ref, h, first_bucket, segs) * LOG2E
    o_ref[0, 1] = jnp.where(jnp.bitwise_and(kk, -CHUNK) <= qq, diag, NEG)
    o_ref[0, 2] = jnp.full((T, T), NEG, F32)


def _bias_prompt(rel_bias, T):
    H = rel_bias.shape[1]
    first_bucket, segs = _bucket_segments(-2 * T, T)
    assert all(lo > -T for lo, _ in segs), "bias must be constant beyond one tile"
    tiles = pl.pallas_call(
        functools.partial(_bias_prompt_kernel, T=T, first_bucket=first_bucket, segs=segs),
        grid=(H,),
        in_specs=[pl.BlockSpec(memory_space=pltpu.SMEM)],
        out_specs=pl.BlockSpec((1, 3, T, T), lambda h: (h, 0, 0, 0)),
        out_shape=jax.ShapeDtypeStruct((H, 3, T, T), F32),
        compiler_params=_cparams(("arbitrary",)),
        name="bias_prompt",
    )(rel_bias)
    return tiles, first_bucket


def _bias_sample_kernel(rb_ref, bp_ref, bn_ref, *, Ls, P, first_bucket, segs):
    h = pl.program_id(0)
    qq = jnp.bitwise_and(lax.broadcasted_iota(jnp.int32, (2 * Ls, P), 0), Ls - 1)
    kk = lax.broadcasted_iota(jnp.int32, (2 * Ls, P), 1)
    bp_ref[0] = _bias_from_rel(kk - P - qq, rb_ref, h, first_bucket, segs)
    qn = jnp.bitwise_and(lax.broadcasted_iota(jnp.int32, (2 * Ls, Ls), 0), Ls - 1)
    kn = lax.broadcasted_iota(jnp.int32, (2 * Ls, Ls), 1)
    bn_ref[0] = _bias_from_rel(kn - qn, rb_ref, h, first_bucket, segs)


def _bias_sample(rel_bias, Ls, P):
    H = rel_bias.shape[1]
    first_bucket, segs = _bucket_segments(-(P + Ls), Ls)
    return pl.pallas_call(
        functools.partial(_bias_sample_kernel, Ls=Ls, P=P, first_bucket=first_bucket, segs=segs),
        grid=(H,),
        in_specs=[pl.BlockSpec(memory_space=pltpu.SMEM)],
        out_specs=[pl.BlockSpec((1, 2 * Ls, P), lambda h: (h, 0, 0)),
                   pl.BlockSpec((1, 2 * Ls, Ls), lambda h: (h, 0, 0))],
        out_shape=[jax.ShapeDtypeStruct((H, 2 * Ls, P), F32),
                   jax.ShapeDtypeStruct((H, 2 * Ls, Ls), F32)],
        compiler_params=_cparams(("arbitrary",)),
        name="bias_sample",
    )(rel_bias)


def _in_proj_kernel(x_ref, g_ref, w_ref, z_ref, f_ref, k_ref, v_ref, h_scr, *, head_layout):
    j = pl.program_id(1)

    @pl.when(j == 0)
    def _():
        h_scr[...] = _rms(x_ref[...], g_ref[...]).astype(BF16)

    def section():
        return jnp.dot(h_scr[...], w_ref[...], preferred_element_type=F32)

    def store_heads(ref, acc):
        if head_layout:
            for hh in range(N_HEADS):
                ref[0, hh] = acc[:, hh * HEAD_W:(hh + 1) * HEAD_W]
        else:
            ref[...] = acc

    @pl.when(j == SEC_HF)
    def _():
        f_ref[...] = section()

    @pl.when(j == SEC_DK)
    def _():
        store_heads(k_ref, section())

    @pl.when(j == SEC_DV)
    def _():
        store_heads(v_ref, section())

    @pl.when((j != SEC_HF) & (j != SEC_DK) & (j != SEC_DV))
    def _():
        z_ref[0] = section().astype(BF16)


def _zslot(j):
    return (j - (j >= SEC_HF).astype(jnp.int32) - (j >= SEC_DK).astype(jnp.int32)
            - (j >= SEC_DV).astype(jnp.int32))


def _in_proj(x2d, g, w_bf, *, tm, rows_per_batch, head_layout):
    N = x2d.shape[0]
    nt = N // tm
    if head_layout:
        B = N // rows_per_batch
        nlt = rows_per_batch // tm
        kv_shape = jax.ShapeDtypeStruct((B, N_HEADS, rows_per_batch, HEAD_W), F32)
        kv_spec = pl.BlockSpec((1, N_HEADS, tm, HEAD_W), lambda i, j: (i // nlt, 0, i % nlt, 0))
    else:
        kv_shape = jax.ShapeDtypeStruct((N, D_MODEL), F32)
        kv_spec = pl.BlockSpec((tm, D_MODEL), lambda i, j: (i, 0))
    return pl.pallas_call(
        functools.partial(_in_proj_kernel, head_layout=head_layout),
        grid=(nt, N_SEC),
        in_specs=[pl.BlockSpec((tm, D_MODEL), lambda i, j: (i, 0)),
                  pl.BlockSpec((1, D_MODEL), lambda i, j: (0, 0)),
                  pl.BlockSpec((D_MODEL, D_MODEL), lambda i, j: (0, j))],
        out_specs=[pl.BlockSpec((1, tm, D_MODEL), lambda i, j: (_zslot(j), i, 0)),
                   pl.BlockSpec((tm, D_MODEL), lambda i, j: (i, 0)),
                   kv_spec, kv_spec],
        out_shape=[jax.ShapeDtypeStruct((N_ZSLOT, N, D_MODEL), BF16),
                   jax.ShapeDtypeStruct((N, D_MODEL), F32),
                   kv_shape, kv_shape],
        scratch_shapes=[pltpu.VMEM((tm, D_MODEL), BF16)],
        compiler_params=_cparams(("arbitrary", "arbitrary")),
        name="in_proj",
    )(x2d, g, w_bf)


def _block_diag(a, b):
    za = jnp.zeros((a.shape[0], b.shape[1]), a.dtype)
    zb = jnp.zeros((b.shape[0], a.shape[1]), b.dtype)
    return jnp.concatenate([jnp.concatenate([a, za], axis=1), jnp.concatenate([zb, b], axis=1)], axis=0)


def _hgrn_kernel(q_ref, f_ref, i_ref, g_ref, lbl_ref, on_ref, s0_ref, ya_ref, sn_ref, st_scr, *, TL, C):
    lt = pl.program_id(1)
    nsub = C // SUB
    lbl = lbl_ref[...]
    e = jnp.exp(lbl - jnp.max(lbl, axis=0, keepdims=True))
    lb_all = e[0:1] / jnp.sum(e, axis=0, keepdims=True)

    @pl.when(lt == 0)
    def _():
        for p in range(N_HEADS // 2):
            st_scr[p] = jnp.concatenate([s0_ref[0, 2 * p].T, s0_ref[0, 2 * p + 1].T], axis=1)

    t2 = lax.broadcasted_iota(jnp.int32, (2 * C, C), 0)
    s2 = lax.broadcasted_iota(jnp.int32, (2 * C, C), 1)
    limit = jnp.where(t2 < C, t2 + 1, jnp.bitwise_and(t2 - C, -SUB))
    tril2 = jnp.where(s2 < limit, 1.0, 0.0).astype(BF16)
    width = max(nsub * C, HEAD_W)
    wrow = lax.broadcasted_iota(jnp.int32, (C, width), 0)
    wcol = lax.broadcasted_iota(jnp.int32, (C, width), 1)
    base = jnp.bitwise_and(wrow, -SUB) * (C // SUB)
    keep = (wcol >= base) & (wcol <= base + wrow)
    keep2 = jnp.concatenate([keep, keep], axis=1)
    plane = lax.broadcasted_iota(jnp.int32, (HEAD_W, PAIR_W), 1)
    onorm = on_ref[...]
    nt = (((1,), (1,)), ((), ()))

    def chunk(c, carry):
        r0 = pl.multiple_of(c * C, C)
        f = f_ref[0, pl.ds(r0, C), :]
        fg = lb_all + (1.0 - lb_all) * _sigmoid(f)
        logf = jnp.log(fg)
        kk = 1.0 - fg
        hi = logf.astype(BF16)
        lo = (logf - hi.astype(F32)).astype(BF16)
        cum = jnp.dot(tril2, jnp.concatenate([hi, lo], axis=1), preferred_element_type=F32)
        bcum = cum[:, :D_MODEL] + cum[:, D_MODEL:]
        b = bcum[:C]
        r = bcum[C:]
        b_last = b[C - 1:C, :]
        q = q_ref[0, 0, pl.ds(r0, C), :].astype(F32)
        q_inter = (q * jnp.exp(b)).astype(BF16)
        q_intra = (q * jnp.exp(b - r)).astype(BF16)
        k_state = (kk * jnp.exp(b_last - b)).astype(BF16)
        k_sub = []
        for i in range(nsub):
            n = SUB * (i + 1)
            ki = (kk[:n] * jnp.exp(r[SUB * i:SUB * i + 1, :] - b[:n])).astype(BF16)
            if n < C:
                ki = jnp.concatenate([ki, jnp.zeros((C - n, D_MODEL), BF16)], axis=0)
            k_sub.append(ki)
        st_decay = jnp.exp(b_last)
        g = g_ref[0, 0, pl.ds(r0, C), :].astype(F32)
        gate = g * _sigmoid(g)
        for p in range(N_HEADS // 2):
            pc = slice(p * PAIR_W, (p + 1) * PAIR_W)
            hc = [slice((2 * p + a) * HEAD_W, (2 * p + a + 1) * HEAD_W) for a in range(2)]
            k4 = []
            vT = []
            kpad = []
            for a in range(2):
                pieces = [k[:, hc[a]] for k in k_sub]
                if nsub * C < width:
                    pieces.append(jnp.zeros((width - nsub * C, HEAD_W), BF16))
                k4.append(jnp.concatenate(pieces, axis=0))
                v = i_ref[0, 0, pl.ds(r0, C), hc[a]].astype(F32)
                vrows = [v] * (HEAD_W // C) if nsub > 1 else [v, jnp.zeros((HEAD_W - C, HEAD_W), F32)]
                vT.append(jnp.concatenate(vrows, axis=0).T.astype(BF16))
                kpad.append(jnp.concatenate([k_state[:, hc[a]], jnp.zeros((HEAD_W - C, HEAD_W), BF16)], axis=0))
            sc = lax.dot_general(q_intra[:, pc], _block_diag(k4[0], k4[1]), nt,
                                 preferred_element_type=F32)
            w = jnp.where(keep2, sc, 0.0).astype(BF16)
            stp = st_scr[p]
            stb = stp.astype(BF16)
            vfull = [jnp.concatenate([t] * (width // HEAD_W), axis=1) for t in vT]
            zw = jnp.zeros((HEAD_W, width), BF16)
            rhs_o = jnp.concatenate(
                [jnp.concatenate([vfull[0], zw, jnp.where(plane < HEAD_W, stb, 0.0)], axis=1),
                 jnp.concatenate([zw, vfull[1], jnp.where(plane >= HEAD_W, stb, 0.0)], axis=1)], axis=0)
            o2 = lax.dot_general(jnp.concatenate([w, q_inter[:, pc]], axis=1), rhs_o, nt,
                                 preferred_element_type=F32)
            upd = jnp.dot(jnp.concatenate(vT, axis=1), _block_diag(kpad[0], kpad[1]),
                          preferred_element_type=F32)
            st_scr[p] = stp * st_decay[:, pc] + upd
            for a in range(2):
                o = o2[:, a * HEAD_W:(a + 1) * HEAD_W]
                ya_ref[0, pl.ds(r0, C), hc[a]] = (_rms(o, onorm) * gate[:, hc[a]]).astype(BF16)
        return carry

    lax.fori_loop(0, TL // C, chunk, 0)

    @pl.when(lt == pl.num_programs(1) - 1)
    def _():
        for p in range(N_HEADS // 2):
            stp = st_scr[p]
            for a in range(2):
                sn_ref[0, 2 * p + a] = stp[:, a * HEAD_W:(a + 1) * HEAD_W].T


def _hgrn(z4, f3, lb_logits, onorm, s0, *, L, tl):
    B = z4.shape[1]
    C = min(CHUNK, L)
    slot = lambda s: pl.BlockSpec((1, 1, tl, D_MODEL), lambda b, t: (s, b, t, 0))
    row = pl.BlockSpec((1, tl, D_MODEL), lambda b, t: (b, t, 0))
    state = pl.BlockSpec((1, N_HEADS, HEAD_W, HEAD_W), lambda b, t: (b, 0, 0, 0))
    return pl.pallas_call(
        functools.partial(_hgrn_kernel, TL=tl, C=C),
        grid=(B, L // tl),
        in_specs=[slot(Z_HQ), row, slot(Z_HI), slot(Z_HG),
                  pl.BlockSpec(lb_logits.shape, lambda b, t: (0, 0)),
                  pl.BlockSpec((1, HEAD_W), lambda b, t: (0, 0)),
                  state],
        out_specs=[row, state],
        out_shape=[jax.ShapeDtypeStruct((B, L, D_MODEL), BF16),
                   jax.ShapeDtypeStruct((B, N_HEADS, HEAD_W, HEAD_W), F32)],
        scratch_shapes=[pltpu.VMEM((N_HEADS // 2, HEAD_W, PAIR_W), F32)],
        compiler_params=_cparams(("arbitrary", "arbitrary")),
        name="hgrn",
    )(z4, f3, z4, z4, lb_logits, onorm, s0)


def _diff_lambda(lp_ref):
    lp = lp_ref[...]
    a = jnp.sum(lp[0:1] * lp[1:2], axis=-1, keepdims=True)
    b = jnp.sum(lp[2:3] * lp[3:4], axis=-1, keepdims=True)
    return jnp.exp(a) - jnp.exp(b) + LAMBDA_INIT


def _split_maps(q):
    lane = lax.broadcasted_iota(jnp.int32, q.shape, 1)
    return jnp.where(lane < DIFF_DH, q, 0.0), jnp.where(lane >= DIFF_DH, q, 0.0)


def _diff_attn_prompt_kernel(rb_ref, q_ref, k_ref, v_ref, bias_ref, lp_ref, sub_ref, o_ref, kb, vT, acc, m_scr,
                             *, T, nk, G, far_bucket):
    hg = pl.program_id(1)
    qi = pl.program_id(2)
    nt = (((1,), (1,)), ((), ()))

    @pl.when(qi == 0)
    def _():
        ones = jnp.ones((ONES_ROWS, T), BF16)
        for g in range(G):
            for j in range(nk):
                kb[g, j] = (k_ref[0, g, j * T:(j + 1) * T, :] * (DIFF_SCALE * LOG2E)).astype(BF16)
                vT[g, j, :HEAD_W, :] = v_ref[0, g, j * T:(j + 1) * T, :].T.astype(BF16)
                vT[g, j, HEAD_W:, :] = ones

    qs = []
    for g in range(G):
        qa, qb = _split_maps(q_ref[0, 0, :, g * HEAD_W:(g + 1) * HEAD_W])
        qs.append((qa, qb))
    acc[...] = jnp.zeros(acc.shape, F32)
    m_scr[...] = jnp.full(m_scr.shape, NEG, F32)

    def tile(j, bias_slot):
        for g in range(G):
            kt = kb[g, j]
            vt = vT[g, j]
            if bias_slot is None:
                shift = rb_ref[far_bucket, hg * G + g] * LOG2E
            else:
                bias = bias_ref[g, bias_slot]
            for mp in range(2):
                idx = 2 * g + mp
                m = m_scr[idx]
                s = lax.dot_general(kt, qs[g][mp], nt, preferred_element_type=F32)
                if bias_slot is None:
                    mn = jnp.maximum(m, jnp.max(s, axis=0, keepdims=True) + shift)
                    p = jnp.exp2(s - (mn - shift))
                else:
                    s = s + bias
                    mn = jnp.maximum(m, jnp.max(s, axis=0, keepdims=True))
                    p = jnp.exp2(s - mn)
                m_scr[idx] = mn
                acc[idx] = jnp.exp2(m - mn) * acc[idx] + jnp.dot(vt, p.astype(BF16),
                                                                  preferred_element_type=F32)

    def far(j, c):
        tile(j, None)
        return c

    lax.fori_loop(0, qi - 1, far, 0)

    @pl.when(qi >= 1)
    def _():
        tile(qi - 1, 0)

    tile(qi, 1)
    lam = _diff_lambda(lp_ref)
    for g in range(G):
        a1 = acc[2 * g]
        a2 = acc[2 * g + 1]
        o = (a1[:HEAD_W] / a1[HEAD_W:HEAD_W + 1] - lam * (a2[:HEAD_W] / a2[HEAD_W:HEAD_W + 1])).T
        o_ref[0, :, g * HEAD_W:(g + 1) * HEAD_W] = (_rms(o, sub_ref[...]) * (1.0 - LAMBDA_INIT)).astype(BF16)


def _diff_attn_prompt(z4, k4, v4, rel_bias, bias, lp, subln, *, T, G, far_bucket):
    B, H, L, _ = k4.shape
    nk = L // T
    W = G * HEAD_W
    kv_spec = pl.BlockSpec((1, G, L, HEAD_W), lambda b, h, qi: (b, h, 0, 0))
    return pl.pallas_call(
        functools.partial(_diff_attn_prompt_kernel, T=T, nk=nk, G=G, far_bucket=far_bucket),
        grid=(B, H // G, nk),
        in_specs=[pl.BlockSpec(memory_space=pltpu.SMEM),
                  pl.BlockSpec((1, 1, T, W), lambda b, h, qi: (Z_DQ, b, qi, h)),
                  kv_spec, kv_spec,
                  pl.BlockSpec((G, 3, T, T), lambda b, h, qi: (h, 0, 0, 0)),
                  pl.BlockSpec(lp.shape, lambda b, h, qi: (0, 0)),
                  pl.BlockSpec((1, HEAD_W), lambda b, h, qi: (0, 0))],
        out_specs=pl.BlockSpec((1, T, W), lambda b, h, qi: (b, qi, h)),
        out_shape=jax.ShapeDtypeStruct((B, L, D_MODEL), BF16),
        scratch_shapes=[pltpu.VMEM((G, nk, T, HEAD_W), BF16),
                        pltpu.VMEM((G, nk, HEAD_W + ONES_ROWS, T), BF16),
                        pltpu.VMEM((2 * G, HEAD_W + ONES_ROWS, T), F32),
                        pltpu.VMEM((2 * G, 1, T), F32)],
        compiler_params=_cparams(("arbitrary", "arbitrary", "arbitrary")),
        name="diff_attn_prompt",
    )(rel_bias, z4, k4, v4, bias, lp, subln)


def _diff_attn_sample_kernel(q_ref, kn_ref, vn_ref, kc_ref, vc_ref, bp_ref, bn_ref, lp_ref, sub_ref, o_ref,
                             *, Ls, hb):
    lam = _diff_lambda(lp_ref)
    nt = (((1,), (1,)), ((), ()))
    for hh in range(hb):
        cols = slice(hh * HEAD_W, (hh + 1) * HEAD_W)
        qa, qb = _split_maps(q_ref[0, 0, :, cols].astype(F32) * DIFF_SCALE)
        q2 = jnp.concatenate([qa, qb], axis=0).astype(BF16)
        kp = kc_ref[0, hh].astype(BF16)
        vp = vc_ref[0, hh].astype(BF16)
        kn = kn_ref[0, :, cols].astype(BF16)
        vn = vn_ref[0, :, cols].astype(BF16)
        s = lax.dot_general(q2, kp, nt, preferred_element_type=F32) + bp_ref[hh]
        sn = lax.dot_general(q2, kn, nt, preferred_element_type=F32) + bn_ref[hh]
        m = jnp.maximum(jnp.max(s, axis=-1, keepdims=True), jnp.max(sn, axis=-1, keepdims=True))
        p = jnp.exp(s - m)
        pn = jnp.exp(sn - m)
        l = jnp.sum(p, axis=-1, keepdims=True) + jnp.sum(pn, axis=-1, keepdims=True)
        o2 = (jnp.dot(p.astype(BF16), vp, preferred_element_type=F32)
              + jnp.dot(pn.astype(BF16), vn, preferred_element_type=F32)) / l
        o = o2[:Ls] - lam * o2[Ls:]
        o_ref[0, :, cols] = (_rms(o, sub_ref[...]) * (1.0 - LAMBDA_INIT)).astype(BF16)


def _diff_attn_sample(z4, kn3, vn3, kc, vc, bias_p, bias_n, lp, subln, *, hb):
    B, H, P, _ = kc.shape
    Ls = kn3.shape[1]
    W = hb * HEAD_W
    cache_spec = pl.BlockSpec((1, hb, P, HEAD_W), lambda b, g: (b, g, 0, 0))
    new_spec = pl.BlockSpec((1, Ls, W), lambda b, g: (b, 0, g))
    return pl.pallas_call(
        functools.partial(_diff_attn_sample_kernel, Ls=Ls, hb=hb),
        grid=(B, H // hb),
        in_specs=[pl.BlockSpec((1, 1, Ls, W), lambda b, g: (Z_DQ, b, 0, g)),
                  new_spec, new_spec, cache_spec, cache_spec,
                  pl.BlockSpec((hb, 2 * Ls, P), lambda b, g: (g, 0, 0)),
                  pl.BlockSpec((hb, 2 * Ls, Ls), lambda b, g: (g, 0, 0)),
                  pl.BlockSpec(lp.shape, lambda b, g: (0, 0)),
                  pl.BlockSpec((1, HEAD_W), lambda b, g: (0, 0))],
        out_specs=pl.BlockSpec((1, Ls, W), lambda b, g: (b, 0, g)),
        out_shape=jax.ShapeDtypeStruct((B, Ls, D_MODEL), BF16),
        compiler_params=_cparams(("arbitrary", "arbitrary")),
        name="diff_attn_sample",
    )(z4, kn3, vn3, kc, vc, bias_p, bias_n, lp, subln)


def _mem_kv_kernel(m_ref, g_ref, w_ref, k_ref, v_ref):
    h = _rms(m_ref[0], g_ref[...]).astype(BF16)
    kv = jnp.dot(h, w_ref[...], preferred_element_type=F32)
    for hh in range(MEM_HEADS):
        k_ref[0, hh] = kv[:, hh * MEM_DH:(hh + 1) * MEM_DH]
        v_ref[0, hh] = kv[:, D_MODEL + hh * MEM_DH:D_MODEL + (hh + 1) * MEM_DH]


def _mem_kv(mem, g, w_bf):
    B, M, _ = mem.shape
    out = jax.ShapeDtypeStruct((B, MEM_HEADS, M, MEM_DH), F32)
    spec = pl.BlockSpec((1, MEM_HEADS, M, MEM_DH), lambda b: (b, 0, 0, 0))
    return pl.pallas_call(
        _mem_kv_kernel,
        grid=(B,),
        in_specs=[pl.BlockSpec((1, M, D_MODEL), lambda b: (b, 0, 0)),
                  pl.BlockSpec((1, D_MODEL), lambda b: (0, 0)),
                  pl.BlockSpec(w_bf.shape, lambda b: (0, 0))],
        out_specs=[spec, spec],
        out_shape=[out, out],
        compiler_params=_cparams(("arbitrary",)),
        name="mem_kv",
    )(mem, g, w_bf)


def _mem_attn_kernel(q_ref, k_ref, v_ref, o_ref):
    nt = (((1,), (1,)), ((), ()))
    for hh in range(MEM_HEADS):
        cols = slice(hh * MEM_DH, (hh + 1) * MEM_DH)
        q = q_ref[0, 0, :, cols]
        s = lax.dot_general(q, k_ref[0, hh].astype(BF16), nt, preferred_element_type=F32) * (MEM_DH ** -0.5)
        p = jnp.exp(s - jnp.max(s, axis=-1, keepdims=True))
        p = p / jnp.sum(p, axis=-1, keepdims=True)
        o = jnp.dot(p.astype(BF16), v_ref[0, hh].astype(BF16), preferred_element_type=F32)
        o_ref[0, :, cols] = o.astype(BF16)


def _mem_attn(z4, mk, mv, *, tl):
    _, B, L, _ = z4.shape
    M = mk.shape[2]
    kv_spec = pl.BlockSpec((1, MEM_HEADS, M, MEM_DH), lambda b, t: (b, 0, 0, 0))
    return pl.pallas_call(
        _mem_attn_kernel,
        grid=(B, L // tl),
        in_specs=[pl.BlockSpec((1, 1, tl, D_MODEL), lambda b, t: (Z_MQ, b, t, 0)), kv_spec, kv_spec],
        out_specs=pl.BlockSpec((1, tl, D_MODEL), lambda b, t: (b, t, 0)),
        out_shape=jax.ShapeDtypeStruct((B, L, D_MODEL), BF16),
        compiler_params=_cparams(("arbitrary", "arbitrary")),
        name="mem_attn",
    )(z4, mk, mv)


def _merge_kernel(x_ref, ya_ref, yb_ref, yc_ref, g0_ref, g1_ref, g2_ref, wb_ref, wo_ref, nf_ref,
                  x1_ref, h2_ref):
    merged = None
    for n, (y_ref, gate_ref) in enumerate(((ya_ref, g0_ref), (yb_ref, g1_ref), (yc_ref, g2_ref))):
        proj = jnp.dot(y_ref[...], wb_ref[n], preferred_element_type=F32)
        term = proj * _sigmoid(gate_ref[0].astype(F32))
        merged = term if merged is None else merged + term
    x1 = x_ref[...] + jnp.dot(merged.astype(BF16), wo_ref[...], preferred_element_type=F32)
    x1_ref[...] = x1
    h2_ref[...] = _rms(x1, nf_ref[...]).astype(BF16)


def _merge(x2d, ya, yb, yc, z3, wb_bf, wo_bf, nf, *, tm):
    N = x2d.shape[0]
    row = pl.BlockSpec((tm, D_MODEL), lambda i: (i, 0))
    gate = lambda n: pl.BlockSpec((1, tm, D_MODEL), lambda i: (Z_G0 + n, i, 0))
    return pl.pallas_call(
        _merge_kernel,
        grid=(N // tm,),
        in_specs=[row, row, row, row, gate(0), gate(1), gate(2),
                  pl.BlockSpec(wb_bf.shape, lambda i: (0, 0, 0)),
                  pl.BlockSpec(wo_bf.shape, lambda i: (0, 0)),
                  pl.BlockSpec((1, D_MODEL), lambda i: (0, 0))],
        out_specs=[row, row],
        out_shape=[jax.ShapeDtypeStruct((N, D_MODEL), F32), jax.ShapeDtypeStruct((N, D_MODEL), BF16)],
        compiler_params=_cparams(("arbitrary",)),
        name="merge",
    )(x2d, ya, yb, yc, z3, z3, z3, wb_bf, wo_bf, nf)


def _ffn_kernel(h_ref, x_ref, cp_ref, wu_ref, cw_ref, cb_ref, wd_ref, nf_ref, y_ref, cn_ref, carry,
                *, tm, cw, d_ff):
    lt = pl.program_id(1)

    @pl.when(lt == 0)
    def _():
        carry[...] = cp_ref[0]

    h = h_ref[0]
    row = lax.broadcasted_iota(jnp.int32, (tm, cw), 0)
    acc = x_ref[0]
    for jc in range(d_ff // cw):
        halves = []
        for base in (jc * cw, d_ff + jc * cw):
            cols = slice(base, base + cw)
            u = jnp.dot(h, wu_ref[:, cols], preferred_element_type=F32)
            p2 = carry[0:1, cols]
            p1 = carry[1:2, cols]
            u1 = jnp.where(row == 0, p1, pltpu.roll(u, 1, 0))
            u2 = jnp.where(row == 0, p2, jnp.where(row == 1, p1, pltpu.roll(u, 2, 0)))
            tail = u[tm - (FFN_CONV - 1):, :]
            carry[:, cols] = tail
            cn_ref[0, :, cols] = tail
            halves.append(cb_ref[:, cols] + cw_ref[0:1, cols] * u2 + cw_ref[1:2, cols] * u1
                          + cw_ref[2:3, cols] * u)
        gate, val = halves
        act = (gate * _sigmoid(gate) * val).astype(BF16)
        acc = acc + jnp.dot(act, wd_ref[jc * cw:(jc + 1) * cw, :], preferred_element_type=F32)
    y_ref[0] = _rms(acc, nf_ref[...])


def _ffn(h2, x1, conv_prev, wu_bf, conv_w, conv_b, wd_bf, nf, *, tm, cw=256):
    B, L, _ = x1.shape
    d_ff = wd_bf.shape[0]
    row = pl.BlockSpec((1, tm, D_MODEL), lambda b, t: (b, t, 0))
    state = pl.BlockSpec((1, FFN_CONV - 1, 2 * d_ff), lambda b, t: (b, 0, 0))
    const = lambda a: pl.BlockSpec(a.shape, lambda b, t: (0,) * a.ndim)
    return pl.pallas_call(
        functools.partial(_ffn_kernel, tm=tm, cw=cw, d_ff=d_ff),
        grid=(B, L // tm),
        in_specs=[row, row, state, const(wu_bf), const(conv_w), const(conv_b), const(wd_bf), const(nf)],
        out_specs=[row, state],
        out_shape=[jax.ShapeDtypeStruct((B, L, D_MODEL), F32),
                   jax.ShapeDtypeStruct((B, FFN_CONV - 1, 2 * d_ff), F32)],
        scratch_shapes=[pltpu.VMEM((FFN_CONV - 1, 2 * d_ff), F32)],
        compiler_params=_cparams(("arbitrary", "arbitrary")),
        name="ffn",
    )(h2, x1, conv_prev, wu_bf, conv_w, conv_b, wd_bf, nf)


def _layer(x, W, *, hgrn_s0, conv_prev, mem_k, mem_v, attn, tm_proj, tm_merge, tm_ffn, tl_mem, head_layout):
    B, L, _ = x.shape
    N = B * L
    x2d = x.reshape(N, D_MODEL)
    z3, f2d, k_new, v_new = _in_proj(x2d, W["norm_attn"], W["w_in"], tm=tm_proj, rows_per_batch=L,
                                     head_layout=head_layout)
    z4 = z3.reshape(N_ZSLOT, B, L, D_MODEL)
    ya, s_new = _hgrn(z4, f2d.reshape(B, L, D_MODEL), W["lb_logits"], W["hgrn_onorm"], hgrn_s0, L=L,
                      tl=min(L, 512))
    yb, k_new, v_new = attn(z4, k_new, v_new)
    yc = _mem_attn(z4, mem_k, mem_v, tl=tl_mem)
    x1, h2 = _merge(x2d, ya.reshape(N, D_MODEL), yb.reshape(N, D_MODEL), yc.reshape(N, D_MODEL), z3,
                    W["w_branch"], W["w_out"], W["norm_ffn"], tm=tm_merge)
    y, conv_new = _ffn(h2.reshape(B, L, D_MODEL), x1.reshape(B, L, D_MODEL), conv_prev, W["w_up"],
                       W["conv_w"], W["conv_b"], W["w_down"], W["norm_final"], tm=tm_ffn)
    return y, k_new, v_new, s_new, conv_new


def kernel(x_prompt, x_sample, mem_prompt, cache_diff_k, cache_diff_v, cache_mem_k, cache_mem_v, state_hgrn, state_ffn_conv, rel_bias, hgrn_lb_logits, norm_attn, w_in, hgrn_onorm, diff_lambda, diff_subln, mem_norm, w_mem_kv, w_branch, w_out, norm_ffn, w_up, conv_w, conv_b, w_down, norm_final):
    assert w_in.shape[0] == 1, "single-layer trunk"
    Bp, Lp, _ = x_prompt.shape
    Bs, Ls, _ = x_sample.shape
    P = cache_diff_k.shape[3]
    d_ff2 = w_up.shape[2]
    row = lambda a: a.reshape(1, -1)
    W = dict(norm_attn=row(norm_attn[0]), w_in=w_in[0].astype(BF16), lb_logits=hgrn_lb_logits,
             hgrn_onorm=row(hgrn_onorm[0]), w_branch=w_branch[0].astype(BF16), w_out=w_out[0].astype(BF16),
             norm_ffn=row(norm_ffn[0]), w_up=w_up[0].astype(BF16), conv_w=conv_w[0], conv_b=row(conv_b[0]),
             w_down=w_down[0].astype(BF16), norm_final=row(norm_final))
    lp = diff_lambda[0]
    subln = row(diff_subln[0])

    T = 512
    bias_p, far_bucket = _bias_prompt(rel_bias, T)
    mk, mv = _mem_kv(mem_prompt, row(mem_norm[0]), w_mem_kv[0].astype(BF16))

    def attn_prompt(z4, k4, v4):
        yb = _diff_attn_prompt(z4, k4, v4, rel_bias, bias_p, lp, subln, T=T, G=2, far_bucket=far_bucket)
        return yb, k4, v4

    yp, pk, pv, ps, pc = _layer(
        x_prompt, W, hgrn_s0=jnp.zeros((Bp, N_HEADS, HEAD_W, HEAD_W), F32),
        conv_prev=jnp.zeros((Bp, FFN_CONV - 1, d_ff2), F32), mem_k=mk, mem_v=mv, attn=attn_prompt,
        tm_proj=512, tm_merge=512, tm_ffn=512, tl_mem=512, head_layout=True)

    bias_sp, bias_sn = _bias_sample(rel_bias, Ls, P)

    def attn_sample(z4, k2d, v2d):
        yb = _diff_attn_sample(z4, k2d.reshape(Bs, Ls, D_MODEL), v2d.reshape(Bs, Ls, D_MODEL),
                               cache_diff_k[0], cache_diff_v[0], bias_sp, bias_sn, lp, subln, hb=4)
        heads = lambda a: jnp.transpose(a.reshape(Bs, Ls, N_HEADS, HEAD_W), (0, 2, 1, 3))
        return yb, heads(k2d), heads(v2d)

    ys, sk, sv, ss, sc = _layer(
        x_sample, W, hgrn_s0=state_hgrn[0], conv_prev=state_ffn_conv[0], mem_k=cache_mem_k[0],
        mem_v=cache_mem_v[0], attn=attn_sample,
        tm_proj=Bs * Ls, tm_merge=Bs * Ls, tm_ffn=Ls, tl_mem=Ls, head_layout=False)

    return (yp, ys, pk[None], pv[None], ps[None], pc[None], mk[None], mv[None],
            sk[None], sv[None], ss[None], sc[None])
```

```python
import functools
import math

import numpy as np
import jax
import jax.numpy as jnp
from jax import lax
from jax.experimental import pallas as pl
from jax.experimental.pallas import tpu as pltpu

F32 = jnp.float32
BF16 = jnp.bfloat16

D_MODEL = 1024
CHUNK = 64
SUB = 16
HEAD_W = 128
N_HEADS = D_MODEL // HEAD_W
PAIR_W = 2 * HEAD_W
DIFF_DH = 64
DIFF_SCALE = DIFF_DH ** -0.5
MEM_HEADS = 4
MEM_DH = D_MODEL // MEM_HEADS
REL_BUCKETS = 32
REL_MAX_DIST = 128
N_BRANCH = 3
FFN_CONV = 3
RMS_EPS = 1e-6
LAMBDA_INIT = 0.8 - 0.6 * math.exp(-0.3 * 0)
NEG = -1e30
LOG2E = math.log2(math.e)
ONES_ROWS = 16

SEC_HQ, SEC_HF, SEC_HI, SEC_HG, SEC_DQ, SEC_DK, SEC_DV, SEC_MQ, SEC_G0 = range(9)
N_SEC = 11
Z_HQ, Z_HI, Z_HG, Z_DQ, Z_MQ, Z_G0 = 0, 1, 2, 3, 4, 5
N_ZSLOT = 8

V7X_VMEM_LIMIT = 56 * 1024 * 1024


def _cparams(sem, vmem=V7X_VMEM_LIMIT):
    return pltpu.CompilerParams(dimension_semantics=sem, vmem_limit_bytes=vmem)


def _sigmoid(x):
    return 0.5 * jnp.tanh(0.5 * x) + 0.5


def _rms(x, g):
    ms = jnp.mean(x * x, axis=-1, keepdims=True)
    return x * lax.rsqrt(ms + RMS_EPS) * g


def _np_bucket(rel):
    nb = REL_BUCKETS // 2
    ret = np.where(rel > 0, nb, 0)
    n = np.abs(rel)
    max_exact = nb // 2
    large = max_exact + (np.log(np.maximum(n, 1).astype(np.float32) / max_exact)
                         / math.log(REL_MAX_DIST / max_exact) * (nb - max_exact)).astype(np.int32)
    large = np.minimum(large, nb - 1)
    return ret + np.where(n < max_exact, n, large)


def _bucket_segments(lo, hi):
    rel = np.arange(lo, hi + 1, dtype=np.int32)
    b = _np_bucket(rel)
    change = np.nonzero(np.diff(b))[0]
    return int(b[0]), [(int(rel[i + 1]), int(b[i + 1])) for i in change]


def _bias_from_rel(rel, rb_ref, h, first_bucket, segs):
    val = jnp.full(rel.shape, rb_ref[first_bucket, h], F32)
    for lo, bk in segs:
        val = jnp.where(rel >= lo, rb_ref[bk, h], val)
    return val


def _bias_prompt_kernel(rb_ref, o_ref, *, T, first_bucket, segs):
    h = pl.program_id(0)
    kk = lax.broadcasted_iota(jnp.int32, (T, T), 0)
    qq = lax.broadcasted_iota(jnp.int32, (T, T), 1)
    o_ref[0, 0] = _bias_from_rel(kk - qq - T, rb_ref, h, first_bucket, segs) * LOG2E
    diag = _bias_from_rel(kk - qq, rb_ref, h, first_bucket, segs) * LOG2E
    o_ref[0, 1] = jnp.where(jnp.bitwise_and(kk, -CHUNK) <= qq, diag, NEG)
    o_ref[0, 2] = jnp.full((T, T), NEG, F32)


def _bias_prompt(rel_bias, T):
    H = rel_bias.shape[1]
    first_bucket, segs = _bucket_segments(-2 * T, T)
    assert all(lo > -T for lo, _ in segs), "bias must be constant beyond one tile"
    tiles = pl.pallas_call(
        functools.partial(_bias_prompt_kernel, T=T, first_bucket=first_bucket, segs=segs),
        grid=(H,),
        in_specs=[pl.BlockSpec(memory_space=pltpu.SMEM)],
        out_specs=pl.BlockSpec((1, 3, T, T), lambda h: (h, 0, 0, 0)),
        out_shape=jax.ShapeDtypeStruct((H, 3, T, T), F32),
        compiler_params=_cparams(("arbitrary",)),
        name="bias_prompt",
    )(rel_bias)
    return tiles, first_bucket


def _bias_sample_kernel(rb_ref, bp_ref, bn_ref, *, Ls, P, first_bucket, segs):
    h = pl.program_id(0)
    qq = jnp.bitwise_and(lax.broadcasted_iota(jnp.int32, (2 * Ls, P), 0), Ls - 1)
    kk = lax.broadcasted_iota(jnp.int32, (2 * Ls, P), 1)
    bp_ref[0] = _bias_from_rel(kk - P - qq, rb_ref, h, first_bucket, segs)
    qn = jnp.bitwise_and(lax.broadcasted_iota(jnp.int32, (2 * Ls, Ls), 0), Ls - 1)
    kn = lax.broadcasted_iota(jnp.int32, (2 * Ls, Ls), 1)
    bn_ref[0] = _bias_from_rel(kn - qn, rb_ref, h, first_bucket, segs)


def _bias_sample(rel_bias, Ls, P):
    H = rel_bias.shape[1]
    first_bucket, segs = _bucket_segments(-(P + Ls), Ls)
    return pl.pallas_call(
        functools.partial(_bias_sample_kernel, Ls=Ls, P=P, first_bucket=first_bucket, segs=segs),
        grid=(H,),
        in_specs=[pl.BlockSpec(memory_space=pltpu.SMEM)],
        out_specs=[pl.BlockSpec((1, 2 * Ls, P), lambda h: (h, 0, 0)),
                   pl.BlockSpec((1, 2 * Ls, Ls), lambda h: (h, 0, 0))],
        out_shape=[jax.ShapeDtypeStruct((H, 2 * Ls, P), F32),
                   jax.ShapeDtypeStruct((H, 2 * Ls, Ls), F32)],
        compiler_params=_cparams(("arbitrary",)),
        name="bias_sample",
    )(rel_bias)


def _in_proj_kernel(x_ref, g_ref, w_ref, z_ref, f_ref, k_ref, v_ref, h_scr, *, head_layout):
    sec = jnp.right_shift(pl.program_id(1), 1)

    @pl.when(pl.program_id(1) == 0)
    def _():
        h_scr[...] = _rms(x_ref[...], g_ref[...]).astype(BF16)

    def half_section():
        return jnp.dot(h_scr[...], w_ref[...], preferred_element_type=F32)

    def store_heads(ref, acc):
        if head_layout:
            for hh in range(N_HEADS // 2):
                ref[0, hh] = acc[:, hh * HEAD_W:(hh + 1) * HEAD_W]
        else:
            ref[...] = acc

    @pl.when(sec == SEC_HF)
    def _():
        f_ref[...] = half_section()

    @pl.when(sec == SEC_DK)
    def _():
        store_heads(k_ref, half_section())

    @pl.when(sec == SEC_DV)
    def _():
        store_heads(v_ref, half_section())

    @pl.when((sec != SEC_HF) & (sec != SEC_DK) & (sec != SEC_DV))
    def _():
        z_ref[0] = half_section().astype(BF16)


def _zslot(sec):
    return (sec - (sec >= SEC_HF).astype(jnp.int32) - (sec >= SEC_DK).astype(jnp.int32)
            - (sec >= SEC_DV).astype(jnp.int32))


def _zhalf(j):
    sec = jnp.right_shift(j, 1)
    is_f32 = (sec == SEC_HF) | (sec == SEC_DK) | (sec == SEC_DV)
    return jnp.where(is_f32, 1, jnp.bitwise_and(j, 1))


def _resident_half(sec_id):
    return lambda j: (j >= 2 * sec_id + 1).astype(jnp.int32)


def _in_proj(x2d, g, w_bf, *, tm, rows_per_batch, head_layout):
    N = x2d.shape[0]
    nt = N // tm
    half = D_MODEL // 2
    hf, hk, hv = _resident_half(SEC_HF), _resident_half(SEC_DK), _resident_half(SEC_DV)
    if head_layout:
        B = N // rows_per_batch
        nlt = rows_per_batch // tm
        kv_shape = jax.ShapeDtypeStruct((B, N_HEADS, rows_per_batch, HEAD_W), F32)
        kv_spec = lambda hh: pl.BlockSpec((1, N_HEADS // 2, tm, HEAD_W),
                                          lambda i, j: (i // nlt, hh(j), i % nlt, 0))
    else:
        kv_shape = jax.ShapeDtypeStruct((N, D_MODEL), F32)
        kv_spec = lambda hh: pl.BlockSpec((tm, half), lambda i, j: (i, hh(j)))
    return pl.pallas_call(
        functools.partial(_in_proj_kernel, head_layout=head_layout),
        grid=(nt, 2 * N_SEC),
        in_specs=[pl.BlockSpec((tm, D_MODEL), lambda i, j: (i, 0)),
                  pl.BlockSpec((1, D_MODEL), lambda i, j: (0, 0)),
                  pl.BlockSpec((D_MODEL, half), lambda i, j: (0, j))],
        out_specs=[pl.BlockSpec((1, tm, half), lambda i, j: (_zslot(jnp.right_shift(j, 1)), i, _zhalf(j))),
                   pl.BlockSpec((tm, half), lambda i, j: (i, hf(j))),
                   kv_spec(hk), kv_spec(hv)],
        out_shape=[jax.ShapeDtypeStruct((N_ZSLOT, N, D_MODEL), BF16),
                   jax.ShapeDtypeStruct((N, D_MODEL), F32),
                   kv_shape, kv_shape],
        scratch_shapes=[pltpu.VMEM((tm, D_MODEL), BF16)],
        compiler_params=_cparams(("arbitrary", "arbitrary")),
        name="in_proj",
    )(x2d, g, w_bf)


def _block_diag(a, b):
    za = jnp.zeros((a.shape[0], b.shape[1]), a.dtype)
    zb = jnp.zeros((b.shape[0], a.shape[1]), b.dtype)
    return jnp.concatenate([jnp.concatenate([a, za], axis=1), jnp.concatenate([zb, b], axis=1)], axis=0)


def _hgrn_kernel(q_ref, f_ref, i_ref, g_ref, lbl_ref, on_ref, s0_ref, ya_ref, sn_ref, st_scr, *, TL, C):
    lt = pl.program_id(1)
    nsub = C // SUB
    lbl = lbl_ref[...]
    e = jnp.exp(lbl - jnp.max(lbl, axis=0, keepdims=True))
    lb_all = e[0:1] / jnp.sum(e, axis=0, keepdims=True)

    @pl.when(lt == 0)
    def _():
        for p in range(N_HEADS // 2):
            st_scr[p] = jnp.concatenate([s0_ref[0, 2 * p].T, s0_ref[0, 2 * p + 1].T], axis=1)

    t2 = lax.broadcasted_iota(jnp.int32, (2 * C, C), 0)
    s2 = lax.broadcasted_iota(jnp.int32, (2 * C, C), 1)
    limit = jnp.where(t2 < C, t2 + 1, jnp.bitwise_and(t2 - C, -SUB))
    tril2 = jnp.where(s2 < limit, 1.0, 0.0).astype(BF16)
    width = max(nsub * C, HEAD_W)
    wrow = lax.broadcasted_iota(jnp.int32, (C, width), 0)
    wcol = lax.broadcasted_iota(jnp.int32, (C, width), 1)
    base = jnp.bitwise_and(wrow, -SUB) * (C // SUB)
    keep = (wcol >= base) & (wcol <= base + wrow)
    keep2 = jnp.concatenate([keep, keep], axis=1)
    plane = lax.broadcasted_iota(jnp.int32, (HEAD_W, PAIR_W), 1)
    onorm = on_ref[...]
    nt = (((1,), (1,)), ((), ()))

    def chunk(c, carry):
        r0 = pl.multiple_of(c * C, C)
        f = f_ref[0, pl.ds(r0, C), :]
        fg = lb_all + (1.0 - lb_all) * _sigmoid(f)
        logf = jnp.log(fg)
        kk = 1.0 - fg
        hi = logf.astype(BF16)
        lo = (logf - hi.astype(F32)).astype(BF16)
        cum = jnp.dot(tril2, jnp.concatenate([hi, lo], axis=1), preferred_element_type=F32)
        bcum = cum[:, :D_MODEL] + cum[:, D_MODEL:]
        b = bcum[:C]
        r = bcum[C:]
        b_last = b[C - 1:C, :]
        q = q_ref[0, 0, pl.ds(r0, C), :].astype(F32)
        q_inter = (q * jnp.exp(b)).astype(BF16)
        q_intra = (q * jnp.exp(b - r)).astype(BF16)
        k_state = (kk * jnp.exp(b_last - b)).astype(BF16)
        k_sub = []
        for i in range(nsub):
            n = SUB * (i + 1)
            ki = (kk[:n] * jnp.exp(r[SUB * i:SUB * i + 1, :] - b[:n])).astype(BF16)
            if n < C:
                ki = jnp.concatenate([ki, jnp.zeros((C - n, D_MODEL), BF16)], axis=0)
            k_sub.append(ki)
        st_decay = jnp.exp(b_last)
        g = g_ref[0, 0, pl.ds(r0, C), :].astype(F32)
        gate = g * _sigmoid(g)
        for p in range(N_HEADS // 2):
            pc = slice(p * PAIR_W, (p + 1) * PAIR_W)
            hc = [slice((2 * p + a) * HEAD_W, (2 * p + a + 1) * HEAD_W) for a in range(2)]
            k4 = []
            vT = []
            kpad = []
            for a in range(2):
                pieces = [k[:, hc[a]] for k in k_sub]
                if nsub * C < width:
                    pieces.append(jnp.zeros((width - nsub * C, HEAD_W), BF16))
                k4.append(jnp.concatenate(pieces, axis=0))
                v = i_ref[0, 0, pl.ds(r0, C), hc[a]].astype(F32)
                vrows = [v] * (HEAD_W // C) if nsub > 1 else [v, jnp.zeros((HEAD_W - C, HEAD_W), F32)]
                vT.append(jnp.concatenate(vrows, axis=0).T.astype(BF16))
                kpad.append(jnp.concatenate([k_state[:, hc[a]], jnp.zeros((HEAD_W - C, HEAD_W), BF16)], axis=0))
            sc = lax.dot_general(q_intra[:, pc], _block_diag(k4[0], k4[1]), nt,
                                 preferred_element_type=F32)
            w = jnp.where(keep2, sc, 0.0).astype(BF16)
            stp = st_scr[p]
            stb = stp.astype(BF16)
            vfull = [jnp.concatenate([t] * (width // HEAD_W), axis=1) for t in vT]
            zw = jnp.zeros((HEAD_W, width), BF16)
            rhs_o = jnp.concatenate(
                [jnp.concatenate([vfull[0], zw, jnp.where(plane < HEAD_W, stb, 0.0)], axis=1),
                 jnp.concatenate([zw, vfull[1], jnp.where(plane >= HEAD_W, stb, 0.0)], axis=1)], axis=0)
            o2 = lax.dot_general(jnp.concatenate([w, q_inter[:, pc]], axis=1), rhs_o, nt,
                                 preferred_element_type=F32)
            upd = jnp.dot(jnp.concatenate(vT, axis=1), _block_diag(kpad[0], kpad[1]),
                          preferred_element_type=F32)
            st_scr[p] = stp * st_decay[:, pc] + upd
            for a in range(2):
                o = o2[:, a * HEAD_W:(a + 1) * HEAD_W]
                ya_ref[0, pl.ds(r0, C), hc[a]] = (_rms(o, onorm) * gate[:, hc[a]]).astype(BF16)
        return carry

    lax.fori_loop(0, TL // C, chunk, 0)

    @pl.when(lt == pl.num_programs(1) - 1)
    def _():
        for p in range(N_HEADS // 2):
            stp = st_scr[p]
            for a in range(2):
                sn_ref[0, 2 * p + a] = stp[:, a * HEAD_W:(a + 1) * HEAD_W].T


def _hgrn(z4, f3, lb_logits, onorm, s0, *, L, tl):
    B = z4.shape[1]
    C = min(CHUNK, L)
    slot = lambda s: pl.BlockSpec((1, 1, tl, D_MODEL), lambda b, t: (s, b, t, 0))
    row = pl.BlockSpec((1, tl, D_MODEL), lambda b, t: (b, t, 0))
    state = pl.BlockSpec((1, N_HEADS, HEAD_W, HEAD_W), lambda b, t: (b, 0, 0, 0))
    return pl.pallas_call(
        functools.partial(_hgrn_kernel, TL=tl, C=C),
        grid=(B, L // tl),
        in_specs=[slot(Z_HQ), row, slot(Z_HI), slot(Z_HG),
                  pl.BlockSpec(lb_logits.shape, lambda b, t: (0, 0)),
                  pl.BlockSpec((1, HEAD_W), lambda b, t: (0, 0)),
                  state],
        out_specs=[row, state],
        out_shape=[jax.ShapeDtypeStruct((B, L, D_MODEL), BF16),
                   jax.ShapeDtypeStruct((B, N_HEADS, HEAD_W, HEAD_W), F32)],
        scratch_shapes=[pltpu.VMEM((N_HEADS // 2, HEAD_W, PAIR_W), F32)],
        compiler_params=_cparams(("arbitrary", "arbitrary")),
        name="hgrn",
    )(z4, f3, z4, z4, lb_logits, onorm, s0)


def _diff_lambda(lp_ref):
    lp = lp_ref[...]
    a = jnp.sum(lp[0:1] * lp[1:2], axis=-1, keepdims=True)
    b = jnp.sum(lp[2:3] * lp[3:4], axis=-1, keepdims=True)
    return jnp.exp(a) - jnp.exp(b) + LAMBDA_INIT


def _split_maps(q):
    lane = lax.broadcasted_iota(jnp.int32, q.shape, 1)
    return jnp.where(lane < DIFF_DH, q, 0.0), jnp.where(lane >= DIFF_DH, q, 0.0)


def _diff_attn_prompt_kernel(rb_ref, q_ref, k_ref, v_ref, bias_ref, lp_ref, sub_ref, o_ref, kb, vT, acc, m_scr,
                             *, T, nk, G, far_bucket):
    hg = pl.program_id(1)
    qi = pl.program_id(2)
    nt = (((1,), (1,)), ((), ()))

    @pl.when(qi == 0)
    def _():
        ones = jnp.ones((ONES_ROWS, T), BF16)
        for g in range(G):
            for j in range(nk):
                kb[g, j] = (k_ref[0, g, j * T:(j + 1) * T, :] * (DIFF_SCALE * LOG2E)).astype(BF16)
                vT[g, j, :HEAD_W, :] = v_ref[0, g, j * T:(j + 1) * T, :].T.astype(BF16)
                vT[g, j, HEAD_W:, :] = ones

    qs = []
    for g in range(G):
        qa, qb = _split_maps(q_ref[0, 0, :, g * HEAD_W:(g + 1) * HEAD_W])
        qs.append((qa, qb))
    acc[...] = jnp.zeros(acc.shape, F32)
    m_scr[...] = jnp.full(m_scr.shape, NEG, F32)

    def tile(j, bias_slot):
        for g in range(G):
            kt = kb[g, j]
            vt = vT[g, j]
            if bias_slot is None:
                shift = rb_ref[far_bucket, hg * G + g] * LOG2E
            else:
                bias = bias_ref[g, bias_slot]
            for mp in range(2):
                idx = 2 * g + mp
                m = m_scr[idx]
                s = lax.dot_general(kt, qs[g][mp], nt, preferred_element_type=F32)
                if bias_slot is None:
                    mn = jnp.maximum(m, jnp.max(s, axis=0, keepdims=True) + shift)
                    p = jnp.exp2(s - (mn - shift))
                else:
                    s = s + bias
                    mn = jnp.maximum(m, jnp.max(s, axis=0, keepdims=True))
                    p = jnp.exp2(s - mn)
                m_scr[idx] = mn
                acc[idx] = jnp.exp2(m - mn) * acc[idx] + jnp.dot(vt, p.astype(BF16),
                                                                  preferred_element_type=F32)

    def far(j, c):
        tile(j, None)
        return c

    lax.fori_loop(0, qi - 1, far, 0)

    @pl.when(qi >= 1)
    def _():
        tile(qi - 1, 0)

    tile(qi, 1)
    lam = _diff_lambda(lp_ref)
    for g in range(G):
        a1 = acc[2 * g]
        a2 = acc[2 * g + 1]
        o = (a1[:HEAD_W] / a1[HEAD_W:HEAD_W + 1] - lam * (a2[:HEAD_W] / a2[HEAD_W:HEAD_W + 1])).T
        o_ref[0, :, g * HEAD_W:(g + 1) * HEAD_W] = (_rms(o, sub_ref[...]) * (1.0 - LAMBDA_INIT)).astype(BF16)


def _diff_attn_prompt(z4, k4, v4, rel_bias, bias, lp, subln, *, T, G, far_bucket):
    B, H, L, _ = k4.shape
    nk = L // T
    W = G * HEAD_W
    kv_spec = pl.BlockSpec((1, G, L, HEAD_W), lambda b, h, qi: (b, h, 0, 0))
    return pl.pallas_call(
        functools.partial(_diff_attn_prompt_kernel, T=T, nk=nk, G=G, far_bucket=far_bucket),
        grid=(B, H // G, nk),
        in_specs=[pl.BlockSpec(memory_space=pltpu.SMEM),
                  pl.BlockSpec((1, 1, T, W), lambda b, h, qi: (Z_DQ, b, qi, h)),
                  kv_spec, kv_spec,
                  pl.BlockSpec((G, 3, T, T), lambda b, h, qi: (h, 0, 0, 0)),
                  pl.BlockSpec(lp.shape, lambda b, h, qi: (0, 0)),
                  pl.BlockSpec((1, HEAD_W), lambda b, h, qi: (0, 0))],
        out_specs=pl.BlockSpec((1, T, W), lambda b, h, qi: (b, qi, h)),
        out_shape=jax.ShapeDtypeStruct((B, L, D_MODEL), BF16),
        scratch_shapes=[pltpu.VMEM((G, nk, T, HEAD_W), BF16),
                        pltpu.VMEM((G, nk, HEAD_W + ONES_ROWS, T), BF16),
                        pltpu.VMEM((2 * G, HEAD_W + ONES_ROWS, T), F32),
                        pltpu.VMEM((2 * G, 1, T), F32)],
        compiler_params=_cparams(("arbitrary", "arbitrary", "arbitrary")),
        name="diff_attn_prompt",
    )(rel_bias, z4, k4, v4, bias, lp, subln)


def _diff_attn_sample_kernel(q_ref, kn_ref, vn_ref, kc_ref, vc_ref, bp_ref, bn_ref, lp_ref, sub_ref, o_ref,
                             *, Ls, hb):
    lam = _diff_lambda(lp_ref)
    nt = (((1,), (1,)), ((), ()))
    for hh in range(hb):
        cols = slice(hh * HEAD_W, (hh + 1) * HEAD_W)
        qa, qb = _split_maps(q_ref[0, 0, :, cols].astype(F32) * DIFF_SCALE)
        q2 = jnp.concatenate([qa, qb], axis=0).astype(BF16)
        kp = kc_ref[0, hh].astype(BF16)
        vp = vc_ref[0, hh].astype(BF16)
        kn = kn_ref[0, :, cols].astype(BF16)
        vn = vn_ref[0, :, cols].astype(BF16)
        s = lax.dot_general(q2, kp, nt, preferred_element_type=F32) + bp_ref[hh]
        sn = lax.dot_general(q2, kn, nt, preferred_element_type=F32) + bn_ref[hh]
        m = jnp.maximum(jnp.max(s, axis=-1, keepdims=True), jnp.max(sn, axis=-1, keepdims=True))
        p = jnp.exp(s - m)
        pn = jnp.exp(sn - m)
        l = jnp.sum(p, axis=-1, keepdims=True) + jnp.sum(pn, axis=-1, keepdims=True)
        o2 = (jnp.dot(p.astype(BF16), vp, preferred_element_type=F32)
              + jnp.dot(pn.astype(BF16), vn, preferred_element_type=F32)) / l
        o = o2[:Ls] - lam * o2[Ls:]
        o_ref[0, :, cols] = (_rms(o, sub_ref[...]) * (1.0 - LAMBDA_INIT)).astype(BF16)


def _diff_attn_sample(z4, kn3, vn3, kc, vc, bias_p, bias_n, lp, subln, *, hb):
    B, H, P, _ = kc.shape
    Ls = kn3.shape[1]
    W = hb * HEAD_W
    cache_spec = pl.BlockSpec((1, hb, P, HEAD_W), lambda b, g: (b, g, 0, 0))
    new_spec = pl.BlockSpec((1, Ls, W), lambda b, g: (b, 0, g))
    return pl.pallas_call(
        functools.partial(_diff_attn_sample_kernel, Ls=Ls, hb=hb),
        grid=(B, H // hb),
        in_specs=[pl.BlockSpec((1, 1, Ls, W), lambda b, g: (Z_DQ, b, 0, g)),
                  new_spec, new_spec, cache_spec, cache_spec,
                  pl.BlockSpec((hb, 2 * Ls, P), lambda b, g: (g, 0, 0)),
                  pl.BlockSpec((hb, 2 * Ls, Ls), lambda b, g: (g, 0, 0)),
                  pl.BlockSpec(lp.shape, lambda b, g: (0, 0)),
                  pl.BlockSpec((1, HEAD_W), lambda b, g: (0, 0))],
        out_specs=pl.BlockSpec((1, Ls, W), lambda b, g: (b, 0, g)),
        out_shape=jax.ShapeDtypeStruct((B, Ls, D_MODEL), BF16),
        compiler_params=_cparams(("arbitrary", "arbitrary")),
        name="diff_attn_sample",
    )(z4, kn3, vn3, kc, vc, bias_p, bias_n, lp, subln)


def _mem_kv_kernel(m_ref, g_ref, w_ref, k_ref, v_ref):
    h = _rms(m_ref[0], g_ref[...]).astype(BF16)
    kv = jnp.dot(h, w_ref[...], preferred_element_type=F32)
    for hh in range(MEM_HEADS):
        k_ref[0, hh] = kv[:, hh * MEM_DH:(hh + 1) * MEM_DH]
        v_ref[0, hh] = kv[:, D_MODEL + hh * MEM_DH:D_MODEL + (hh + 1) * MEM_DH]


def _mem_kv(mem, g, w_bf):
    B, M, _ = mem.shape
    out = jax.ShapeDtypeStruct((B, MEM_HEADS, M, MEM_DH), F32)
    spec = pl.BlockSpec((1, MEM_HEADS, M, MEM_DH), lambda b: (b, 0, 0, 0))
    return pl.pallas_call(
        _mem_kv_kernel,
        grid=(B,),
        in_specs=[pl.BlockSpec((1, M, D_MODEL), lambda b: (b, 0, 0)),
                  pl.BlockSpec((1, D_MODEL), lambda b: (0, 0)),
                  pl.BlockSpec(w_bf.shape, lambda b: (0, 0))],
        out_specs=[spec, spec],
        out_shape=[out, out],
        compiler_params=_cparams(("arbitrary",)),
        name="mem_kv",
    )(mem, g, w_bf)


def _mem_attn_kernel(q_ref, k_ref, v_ref, o_ref):
    nt = (((1,), (1,)), ((), ()))
    for hh in range(MEM_HEADS):
        cols = slice(hh * MEM_DH, (hh + 1) * MEM_DH)
        q = q_ref[0, 0, :, cols]
        s = lax.dot_general(q, k_ref[0, hh].astype(BF16), nt, preferred_element_type=F32) * (MEM_DH ** -0.5)
        p = jnp.exp(s - jnp.max(s, axis=-1, keepdims=True))
        p = p / jnp.sum(p, axis=-1, keepdims=True)
        o = jnp.dot(p.astype(BF16), v_ref[0, hh].astype(BF16), preferred_element_type=F32)
        o_ref[0, :, cols] = o.astype(BF16)


def _mem_attn(z4, mk, mv, *, tl):
    _, B, L, _ = z4.shape
    M = mk.shape[2]
    kv_spec = pl.BlockSpec((1, MEM_HEADS, M, MEM_DH), lambda b, t: (b, 0, 0, 0))
    return pl.pallas_call(
        _mem_attn_kernel,
        grid=(B, L // tl),
        in_specs=[pl.BlockSpec((1, 1, tl, D_MODEL), lambda b, t: (Z_MQ, b, t, 0)), kv_spec, kv_spec],
        out_specs=pl.BlockSpec((1, tl, D_MODEL), lambda b, t: (b, t, 0)),
        out_shape=jax.ShapeDtypeStruct((B, L, D_MODEL), BF16),
        compiler_params=_cparams(("arbitrary", "arbitrary")),
        name="mem_attn",
    )(z4, mk, mv)


def _merge_kernel(x_ref, ya_ref, yb_ref, yc_ref, g0_ref, g1_ref, g2_ref, wb_ref, wo_ref, nf_ref,
                  x1_ref, h2_ref):
    merged = None
    for n, (y_ref, gate_ref) in enumerate(((ya_ref, g0_ref), (yb_ref, g1_ref), (yc_ref, g2_ref))):
        proj = jnp.dot(y_ref[...], wb_ref[n], preferred_element_type=F32)
        term = proj * _sigmoid(gate_ref[0].astype(F32))
        merged = term if merged is None else merged + term
    x1 = x_ref[...] + jnp.dot(merged.astype(BF16), wo_ref[...], preferred_element_type=F32)
    x1_ref[...] = x1
    h2_ref[...] = _rms(x1, nf_ref[...]).astype(BF16)


def _merge(x2d, ya, yb, yc, z3, wb_bf, wo_bf, nf, *, tm):
    N = x2d.shape[0]
    row = pl.BlockSpec((tm, D_MODEL), lambda i: (i, 0))
    gate = lambda n: pl.BlockSpec((1, tm, D_MODEL), lambda i: (Z_G0 + n, i, 0))
    return pl.pallas_call(
        _merge_kernel,
        grid=(N // tm,),
        in_specs=[row, row, row, row, gate(0), gate(1), gate(2),
                  pl.BlockSpec(wb_bf.shape, lambda i: (0, 0, 0)),
                  pl.BlockSpec(wo_bf.shape, lambda i: (0, 0)),
                  pl.BlockSpec((1, D_MODEL), lambda i: (0, 0))],
        out_specs=[row, row],
        out_shape=[jax.ShapeDtypeStruct((N, D_MODEL), F32), jax.ShapeDtypeStruct((N, D_MODEL), BF16)],
        compiler_params=_cparams(("arbitrary",)),
        name="merge",
    )(x2d, ya, yb, yc, z3, z3, z3, wb_bf, wo_bf, nf)


def _ffn_kernel(h_ref, x_ref, cp_ref, wu_ref, cw_ref, cb_ref, wd_ref, nf_ref, y_ref, cn_ref, carry, u_scr, act_scr,
                *, tm, cw, d_ff):
    lt = pl.program_id(1)

    @pl.when(lt == 0)
    def _():
        carry[...] = cp_ref[0]

    h = h_ref[0]
    edge = 8
    erow = lax.broadcasted_iota(jnp.int32, (edge, cw), 0)
    for jc in range(d_ff // cw):
        slot = jc % 2
        bases = (jc * cw, d_ff + jc * cw)
        for hf, base in enumerate(bases):
            u_scr[slot, hf] = jnp.dot(h, wu_ref[:, base:base + cw], preferred_element_type=F32)
        halves = []
        for hf, base in enumerate(bases):
            cols = slice(base, base + cw)
            u = u_scr[slot, hf]
            w0, w1, w2, bb = cw_ref[0:1, cols], cw_ref[1:2, cols], cw_ref[2:3, cols], cb_ref[:, cols]
            u1 = pltpu.roll(u, 1, 0)
            u2 = pltpu.roll(u, 2, 0)
            c = bb + w0 * u2 + w1 * u1 + w2 * u
            p2 = carry[0:1, cols]
            p1 = carry[1:2, cols]
            u1e = jnp.where(erow == 0, p1, u1[:edge])
            u2e = jnp.where(erow == 0, p2, jnp.where(erow == 1, p1, u2[:edge]))
            ce = bb + w0 * u2e + w1 * u1e + w2 * u[:edge]
            halves.append(jnp.concatenate([ce, c[edge:]], axis=0))
            tail = u[tm - (FFN_CONV - 1):, :]
            carry[:, cols] = tail
            cn_ref[0, :, cols] = tail
        gate, val = halves
        act_scr[:, jc * cw:(jc + 1) * cw] = (gate * _sigmoid(gate) * val).astype(BF16)
    y = x_ref[0] + jnp.dot(act_scr[...], wd_ref[...], preferred_element_type=F32)
    y_ref[0] = _rms(y, nf_ref[...])


def _ffn(h2, x1, conv_prev, wu_bf, conv_w, conv_b, wd_bf, nf, *, tm, cw=256):
    B, L, _ = x1.shape
    d_ff = wd_bf.shape[0]
    row = pl.BlockSpec((1, tm, D_MODEL), lambda b, t: (b, t, 0))
    state = pl.BlockSpec((1, FFN_CONV - 1, 2 * d_ff), lambda b, t: (b, 0, 0))
    const = lambda a: pl.BlockSpec(a.shape, lambda b, t: (0,) * a.ndim)
    return pl.pallas_call(
        functools.partial(_ffn_kernel, tm=tm, cw=cw, d_ff=d_ff),
        grid=(B, L // tm),
        in_specs=[row, row, state, const(wu_bf), const(conv_w), const(conv_b), const(wd_bf), const(nf)],
        out_specs=[row, state],
        out_shape=[jax.ShapeDtypeStruct((B, L, D_MODEL), F32),
                   jax.ShapeDtypeStruct((B, FFN_CONV - 1, 2 * d_ff), F32)],
        scratch_shapes=[pltpu.VMEM((FFN_CONV - 1, 2 * d_ff), F32),
                        pltpu.VMEM((2, 2, tm, cw), F32),
                        pltpu.VMEM((tm, d_ff), BF16)],
        compiler_params=_cparams(("arbitrary", "arbitrary")),
        name="ffn",
    )(h2, x1, conv_prev, wu_bf, conv_w, conv_b, wd_bf, nf)


def _layer(x, W, *, hgrn_s0, conv_prev, mem_k, mem_v, attn, tm_proj, tm_merge, tm_ffn, tl_mem, head_layout):
    B, L, _ = x.shape
    N = B * L
    x2d = x.reshape(N, D_MODEL)
    z3, f2d, k_new, v_new = _in_proj(x2d, W["norm_attn"], W["w_in"], tm=tm_proj, rows_per_batch=L,
                                     head_layout=head_layout)
    z4 = z3.reshape(N_ZSLOT, B, L, D_MODEL)
    ya, s_new = _hgrn(z4, f2d.reshape(B, L, D_MODEL), W["lb_logits"], W["hgrn_onorm"], hgrn_s0, L=L,
                      tl=min(L, 512))
    yb, k_new, v_new = attn(z4, k_new, v_new)
    yc = _mem_attn(z4, mem_k, mem_v, tl=tl_mem)
    x1, h2 = _merge(x2d, ya.reshape(N, D_MODEL), yb.reshape(N, D_MODEL), yc.reshape(N, D_MODEL), z3,
                    W["w_branch"], W["w_out"], W["norm_ffn"], tm=tm_merge)
    y, conv_new = _ffn(h2.reshape(B, L, D_MODEL), x1.reshape(B, L, D_MODEL), conv_prev, W["w_up"],
                       W["conv_w"], W["conv_b"], W["w_down"], W["norm_final"], tm=tm_ffn)
    return y, k_new, v_new, s_new, conv_new


def kernel(x_prompt, x_sample, mem_prompt, cache_diff_k, cache_diff_v, cache_mem_k, cache_mem_v, state_hgrn, state_ffn_conv, rel_bias, hgrn_lb_logits, norm_attn, w_in, hgrn_onorm, diff_lambda, diff_subln, mem_norm, w_mem_kv, w_branch, w_out, norm_ffn, w_up, conv_w, conv_b, w_down, norm_final):
    assert w_in.shape[0] == 1, "single-layer trunk"
    Bp, Lp, _ = x_prompt.shape
    Bs, Ls, _ = x_sample.shape
    P = cache_diff_k.shape[3]
    d_ff2 = w_up.shape[2]
    row = lambda a: a.reshape(1, -1)
    W = dict(norm_attn=row(norm_attn[0]), w_in=w_in[0].astype(BF16), lb_logits=hgrn_lb_logits,
             hgrn_onorm=row(hgrn_onorm[0]), w_branch=w_branch[0].astype(BF16), w_out=w_out[0].astype(BF16),
             norm_ffn=row(norm_ffn[0]), w_up=w_up[0].astype(BF16), conv_w=conv_w[0], conv_b=row(conv_b[0]),
             w_down=w_down[0].astype(BF16), norm_final=row(norm_final))
    lp = diff_lambda[0]
    subln = row(diff_subln[0])

    T = 512
    bias_p, far_bucket = _bias_prompt(rel_bias, T)
    mk, mv = _mem_kv(mem_prompt, row(mem_norm[0]), w_mem_kv[0].astype(BF16))

    def attn_prompt(z4, k4, v4):
        yb = _diff_attn_prompt(z4, k4, v4, rel_bias, bias_p, lp, subln, T=T, G=2, far_bucket=far_bucket)
        return yb, k4, v4

    yp, pk, pv, ps, pc = _layer(
        x_prompt, W, hgrn_s0=jnp.zeros((Bp, N_HEADS, HEAD_W, HEAD_W), F32),
        conv_prev=jnp.zeros((Bp, FFN_CONV - 1, d_ff2), F32), mem_k=mk, mem_v=mv, attn=attn_prompt,
        tm_proj=1024, tm_merge=512, tm_ffn=512, tl_mem=512, head_layout=True)

    bias_sp, bias_sn = _bias_sample(rel_bias, Ls, P)

    def attn_sample(z4, k2d, v2d):
        yb = _diff_attn_sample(z4, k2d.reshape(Bs, Ls, D_MODEL), v2d.reshape(Bs, Ls, D_MODEL),
                               cache_diff_k[0], cache_diff_v[0], bias_sp, bias_sn, lp, subln, hb=4)
        heads = lambda a: jnp.transpose(a.reshape(Bs, Ls, N_HEADS, HEAD_W), (0, 2, 1, 3))
        return yb, heads(k2d), heads(v2d)

    ys, sk, sv, ss, sc = _layer(
        x_sample, W, hgrn_s0=state_hgrn[0], conv_prev=state_ffn_conv[0], mem_k=cache_mem_k[0],
        mem_v=cache_mem_v[0], attn=attn_sample,
        tm_proj=Bs * Ls, tm_merge=Bs * Ls, tm_ffn=Ls, tl_mem=Ls, head_layout=False)

    return (yp, ys, pk[None], pv[None], ps[None], pc[None], mk[None], mv[None],
            sk[None], sv[None], ss[None], sc[None])
```

```python
import functools
import math

import numpy as np
import jax
import jax.numpy as jnp
from jax import lax
from jax.experimental import pallas as pl
from jax.experimental.pallas import tpu as pltpu

F32 = jnp.float32
BF16 = jnp.bfloat16

D_MODEL = 1024
CHUNK = 64
SUB = 16
HEAD_W = 128
N_HEADS = D_MODEL // HEAD_W
PAIR_W = 2 * HEAD_W
DIFF_DH = 64
DIFF_SCALE = DIFF_DH ** -0.5
MEM_HEADS = 4
MEM_DH = D_MODEL // MEM_HEADS
REL_BUCKETS = 32
REL_MAX_DIST = 128
N_BRANCH = 3
FFN_CONV = 3
RMS_EPS = 1e-6
LAMBDA_INIT = 0.8 - 0.6 * math.exp(-0.3 * 0)
NEG = -1e30
LOG2E = math.log2(math.e)
ONES_ROWS = 16

SEC_HQ, SEC_HF, SEC_HI, SEC_HG, SEC_DQ, SEC_DK, SEC_DV, SEC_MQ, SEC_G0 = range(9)
N_SEC = 11
Z_HQ, Z_HI, Z_HG, Z_DQ, Z_MQ, Z_G0 = 0, 1, 2, 3, 4, 5
N_ZSLOT = 8

V7X_VMEM_LIMIT = 56 * 1024 * 1024


def _cparams(sem, vmem=V7X_VMEM_LIMIT):
    return pltpu.CompilerParams(dimension_semantics=sem, vmem_limit_bytes=vmem)


def _sigmoid(x):
    return 0.5 * jnp.tanh(0.5 * x) + 0.5


def _rms(x, g):
    ms = jnp.mean(x * x, axis=-1, keepdims=True)
    return x * lax.rsqrt(ms + RMS_EPS) * g


def _np_bucket(rel):
    nb = REL_BUCKETS // 2
    ret = np.where(rel > 0, nb, 0)
    n = np.abs(rel)
    max_exact = nb // 2
    large = max_exact + (np.log(np.maximum(n, 1).astype(np.float32) / max_exact)
                         / math.log(REL_MAX_DIST / max_exact) * (nb - max_exact)).astype(np.int32)
    large = np.minimum(large, nb - 1)
    return ret + np.where(n < max_exact, n, large)


def _bucket_segments(lo, hi):
    rel = np.arange(lo, hi + 1, dtype=np.int32)
    b = _np_bucket(rel)
    change = np.nonzero(np.diff(b))[0]
    return int(b[0]), [(int(rel[i + 1]), int(b[i + 1])) for i in change]


def _bias_from_rel(rel, rb_ref, h, first_bucket, segs):
    val = jnp.full(rel.shape, rb_ref[first_bucket, h], F32)
    for lo, bk in segs:
        val = jnp.where(rel >= lo, rb_ref[bk, h], val)
    return val


def _bias_prompt_kernel(rb_ref, o_ref, *, T, first_bucket, segs):
    h = pl.program_id(0)
    kk = lax.broadcasted_iota(jnp.int32, (T, T), 0)
    qq = lax.broadcasted_iota(jnp.int32, (T, T), 1)
    o_ref[0, 0] = _bias_from_rel(kk - qq - T, rb_ref, h, first_bucket, segs) * LOG2E
    diag = _bias_from_rel(kk - qq, rb_ref, h, first_bucket, segs) * LOG2E
    o_ref[0, 1] = jnp.where(jnp.bitwise_and(kk, -CHUNK) <= qq, diag, NEG)
    o_ref[0, 2] = jnp.full((T, T), NEG, F32)


def _bias_prompt(rel_bias, T):
    H = rel_bias.shape[1]
    first_bucket, segs = _bucket_segments(-2 * T, T)
    assert all(lo > -T for lo, _ in segs), "bias must be constant beyond one tile"
    tiles = pl.pallas_call(
        functools.partial(_bias_prompt_kernel, T=T, first_bucket=first_bucket, segs=segs),
        grid=(H,),
        in_specs=[pl.BlockSpec(memory_space=pltpu.SMEM)],
        out_specs=pl.BlockSpec((1, 3, T, T), lambda h: (h, 0, 0, 0)),
        out_shape=jax.ShapeDtypeStruct((H, 3, T, T), F32),
        compiler_params=_cparams(("arbitrary",)),
        name="bias_prompt",
    )(rel_bias)
    return tiles, first_bucket


def _bias_sample_kernel(rb_ref, bp_ref, bn_ref, *, Ls, P, first_bucket, segs):
    h = pl.program_id(0)
    qq = jnp.bitwise_and(lax.broadcasted_iota(jnp.int32, (2 * Ls, P), 0), Ls - 1)
    kk = lax.broadcasted_iota(jnp.int32, (2 * Ls, P), 1)
    bp_ref[0] = _bias_from_rel(kk - P - qq, rb_ref, h, first_bucket, segs)
    qn = jnp.bitwise_and(lax.broadcasted_iota(jnp.int32, (2 * Ls, Ls), 0), Ls - 1)
    kn = lax.broadcasted_iota(jnp.int32, (2 * Ls, Ls), 1)
    bn_ref[0] = _bias_from_rel(kn - qn, rb_ref, h, first_bucket, segs)


def _bias_sample(rel_bias, Ls, P):
    H = rel_bias.shape[1]
    first_bucket, segs = _bucket_segments(-(P + Ls), Ls)
    return pl.pallas_call(
        functools.partial(_bias_sample_kernel, Ls=Ls, P=P, first_bucket=first_bucket, segs=segs),
        grid=(H,),
        in_specs=[pl.BlockSpec(memory_space=pltpu.SMEM)],
        out_specs=[pl.BlockSpec((1, 2 * Ls, P), lambda h: (h, 0, 0)),
                   pl.BlockSpec((1, 2 * Ls, Ls), lambda h: (h, 0, 0))],
        out_shape=[jax.ShapeDtypeStruct((H, 2 * Ls, P), F32),
                   jax.ShapeDtypeStruct((H, 2 * Ls, Ls), F32)],
        compiler_params=_cparams(("arbitrary",)),
        name="bias_sample",
    )(rel_bias)


def _in_proj_kernel(x_ref, g_ref, w_ref, z_ref, f_ref, k_ref, v_ref, h_scr, *, head_layout):
    sec = jnp.right_shift(pl.program_id(1), 1)

    @pl.when(pl.program_id(1) == 0)
    def _():
        h_scr[...] = _rms(x_ref[...], g_ref[...]).astype(BF16)

    def half_section():
        return jnp.dot(h_scr[...], w_ref[0], preferred_element_type=F32)

    def store_heads(ref, acc):
        if head_layout:
            for hh in range(N_HEADS // 2):
                ref[0, hh] = acc[:, hh * HEAD_W:(hh + 1) * HEAD_W]
        else:
            ref[...] = acc

    @pl.when(sec == SEC_HF)
    def _():
        f_ref[...] = half_section()

    @pl.when(sec == SEC_DK)
    def _():
        store_heads(k_ref, half_section())

    @pl.when(sec == SEC_DV)
    def _():
        store_heads(v_ref, half_section())

    @pl.when((sec != SEC_HF) & (sec != SEC_DK) & (sec != SEC_DV))
    def _():
        z_ref[0] = half_section().astype(BF16)


def _zslot(sec):
    return (sec - (sec >= SEC_HF).astype(jnp.int32) - (sec >= SEC_DK).astype(jnp.int32)
            - (sec >= SEC_DV).astype(jnp.int32))


def _col_blocks(w, width):
    K, N = w.shape
    return jnp.transpose(w.astype(BF16).reshape(K, N // width, width), (1, 0, 2))


def _zhalf(j):
    sec = jnp.right_shift(j, 1)
    is_f32 = (sec == SEC_HF) | (sec == SEC_DK) | (sec == SEC_DV)
    return jnp.where(is_f32, 1, jnp.bitwise_and(j, 1))


def _resident_half(sec_id):
    return lambda j: (j >= 2 * sec_id + 1).astype(jnp.int32)


def _in_proj(x2d, g, w_bf, *, tm, rows_per_batch, head_layout):
    N = x2d.shape[0]
    nt = N // tm
    half = D_MODEL // 2
    hf, hk, hv = _resident_half(SEC_HF), _resident_half(SEC_DK), _resident_half(SEC_DV)
    if head_layout:
        B = N // rows_per_batch
        nlt = rows_per_batch // tm
        kv_shape = jax.ShapeDtypeStruct((B, N_HEADS, rows_per_batch, HEAD_W), F32)
        kv_spec = lambda hh: pl.BlockSpec((1, N_HEADS // 2, tm, HEAD_W),
                                          lambda i, j: (i // nlt, hh(j), i % nlt, 0))
    else:
        kv_shape = jax.ShapeDtypeStruct((N, D_MODEL), F32)
        kv_spec = lambda hh: pl.BlockSpec((tm, half), lambda i, j: (i, hh(j)))
    return pl.pallas_call(
        functools.partial(_in_proj_kernel, head_layout=head_layout),
        grid=(nt, 2 * N_SEC),
        in_specs=[pl.BlockSpec((tm, D_MODEL), lambda i, j: (i, 0)),
                  pl.BlockSpec((1, D_MODEL), lambda i, j: (0, 0)),
                  pl.BlockSpec((1, D_MODEL, half), lambda i, j: (j, 0, 0))],
        out_specs=[pl.BlockSpec((1, tm, half), lambda i, j: (_zslot(jnp.right_shift(j, 1)), i, _zhalf(j))),
                   pl.BlockSpec((tm, half), lambda i, j: (i, hf(j))),
                   kv_spec(hk), kv_spec(hv)],
        out_shape=[jax.ShapeDtypeStruct((N_ZSLOT, N, D_MODEL), BF16),
                   jax.ShapeDtypeStruct((N, D_MODEL), F32),
                   kv_shape, kv_shape],
        scratch_shapes=[pltpu.VMEM((tm, D_MODEL), BF16)],
        compiler_params=_cparams(("arbitrary", "arbitrary")),
        name="in_proj",
    )(x2d, g, w_bf)


def _block_diag(a, b):
    za = jnp.zeros((a.shape[0], b.shape[1]), a.dtype)
    zb = jnp.zeros((b.shape[0], a.shape[1]), b.dtype)
    return jnp.concatenate([jnp.concatenate([a, za], axis=1), jnp.concatenate([zb, b], axis=1)], axis=0)


def _hgrn_kernel(q_ref, f_ref, i_ref, g_ref, lbl_ref, on_ref, s0_ref, ya_ref, sn_ref,
                 st_scr, sc_scr, o_scr, upd_scr, *, TL, C):
    lt = pl.program_id(1)
    nsub = C // SUB
    lbl = lbl_ref[...]
    e = jnp.exp(lbl - jnp.max(lbl, axis=0, keepdims=True))
    lb_all = e[0:1] / jnp.sum(e, axis=0, keepdims=True)

    @pl.when(lt == 0)
    def _():
        for p in range(N_HEADS // 2):
            st_scr[p] = jnp.concatenate([s0_ref[0, 2 * p].T, s0_ref[0, 2 * p + 1].T], axis=1)

    crow = lax.broadcasted_iota(jnp.int32, (C, D_MODEL), 0)
    width = max(nsub * C, HEAD_W)
    wrow = lax.broadcasted_iota(jnp.int32, (C, width), 0)
    wcol = lax.broadcasted_iota(jnp.int32, (C, width), 1)
    base = jnp.bitwise_and(wrow, -SUB) * (C // SUB)
    keep = (wcol >= base) & (wcol <= base + wrow)
    keep2 = jnp.concatenate([keep, keep], axis=1)
    plane = lax.broadcasted_iota(jnp.int32, (HEAD_W, PAIR_W), 1)
    onorm = on_ref[...]
    nt = (((1,), (1,)), ((), ()))

    def chunk(c, carry):
        r0 = pl.multiple_of(c * C, C)
        f = f_ref[0, pl.ds(r0, C), :]
        fg = lb_all + (1.0 - lb_all) * _sigmoid(f)
        logf = jnp.log(fg)
        kk = 1.0 - fg
        b = logf
        sh = 1
        while sh < C:
            if sh % 8 == 0:
                shifted = jnp.concatenate([jnp.zeros((sh, D_MODEL), F32), b[:C - sh]], axis=0)
            else:
                shifted = jnp.where(crow >= sh, pltpu.roll(b, sh, 0), 0.0)
            b = b + shifted
            sh *= 2
        r_blocks = [jnp.zeros((SUB, D_MODEL), F32)]
        for i in range(1, nsub):
            r_blocks.append(jnp.broadcast_to(b[SUB * i - 1:SUB * i, :], (SUB, D_MODEL)))
        r = r_blocks[0] if nsub == 1 else jnp.concatenate(r_blocks, axis=0)
        b_last = b[C - 1:C, :]
        q = q_ref[0, 0, pl.ds(r0, C), :].astype(F32)
        q_inter = (q * jnp.exp(b)).astype(BF16)
        q_intra = (q * jnp.exp(b - r)).astype(BF16)
        k_state = (kk * jnp.exp(b_last - b)).astype(BF16)
        k_sub = []
        for i in range(nsub):
            n = SUB * (i + 1)
            ki = (kk[:n] * jnp.exp(r[SUB * i:SUB * i + 1, :] - b[:n])).astype(BF16)
            if n < C:
                ki = jnp.concatenate([ki, jnp.zeros((C - n, D_MODEL), BF16)], axis=0)
            k_sub.append(ki)
        st_decay = jnp.exp(b_last)
        g = g_ref[0, 0, pl.ds(r0, C), :].astype(F32)
        gate = g * _sigmoid(g)
        pairs = range(N_HEADS // 2)
        pcs = [slice(p * PAIR_W, (p + 1) * PAIR_W) for p in pairs]
        hcs = [[slice((2 * p + a) * HEAD_W, (2 * p + a + 1) * HEAD_W) for a in range(2)] for p in pairs]
        vTs = []
        for p in pairs:
            k4 = []
            vT = []
            for a in range(2):
                pieces = [k[:, hcs[p][a]] for k in k_sub]
                if nsub * C < width:
                    pieces.append(jnp.zeros((width - nsub * C, HEAD_W), BF16))
                k4.append(jnp.concatenate(pieces, axis=0))
                v = i_ref[0, 0, pl.ds(r0, C), hcs[p][a]].astype(F32)
                vrows = [v] * (HEAD_W // C) if nsub > 1 else [v, jnp.zeros((HEAD_W - C, HEAD_W), F32)]
                vT.append(jnp.concatenate(vrows, axis=0).T.astype(BF16))
            vTs.append(vT)
            sc_scr[p] = lax.dot_general(q_intra[:, pcs[p]], _block_diag(k4[0], k4[1]), nt,
                                        preferred_element_type=F32)
        for p in pairs:
            kpad = [jnp.concatenate([k_state[:, hcs[p][a]], jnp.zeros((HEAD_W - C, HEAD_W), BF16)], axis=0)
                    for a in range(2)]
            upd_scr[p] = jnp.dot(jnp.concatenate(vTs[p], axis=1), _block_diag(kpad[0], kpad[1]),
                                 preferred_element_type=F32)
        for p in pairs:
            w = jnp.where(keep2, sc_scr[p], 0.0).astype(BF16)
            stp = st_scr[p]
            stb = stp.astype(BF16)
            vfull = [jnp.concatenate([t] * (width // HEAD_W), axis=1) for t in vTs[p]]
            zw = jnp.zeros((HEAD_W, width), BF16)
            rhs_o = jnp.concatenate(
                [jnp.concatenate([vfull[0], zw, jnp.where(plane < HEAD_W, stb, 0.0)], axis=1),
                 jnp.concatenate([zw, vfull[1], jnp.where(plane >= HEAD_W, stb, 0.0)], axis=1)], axis=0)
            o_scr[p] = lax.dot_general(jnp.concatenate([w, q_inter[:, pcs[p]]], axis=1), rhs_o, nt,
                                       preferred_element_type=F32)
            st_scr[p] = stp * st_decay[:, pcs[p]] + upd_scr[p]
        for p in pairs:
            for a in range(2):
                o = o_scr[p, :, a * HEAD_W:(a + 1) * HEAD_W]
                ya_ref[0, pl.ds(r0, C), hcs[p][a]] = (_rms(o, onorm) * gate[:, hcs[p][a]]).astype(BF16)
        return carry

    lax.fori_loop(0, TL // C, chunk, 0)

    @pl.when(lt == pl.num_programs(1) - 1)
    def _():
        for p in range(N_HEADS // 2):
            stp = st_scr[p]
            for a in range(2):
                sn_ref[0, 2 * p + a] = stp[:, a * HEAD_W:(a + 1) * HEAD_W].T


def _hgrn(z4, f3, lb_logits, onorm, s0, *, L, tl):
    B = z4.shape[1]
    C = min(CHUNK, L)
    slot = lambda s: pl.BlockSpec((1, 1, tl, D_MODEL), lambda b, t: (s, b, t, 0))
    row = pl.BlockSpec((1, tl, D_MODEL), lambda b, t: (b, t, 0))
    state = pl.BlockSpec((1, N_HEADS, HEAD_W, HEAD_W), lambda b, t: (b, 0, 0, 0))
    return pl.pallas_call(
        functools.partial(_hgrn_kernel, TL=tl, C=C),
        grid=(B, L // tl),
        in_specs=[slot(Z_HQ), row, slot(Z_HI), slot(Z_HG),
                  pl.BlockSpec(lb_logits.shape, lambda b, t: (0, 0)),
                  pl.BlockSpec((1, HEAD_W), lambda b, t: (0, 0)),
                  state],
        out_specs=[row, state],
        out_shape=[jax.ShapeDtypeStruct((B, L, D_MODEL), BF16),
                   jax.ShapeDtypeStruct((B, N_HEADS, HEAD_W, HEAD_W), F32)],
        scratch_shapes=[pltpu.VMEM((N_HEADS // 2, HEAD_W, PAIR_W), F32),
                        pltpu.VMEM((N_HEADS // 2, C, 2 * max((C // SUB) * C, HEAD_W)), F32),
                        pltpu.VMEM((N_HEADS // 2, C, PAIR_W), F32),
                        pltpu.VMEM((N_HEADS // 2, HEAD_W, PAIR_W), F32)],
        compiler_params=_cparams(("arbitrary", "arbitrary")),
        name="hgrn",
    )(z4, f3, z4, z4, lb_logits, onorm, s0)


def _diff_lambda(lp_ref):
    lp = lp_ref[...]
    a = jnp.sum(lp[0:1] * lp[1:2], axis=-1, keepdims=True)
    b = jnp.sum(lp[2:3] * lp[3:4], axis=-1, keepdims=True)
    return jnp.exp(a) - jnp.exp(b) + LAMBDA_INIT


def _split_maps(q):
    lane = lax.broadcasted_iota(jnp.int32, q.shape, 1)
    return jnp.where(lane < DIFF_DH, q, 0.0), jnp.where(lane >= DIFF_DH, q, 0.0)


def _diff_attn_prompt_kernel(rb_ref, q_ref, k_ref, v_ref, bias_ref, lp_ref, sub_ref, o_ref, kb, vT, acc, m_scr, s_scr,
                             *, T, nk, G, far_bucket):
    hg = pl.program_id(1)
    qi = pl.program_id(2)
    nt = (((1,), (1,)), ((), ()))

    @pl.when(qi == 0)
    def _():
        ones = jnp.ones((ONES_ROWS, T), BF16)
        for g in range(G):
            for j in range(nk):
                kb[g, j] = (k_ref[0, g, j * T:(j + 1) * T, :] * (DIFF_SCALE * LOG2E)).astype(BF16)
                vT[g, j, :HEAD_W, :] = v_ref[0, g, j * T:(j + 1) * T, :].T.astype(BF16)
                vT[g, j, HEAD_W:, :] = ones

    qs = []
    for g in range(G):
        qa, qb = _split_maps(q_ref[0, 0, :, g * HEAD_W:(g + 1) * HEAD_W])
        qs.append((qa, qb))
    acc[...] = jnp.zeros(acc.shape, F32)
    m_scr[...] = jnp.full(m_scr.shape, NEG, F32)

    def tile(j, bias_slot):
        for g in range(G):
            kt = kb[g, j]
            for mp in range(2):
                s_scr[2 * g + mp] = lax.dot_general(kt, qs[g][mp], nt, preferred_element_type=F32)
        for g in range(G):
            vt = vT[g, j]
            if bias_slot is None:
                shift = rb_ref[far_bucket, hg * G + g] * LOG2E
            else:
                bias = bias_ref[g, bias_slot]
            for mp in range(2):
                idx = 2 * g + mp
                m = m_scr[idx]
                s = s_scr[idx]
                if bias_slot is None:
                    mn = jnp.maximum(m, jnp.max(s, axis=0, keepdims=True) + shift)
                    p = jnp.exp2(s - (mn - shift))
                else:
                    s = s + bias
                    mn = jnp.maximum(m, jnp.max(s, axis=0, keepdims=True))
                    p = jnp.exp2(s - mn)
                m_scr[idx] = mn
                acc[idx] = jnp.exp2(m - mn) * acc[idx] + jnp.dot(vt, p.astype(BF16),
                                                                  preferred_element_type=F32)

    def far(j, c):
        tile(j, None)
        return c

    lax.fori_loop(0, qi - 1, far, 0)

    @pl.when(qi >= 1)
    def _():
        tile(qi - 1, 0)

    tile(qi, 1)
    lam = _diff_lambda(lp_ref)
    for g in range(G):
        a1 = acc[2 * g]
        a2 = acc[2 * g + 1]
        o = (a1[:HEAD_W] / a1[HEAD_W:HEAD_W + 1] - lam * (a2[:HEAD_W] / a2[HEAD_W:HEAD_W + 1])).T
        o_ref[0, :, g * HEAD_W:(g + 1) * HEAD_W] = (_rms(o, sub_ref[...]) * (1.0 - LAMBDA_INIT)).astype(BF16)


def _diff_attn_prompt(z4, k4, v4, rel_bias, bias, lp, subln, *, T, G, far_bucket):
    B, H, L, _ = k4.shape
    nk = L // T
    W = G * HEAD_W
    kv_spec = pl.BlockSpec((1, G, L, HEAD_W), lambda b, h, qi: (b, h, 0, 0))
    return pl.pallas_call(
        functools.partial(_diff_attn_prompt_kernel, T=T, nk=nk, G=G, far_bucket=far_bucket),
        grid=(B, H // G, nk),
        in_specs=[pl.BlockSpec(memory_space=pltpu.SMEM),
                  pl.BlockSpec((1, 1, T, W), lambda b, h, qi: (Z_DQ, b, qi, h)),
                  kv_spec, kv_spec,
                  pl.BlockSpec((G, 3, T, T), lambda b, h, qi: (h, 0, 0, 0)),
                  pl.BlockSpec(lp.shape, lambda b, h, qi: (0, 0)),
                  pl.BlockSpec((1, HEAD_W), lambda b, h, qi: (0, 0))],
        out_specs=pl.BlockSpec((1, T, W), lambda b, h, qi: (b, qi, h)),
        out_shape=jax.ShapeDtypeStruct((B, L, D_MODEL), BF16),
        scratch_shapes=[pltpu.VMEM((G, nk, T, HEAD_W), BF16),
                        pltpu.VMEM((G, nk, HEAD_W + ONES_ROWS, T), BF16),
                        pltpu.VMEM((2 * G, HEAD_W + ONES_ROWS, T), F32),
                        pltpu.VMEM((2 * G, 1, T), F32),
                        pltpu.VMEM((2 * G, T, T), F32)],
        compiler_params=_cparams(("arbitrary", "arbitrary", "arbitrary")),
        name="diff_attn_prompt",
    )(rel_bias, z4, k4, v4, bias, lp, subln)


def _diff_attn_sample_kernel(q_ref, kn_ref, vn_ref, kc_ref, vc_ref, bp_ref, bn_ref, lp_ref, sub_ref, o_ref,
                             *, Ls, hb):
    lam = _diff_lambda(lp_ref)
    nt = (((1,), (1,)), ((), ()))
    for hh in range(hb):
        cols = slice(hh * HEAD_W, (hh + 1) * HEAD_W)
        qa, qb = _split_maps(q_ref[0, 0, :, cols].astype(F32) * DIFF_SCALE)
        q2 = jnp.concatenate([qa, qb], axis=0).astype(BF16)
        kp = kc_ref[0, hh].astype(BF16)
        vp = vc_ref[0, hh].astype(BF16)
        kn = kn_ref[0, :, cols].astype(BF16)
        vn = vn_ref[0, :, cols].astype(BF16)
        s = lax.dot_general(q2, kp, nt, preferred_element_type=F32) + bp_ref[hh]
        sn = lax.dot_general(q2, kn, nt, preferred_element_type=F32) + bn_ref[hh]
        m = jnp.maximum(jnp.max(s, axis=-1, keepdims=True), jnp.max(sn, axis=-1, keepdims=True))
        p = jnp.exp(s - m)
        pn = jnp.exp(sn - m)
        l = jnp.sum(p, axis=-1, keepdims=True) + jnp.sum(pn, axis=-1, keepdims=True)
        o2 = (jnp.dot(p.astype(BF16), vp, preferred_element_type=F32)
              + jnp.dot(pn.astype(BF16), vn, preferred_element_type=F32)) / l
        o = o2[:Ls] - lam * o2[Ls:]
        o_ref[0, :, cols] = (_rms(o, sub_ref[...]) * (1.0 - LAMBDA_INIT)).astype(BF16)


def _diff_attn_sample(z4, kn3, vn3, kc, vc, bias_p, bias_n, lp, subln, *, hb):
    B, H, P, _ = kc.shape
    Ls = kn3.shape[1]
    W = hb * HEAD_W
    cache_spec = pl.BlockSpec((1, hb, P, HEAD_W), lambda b, g: (b, g, 0, 0))
    new_spec = pl.BlockSpec((1, Ls, W), lambda b, g: (b, 0, g))
    return pl.pallas_call(
        functools.partial(_diff_attn_sample_kernel, Ls=Ls, hb=hb),
        grid=(B, H // hb),
        in_specs=[pl.BlockSpec((1, 1, Ls, W), lambda b, g: (Z_DQ, b, 0, g)),
                  new_spec, new_spec, cache_spec, cache_spec,
                  pl.BlockSpec((hb, 2 * Ls, P), lambda b, g: (g, 0, 0)),
                  pl.BlockSpec((hb, 2 * Ls, Ls), lambda b, g: (g, 0, 0)),
                  pl.BlockSpec(lp.shape, lambda b, g: (0, 0)),
                  pl.BlockSpec((1, HEAD_W), lambda b, g: (0, 0))],
        out_specs=pl.BlockSpec((1, Ls, W), lambda b, g: (b, 0, g)),
        out_shape=jax.ShapeDtypeStruct((B, Ls, D_MODEL), BF16),
        compiler_params=_cparams(("arbitrary", "arbitrary")),
        name="diff_attn_sample",
    )(z4, kn3, vn3, kc, vc, bias_p, bias_n, lp, subln)


def _mem_kv_kernel(m_ref, g_ref, w_ref, k_ref, v_ref):
    h = _rms(m_ref[0], g_ref[...]).astype(BF16)
    kv = jnp.dot(h, w_ref[...], preferred_element_type=F32)
    for hh in range(MEM_HEADS):
        k_ref[0, hh] = kv[:, hh * MEM_DH:(hh + 1) * MEM_DH]
        v_ref[0, hh] = kv[:, D_MODEL + hh * MEM_DH:D_MODEL + (hh + 1) * MEM_DH]


def _mem_kv(mem, g, w_bf):
    B, M, _ = mem.shape
    out = jax.ShapeDtypeStruct((B, MEM_HEADS, M, MEM_DH), F32)
    spec = pl.BlockSpec((1, MEM_HEADS, M, MEM_DH), lambda b: (b, 0, 0, 0))
    return pl.pallas_call(
        _mem_kv_kernel,
        grid=(B,),
        in_specs=[pl.BlockSpec((1, M, D_MODEL), lambda b: (b, 0, 0)),
                  pl.BlockSpec((1, D_MODEL), lambda b: (0, 0)),
                  pl.BlockSpec(w_bf.shape, lambda b: (0, 0))],
        out_specs=[spec, spec],
        out_shape=[out, out],
        compiler_params=_cparams(("arbitrary",)),
        name="mem_kv",
    )(mem, g, w_bf)


def _mem_attn_kernel(q_ref, k_ref, v_ref, o_ref):
    nt = (((1,), (1,)), ((), ()))
    for hh in range(MEM_HEADS):
        cols = slice(hh * MEM_DH, (hh + 1) * MEM_DH)
        q = q_ref[0, 0, :, cols]
        s = lax.dot_general(q, k_ref[0, hh].astype(BF16), nt, preferred_element_type=F32) * (MEM_DH ** -0.5)
        p = jnp.exp(s - jnp.max(s, axis=-1, keepdims=True))
        p = p / jnp.sum(p, axis=-1, keepdims=True)
        o = jnp.dot(p.astype(BF16), v_ref[0, hh].astype(BF16), preferred_element_type=F32)
        o_ref[0, :, cols] = o.astype(BF16)


def _mem_attn(z4, mk, mv, *, tl):
    _, B, L, _ = z4.shape
    M = mk.shape[2]
    kv_spec = pl.BlockSpec((1, MEM_HEADS, M, MEM_DH), lambda b, t: (b, 0, 0, 0))
    return pl.pallas_call(
        _mem_attn_kernel,
        grid=(B, L // tl),
        in_specs=[pl.BlockSpec((1, 1, tl, D_MODEL), lambda b, t: (Z_MQ, b, t, 0)), kv_spec, kv_spec],
        out_specs=pl.BlockSpec((1, tl, D_MODEL), lambda b, t: (b, t, 0)),
        out_shape=jax.ShapeDtypeStruct((B, L, D_MODEL), BF16),
        compiler_params=_cparams(("arbitrary", "arbitrary")),
        name="mem_attn",
    )(z4, mk, mv)


def _merge_kernel(x_ref, ya_ref, yb_ref, yc_ref, g0_ref, g1_ref, g2_ref, wb_ref, wo_ref, nf_ref,
                  x1_ref, h2_ref):
    merged = None
    for n, (y_ref, gate_ref) in enumerate(((ya_ref, g0_ref), (yb_ref, g1_ref), (yc_ref, g2_ref))):
        proj = jnp.dot(y_ref[...], wb_ref[n], preferred_element_type=F32)
        term = proj * _sigmoid(gate_ref[0].astype(F32))
        merged = term if merged is None else merged + term
    x1 = x_ref[...] + jnp.dot(merged.astype(BF16), wo_ref[...], preferred_element_type=F32)
    x1_ref[...] = x1
    h2_ref[...] = _rms(x1, nf_ref[...]).astype(BF16)


def _merge(x2d, ya, yb, yc, z3, wb_bf, wo_bf, nf, *, tm):
    N = x2d.shape[0]
    row = pl.BlockSpec((tm, D_MODEL), lambda i: (i, 0))
    gate = lambda n: pl.BlockSpec((1, tm, D_MODEL), lambda i: (Z_G0 + n, i, 0))
    return pl.pallas_call(
        _merge_kernel,
        grid=(N // tm,),
        in_specs=[row, row, row, row, gate(0), gate(1), gate(2),
                  pl.BlockSpec(wb_bf.shape, lambda i: (0, 0, 0)),
                  pl.BlockSpec(wo_bf.shape, lambda i: (0, 0)),
                  pl.BlockSpec((1, D_MODEL), lambda i: (0, 0))],
        out_specs=[row, row],
        out_shape=[jax.ShapeDtypeStruct((N, D_MODEL), F32), jax.ShapeDtypeStruct((N, D_MODEL), BF16)],
        compiler_params=_cparams(("arbitrary",)),
        name="merge",
    )(x2d, ya, yb, yc, z3, z3, z3, wb_bf, wo_bf, nf)


def _ffn_kernel(h_ref, x_ref, cp_ref, wu_ref, cw_ref, cb_ref, wd_ref, nf_ref, y_ref, cn_ref, carry, u_scr, act_scr,
                *, tm, cw, d_ff):
    lt = pl.program_id(1)

    @pl.when(lt == 0)
    def _():
        carry[...] = cp_ref[0]

    h = h_ref[0]
    edge = 8
    erow = lax.broadcasted_iota(jnp.int32, (edge, cw), 0)
    for jc in range(d_ff // cw):
        slot = jc % 2
        bases = (jc * cw, d_ff + jc * cw)
        for hf, base in enumerate(bases):
            u_scr[slot, hf] = jnp.dot(h, wu_ref[:, base:base + cw], preferred_element_type=F32)
        halves = []
        for hf, base in enumerate(bases):
            cols = slice(base, base + cw)
            u = u_scr[slot, hf]
            w0, w1, w2, bb = cw_ref[0:1, cols], cw_ref[1:2, cols], cw_ref[2:3, cols], cb_ref[:, cols]
            u1 = pltpu.roll(u, 1, 0)
            u2 = pltpu.roll(u, 2, 0)
            c = bb + w0 * u2 + w1 * u1 + w2 * u
            p2 = carry[0:1, cols]
            p1 = carry[1:2, cols]
            u1e = jnp.where(erow == 0, p1, u1[:edge])
            u2e = jnp.where(erow == 0, p2, jnp.where(erow == 1, p1, u2[:edge]))
            ce = bb + w0 * u2e + w1 * u1e + w2 * u[:edge]
            halves.append(jnp.concatenate([ce, c[edge:]], axis=0))
            tail = u[tm - (FFN_CONV - 1):, :]
            carry[:, cols] = tail
            cn_ref[0, :, cols] = tail
        gate, val = halves
        act_scr[:, jc * cw:(jc + 1) * cw] = (gate * _sigmoid(gate) * val).astype(BF16)
    y = x_ref[0] + jnp.dot(act_scr[...], wd_ref[...], preferred_element_type=F32)
    y_ref[0] = _rms(y, nf_ref[...])


def _ffn(h2, x1, conv_prev, wu_bf, conv_w, conv_b, wd_bf, nf, *, tm, cw=256):
    B, L, _ = x1.shape
    d_ff = wd_bf.shape[0]
    row = pl.BlockSpec((1, tm, D_MODEL), lambda b, t: (b, t, 0))
    state = pl.BlockSpec((1, FFN_CONV - 1, 2 * d_ff), lambda b, t: (b, 0, 0))
    const = lambda a: pl.BlockSpec(a.shape, lambda b, t: (0,) * a.ndim)
    return pl.pallas_call(
        functools.partial(_ffn_kernel, tm=tm, cw=cw, d_ff=d_ff),
        grid=(B, L // tm),
        in_specs=[row, row, state, const(wu_bf), const(conv_w), const(conv_b), const(wd_bf), const(nf)],
        out_specs=[row, state],
        out_shape=[jax.ShapeDtypeStruct((B, L, D_MODEL), F32),
                   jax.ShapeDtypeStruct((B, FFN_CONV - 1, 2 * d_ff), F32)],
        scratch_shapes=[pltpu.VMEM((FFN_CONV - 1, 2 * d_ff), F32),
                        pltpu.VMEM((2, 2, tm, cw), F32),
                        pltpu.VMEM((tm, d_ff), BF16)],
        compiler_params=_cparams(("arbitrary", "arbitrary")),
        name="ffn",
    )(h2, x1, conv_prev, wu_bf, conv_w, conv_b, wd_bf, nf)


def _layer(x, W, *, hgrn_s0, conv_prev, mem_k, mem_v, attn, tm_proj, tm_merge, tm_ffn, tl_mem, head_layout):
    B, L, _ = x.shape
    N = B * L
    x2d = x.reshape(N, D_MODEL)
    z3, f2d, k_new, v_new = _in_proj(x2d, W["norm_attn"], W["w_in"], tm=tm_proj, rows_per_batch=L,
                                     head_layout=head_layout)
    z4 = z3.reshape(N_ZSLOT, B, L, D_MODEL)
    ya, s_new = _hgrn(z4, f2d.reshape(B, L, D_MODEL), W["lb_logits"], W["hgrn_onorm"], hgrn_s0, L=L,
                      tl=min(L, 512))
    yb, k_new, v_new = attn(z4, k_new, v_new)
    yc = _mem_attn(z4, mem_k, mem_v, tl=tl_mem)
    x1, h2 = _merge(x2d, ya.reshape(N, D_MODEL), yb.reshape(N, D_MODEL), yc.reshape(N, D_MODEL), z3,
                    W["w_branch"], W["w_out"], W["norm_ffn"], tm=tm_merge)
    y, conv_new = _ffn(h2.reshape(B, L, D_MODEL), x1.reshape(B, L, D_MODEL), conv_prev, W["w_up"],
                       W["conv_w"], W["conv_b"], W["w_down"], W["norm_final"], tm=tm_ffn)
    return y, k_new, v_new, s_new, conv_new


def kernel(x_prompt, x_sample, mem_prompt, cache_diff_k, cache_diff_v, cache_mem_k, cache_mem_v, state_hgrn, state_ffn_conv, rel_bias, hgrn_lb_logits, norm_attn, w_in, hgrn_onorm, diff_lambda, diff_subln, mem_norm, w_mem_kv, w_branch, w_out, norm_ffn, w_up, conv_w, conv_b, w_down, norm_final):
    assert w_in.shape[0] == 1, "single-layer trunk"
    Bp, Lp, _ = x_prompt.shape
    Bs, Ls, _ = x_sample.shape
    P = cache_diff_k.shape[3]
    d_ff2 = w_up.shape[2]
    row = lambda a: a.reshape(1, -1)
    W = dict(norm_attn=row(norm_attn[0]), w_in=_col_blocks(w_in[0], D_MODEL // 2), lb_logits=hgrn_lb_logits,
             hgrn_onorm=row(hgrn_onorm[0]), w_branch=w_branch[0].astype(BF16), w_out=w_out[0].astype(BF16),
             norm_ffn=row(norm_ffn[0]), w_up=w_up[0].astype(BF16), conv_w=conv_w[0], conv_b=row(conv_b[0]),
             w_down=w_down[0].astype(BF16), norm_final=row(norm_final))
    lp = diff_lambda[0]
    subln = row(diff_subln[0])

    T = 512
    bias_p, far_bucket = _bias_prompt(rel_bias, T)
    mk, mv = _mem_kv(mem_prompt, row(mem_norm[0]), w_mem_kv[0].astype(BF16))

    def attn_prompt(z4, k4, v4):
        yb = _diff_attn_prompt(z4, k4, v4, rel_bias, bias_p, lp, subln, T=T, G=2, far_bucket=far_bucket)
        return yb, k4, v4

    yp, pk, pv, ps, pc = _layer(
        x_prompt, W, hgrn_s0=jnp.zeros((Bp, N_HEADS, HEAD_W, HEAD_W), F32),
        conv_prev=jnp.zeros((Bp, FFN_CONV - 1, d_ff2), F32), mem_k=mk, mem_v=mv, attn=attn_prompt,
        tm_proj=1024, tm_merge=512, tm_ffn=512, tl_mem=512, head_layout=True)

    bias_sp, bias_sn = _bias_sample(rel_bias, Ls, P)

    def attn_sample(z4, k2d, v2d):
        yb = _diff_attn_sample(z4, k2d.reshape(Bs, Ls, D_MODEL), v2d.reshape(Bs, Ls, D_MODEL),
                               cache_diff_k[0], cache_diff_v[0], bias_sp, bias_sn, lp, subln, hb=4)
        heads = lambda a: jnp.transpose(a.reshape(Bs, Ls, N_HEADS, HEAD_W), (0, 2, 1, 3))
        return yb, heads(k2d), heads(v2d)

    ys, sk, sv, ss, sc = _layer(
        x_sample, W, hgrn_s0=state_hgrn[0], conv_prev=state_ffn_conv[0], mem_k=cache_mem_k[0],
        mem_v=cache_mem_v[0], attn=attn_sample,
        tm_proj=Bs * Ls, tm_merge=Bs * Ls, tm_ffn=Ls, tl_mem=Ls, head_layout=False)

    return (yp, ys, pk[None], pv[None], ps[None], pc[None], mk[None], mv[None],
            sk[None], sv[None], ss[None], sc[None])
```

```python
import functools
import math

import numpy as np
import jax
import jax.numpy as jnp
from jax import lax
from jax.experimental import pallas as pl
from jax.experimental.pallas import tpu as pltpu

F32 = jnp.float32
BF16 = jnp.bfloat16

D_MODEL = 1024
CHUNK = 64
SUB = 16
HEAD_W = 128
N_HEADS = D_MODEL // HEAD_W
PAIR_W = 2 * HEAD_W
DIFF_DH = 64
DIFF_SCALE = DIFF_DH ** -0.5
MEM_HEADS = 4
MEM_DH = D_MODEL // MEM_HEADS
REL_BUCKETS = 32
REL_MAX_DIST = 128
N_BRANCH = 3
FFN_CONV = 3
RMS_EPS = 1e-6
LAMBDA_INIT = 0.8 - 0.6 * math.exp(-0.3 * 0)
NEG = -1e30
LOG2E = math.log2(math.e)
ONES_ROWS = 16

SEC_HQ, SEC_HF, SEC_HI, SEC_HG, SEC_DQ, SEC_DK, SEC_DV, SEC_MQ, SEC_G0 = range(9)
N_SEC = 11
Z_HQ, Z_HI, Z_HG, Z_DQ, Z_MQ, Z_G0 = 0, 1, 2, 3, 4, 5
N_ZSLOT = 8

V7X_VMEM_LIMIT = 56 * 1024 * 1024


def _cparams(sem, vmem=V7X_VMEM_LIMIT):
    return pltpu.CompilerParams(dimension_semantics=sem, vmem_limit_bytes=vmem)


def _sigmoid(x):
    return 0.5 * jnp.tanh(0.5 * x) + 0.5


def _rms(x, g):
    ms = jnp.mean(x * x, axis=-1, keepdims=True)
    return x * lax.rsqrt(ms + RMS_EPS) * g


def _np_bucket(rel):
    nb = REL_BUCKETS // 2
    ret = np.where(rel > 0, nb, 0)
    n = np.abs(rel)
    max_exact = nb // 2
    large = max_exact + (np.log(np.maximum(n, 1).astype(np.float32) / max_exact)
                         / math.log(REL_MAX_DIST / max_exact) * (nb - max_exact)).astype(np.int32)
    large = np.minimum(large, nb - 1)
    return ret + np.where(n < max_exact, n, large)


def _bucket_segments(lo, hi):
    rel = np.arange(lo, hi + 1, dtype=np.int32)
    b = _np_bucket(rel)
    change = np.nonzero(np.diff(b))[0]
    return int(b[0]), [(int(rel[i + 1]), int(b[i + 1])) for i in change]


def _bias_from_rel(rel, rb_ref, h, first_bucket, segs):
    val = jnp.full(rel.shape, rb_ref[first_bucket, h], F32)
    for lo, bk in segs:
        val = jnp.where(rel >= lo, rb_ref[bk, h], val)
    return val


def _bias_prompt_kernel(rb_ref, o_ref, *, T, first_bucket, segs):
    h = pl.program_id(0)
    kk = lax.broadcasted_iota(jnp.int32, (T, T), 0)
    qq = lax.broadcasted_iota(jnp.int32, (T, T), 1)
    o_ref[0, 0] = _bias_from_rel(kk - qq - T, rb_ref, h, first_bucket, segs) * LOG2E
    diag = _bias_from_rel(kk - qq, rb_ref, h, first_bucket, segs) * LOG2E
    o_ref[0, 1] = jnp.where(jnp.bitwise_and(kk, -CHUNK) <= qq, diag, NEG)
    o_ref[0, 2] = jnp.full((T, T), NEG, F32)


def _bias_prompt(rel_bias, T):
    H = rel_bias.shape[1]
    first_bucket, segs = _bucket_segments(-2 * T, T)
    assert all(lo > -T for lo, _ in segs), "bias must be constant beyond one tile"
    tiles = pl.pallas_call(
        functools.partial(_bias_prompt_kernel, T=T, first_bucket=first_bucket, segs=segs),
        grid=(H,),
        in_specs=[pl.BlockSpec(memory_space=pltpu.SMEM)],
        out_specs=pl.BlockSpec((1, 3, T, T), lambda h: (h, 0, 0, 0)),
        out_shape=jax.ShapeDtypeStruct((H, 3, T, T), F32),
        compiler_params=_cparams(("arbitrary",)),
        name="bias_prompt",
    )(rel_bias)
    return tiles, first_bucket


def _bias_sample_kernel(rb_ref, bp_ref, bn_ref, *, Ls, P, first_bucket, segs):
    h = pl.program_id(0)
    qq = jnp.bitwise_and(lax.broadcasted_iota(jnp.int32, (2 * Ls, P), 0), Ls - 1)
    kk = lax.broadcasted_iota(jnp.int32, (2 * Ls, P), 1)
    bp_ref[0] = _bias_from_rel(kk - P - qq, rb_ref, h, first_bucket, segs)
    qn = jnp.bitwise_and(lax.broadcasted_iota(jnp.int32, (2 * Ls, Ls), 0), Ls - 1)
    kn = lax.broadcasted_iota(jnp.int32, (2 * Ls, Ls), 1)
    bn_ref[0] = _bias_from_rel(kn - qn, rb_ref, h, first_bucket, segs)


def _bias_sample(rel_bias, Ls, P):
    H = rel_bias.shape[1]
    first_bucket, segs = _bucket_segments(-(P + Ls), Ls)
    return pl.pallas_call(
        functools.partial(_bias_sample_kernel, Ls=Ls, P=P, first_bucket=first_bucket, segs=segs),
        grid=(H,),
        in_specs=[pl.BlockSpec(memory_space=pltpu.SMEM)],
        out_specs=[pl.BlockSpec((1, 2 * Ls, P), lambda h: (h, 0, 0)),
                   pl.BlockSpec((1, 2 * Ls, Ls), lambda h: (h, 0, 0))],
        out_shape=[jax.ShapeDtypeStruct((H, 2 * Ls, P), F32),
                   jax.ShapeDtypeStruct((H, 2 * Ls, Ls), F32)],
        compiler_params=_cparams(("arbitrary",)),
        name="bias_sample",
    )(rel_bias)


def _in_proj_kernel(x_ref, g_ref, w_ref, z_ref, f_ref, k_ref, v_ref, h_scr, *, head_layout):
    sec = pl.program_id(1)

    @pl.when(sec == 0)
    def _():
        h_scr[...] = _rms(x_ref[...], g_ref[...]).astype(BF16)

    def section():
        return jnp.dot(h_scr[...], w_ref[...], preferred_element_type=F32)

    def store_heads(ref, acc):
        if head_layout:
            for hh in range(N_HEADS):
                ref[0, hh] = acc[:, hh * HEAD_W:(hh + 1) * HEAD_W]
        else:
            ref[...] = acc

    @pl.when(sec == SEC_HF)
    def _():
        f_ref[...] = section()

    @pl.when(sec == SEC_DK)
    def _():
        store_heads(k_ref, section())

    @pl.when(sec == SEC_DV)
    def _():
        store_heads(v_ref, section())

    @pl.when((sec != SEC_HF) & (sec != SEC_DK) & (sec != SEC_DV))
    def _():
        z_ref[0] = section().astype(BF16)


def _zslot(sec):
    return (sec - (sec >= SEC_HF).astype(jnp.int32) - (sec >= SEC_DK).astype(jnp.int32)
            - (sec >= SEC_DV).astype(jnp.int32))


def _in_proj(x2d, g, w_bf, *, tm, rows_per_batch, head_layout):
    N = x2d.shape[0]
    nt = N // tm
    if head_layout:
        B = N // rows_per_batch
        nlt = rows_per_batch // tm
        kv_shape = jax.ShapeDtypeStruct((B, N_HEADS, rows_per_batch, HEAD_W), F32)
        kv_spec = pl.BlockSpec((1, N_HEADS, tm, HEAD_W), lambda i, j: (i // nlt, 0, i % nlt, 0))
    else:
        kv_shape = jax.ShapeDtypeStruct((N, D_MODEL), F32)
        kv_spec = pl.BlockSpec((tm, D_MODEL), lambda i, j: (i, 0))
    return pl.pallas_call(
        functools.partial(_in_proj_kernel, head_layout=head_layout),
        grid=(nt, N_SEC),
        in_specs=[pl.BlockSpec((tm, D_MODEL), lambda i, j: (i, 0)),
                  pl.BlockSpec((1, D_MODEL), lambda i, j: (0, 0)),
                  pl.BlockSpec((D_MODEL, D_MODEL), lambda i, j: (0, j))],
        out_specs=[pl.BlockSpec((1, tm, D_MODEL), lambda i, j: (_zslot(j), i, 0)),
                   pl.BlockSpec((tm, D_MODEL), lambda i, j: (i, 0)),
                   kv_spec, kv_spec],
        out_shape=[jax.ShapeDtypeStruct((N_ZSLOT, N, D_MODEL), BF16),
                   jax.ShapeDtypeStruct((N, D_MODEL), F32),
                   kv_shape, kv_shape],
        scratch_shapes=[pltpu.VMEM((tm, D_MODEL), BF16)],
        compiler_params=_cparams(("arbitrary", "arbitrary")),
        name="in_proj",
    )(x2d, g, w_bf)


def _block_diag(a, b):
    za = jnp.zeros((a.shape[0], b.shape[1]), a.dtype)
    zb = jnp.zeros((b.shape[0], a.shape[1]), b.dtype)
    return jnp.concatenate([jnp.concatenate([a, za], axis=1), jnp.concatenate([zb, b], axis=1)], axis=0)


def _hgrn_kernel(q_ref, f_ref, i_ref, g_ref, lbl_ref, on_ref, s0_ref, ya_ref, sn_ref,
                 st_scr, sc_scr, o_scr, upd_scr, *, TL, C):
    lt = pl.program_id(1)
    nsub = C // SUB
    lbl = lbl_ref[...]
    e = jnp.exp(lbl - jnp.max(lbl, axis=0, keepdims=True))
    lb_all = e[0:1] / jnp.sum(e, axis=0, keepdims=True)

    @pl.when(lt == 0)
    def _():
        for p in range(N_HEADS // 2):
            st_scr[p] = jnp.concatenate([s0_ref[0, 2 * p].T, s0_ref[0, 2 * p + 1].T], axis=1)

    crow = lax.broadcasted_iota(jnp.int32, (C, D_MODEL), 0)
    width = max(nsub * C, HEAD_W)
    wrow = lax.broadcasted_iota(jnp.int32, (C, width), 0)
    wcol = lax.broadcasted_iota(jnp.int32, (C, width), 1)
    base = jnp.bitwise_and(wrow, -SUB) * (C // SUB)
    keep = (wcol >= base) & (wcol <= base + wrow)
    keep2 = jnp.concatenate([keep, keep], axis=1)
    plane = lax.broadcasted_iota(jnp.int32, (HEAD_W, PAIR_W), 1)
    onorm = on_ref[...]
    nt = (((1,), (1,)), ((), ()))

    def chunk(c, carry):
        r0 = pl.multiple_of(c * C, C)
        f = f_ref[0, pl.ds(r0, C), :]
        fg = lb_all + (1.0 - lb_all) * _sigmoid(f)
        logf = jnp.log(fg)
        kk = 1.0 - fg
        b = logf
        sh = 1
        while sh < C:
            if sh % 8 == 0:
                shifted = jnp.concatenate([jnp.zeros((sh, D_MODEL), F32), b[:C - sh]], axis=0)
            else:
                shifted = jnp.where(crow >= sh, pltpu.roll(b, sh, 0), 0.0)
            b = b + shifted
            sh *= 2
        r_blocks = [jnp.zeros((SUB, D_MODEL), F32)]
        for i in range(1, nsub):
            r_blocks.append(jnp.broadcast_to(b[SUB * i - 1:SUB * i, :], (SUB, D_MODEL)))
        r = r_blocks[0] if nsub == 1 else jnp.concatenate(r_blocks, axis=0)
        b_last = b[C - 1:C, :]
        q = q_ref[0, 0, pl.ds(r0, C), :].astype(F32)
        q_inter = (q * jnp.exp(b)).astype(BF16)
        q_intra = (q * jnp.exp(b - r)).astype(BF16)
        k_state = (kk * jnp.exp(b_last - b)).astype(BF16)
        k_sub = []
        for i in range(nsub):
            n = SUB * (i + 1)
            ki = (kk[:n] * jnp.exp(r[SUB * i:SUB * i + 1, :] - b[:n])).astype(BF16)
            if n < C:
                ki = jnp.concatenate([ki, jnp.zeros((C - n, D_MODEL), BF16)], axis=0)
            k_sub.append(ki)
        st_decay = jnp.exp(b_last)
        g = g_ref[0, 0, pl.ds(r0, C), :].astype(F32)
        gate = g * _sigmoid(g)
        pairs = range(N_HEADS // 2)
        pcs = [slice(p * PAIR_W, (p + 1) * PAIR_W) for p in pairs]
        hcs = [[slice((2 * p + a) * HEAD_W, (2 * p + a + 1) * HEAD_W) for a in range(2)] for p in pairs]
        vTs = []
        for p in pairs:
            k4 = []
            vT = []
            for a in range(2):
                pieces = [k[:, hcs[p][a]] for k in k_sub]
                if nsub * C < width:
                    pieces.append(jnp.zeros((width - nsub * C, HEAD_W), BF16))
                k4.append(jnp.concatenate(pieces, axis=0))
                v = i_ref[0, 0, pl.ds(r0, C), hcs[p][a]].astype(F32)
                vrows = [v] * (HEAD_W // C) if nsub > 1 else [v, jnp.zeros((HEAD_W - C, HEAD_W), F32)]
                vT.append(jnp.concatenate(vrows, axis=0).T.astype(BF16))
            vTs.append(vT)
            sc_scr[p] = lax.dot_general(q_intra[:, pcs[p]], _block_diag(k4[0], k4[1]), nt,
                                        preferred_element_type=F32)
        for p in pairs:
            kpad = [jnp.concatenate([k_state[:, hcs[p][a]], jnp.zeros((HEAD_W - C, HEAD_W), BF16)], axis=0)
                    for a in range(2)]
            upd_scr[p] = jnp.dot(jnp.concatenate(vTs[p], axis=1), _block_diag(kpad[0], kpad[1]),
                                 preferred_element_type=F32)
        for p in pairs:
            w = jnp.where(keep2, sc_scr[p], 0.0).astype(BF16)
            stp = st_scr[p]
            stb = stp.astype(BF16)
            vfull = [jnp.concatenate([t] * (width // HEAD_W), axis=1) for t in vTs[p]]
            zw = jnp.zeros((HEAD_W, width), BF16)
            rhs_o = jnp.concatenate(
                [jnp.concatenate([vfull[0], zw, jnp.where(plane < HEAD_W, stb, 0.0)], axis=1),
                 jnp.concatenate([zw, vfull[1], jnp.where(plane >= HEAD_W, stb, 0.0)], axis=1)], axis=0)
            o_scr[p] = lax.dot_general(jnp.concatenate([w, q_inter[:, pcs[p]]], axis=1), rhs_o, nt,
                                       preferred_element_type=F32)
            st_scr[p] = stp * st_decay[:, pcs[p]] + upd_scr[p]
        for p in pairs:
            for a in range(2):
                o = o_scr[p, :, a * HEAD_W:(a + 1) * HEAD_W]
                ya_ref[0, pl.ds(r0, C), hcs[p][a]] = (_rms(o, onorm) * gate[:, hcs[p][a]]).astype(BF16)
        return carry

    lax.fori_loop(0, TL // C, chunk, 0)

    @pl.when(lt == pl.num_programs(1) - 1)
    def _():
        for p in range(N_HEADS // 2):
            stp = st_scr[p]
            for a in range(2):
                sn_ref[0, 2 * p + a] = stp[:, a * HEAD_W:(a + 1) * HEAD_W].T


def _hgrn(z4, f3, lb_logits, onorm, s0, *, L, tl):
    B = z4.shape[1]
    C = min(CHUNK, L)
    slot = lambda s: pl.BlockSpec((1, 1, tl, D_MODEL), lambda b, t: (s, b, t, 0))
    row = pl.BlockSpec((1, tl, D_MODEL), lambda b, t: (b, t, 0))
    state = pl.BlockSpec((1, N_HEADS, HEAD_W, HEAD_W), lambda b, t: (b, 0, 0, 0))
    return pl.pallas_call(
        functools.partial(_hgrn_kernel, TL=tl, C=C),
        grid=(B, L // tl),
        in_specs=[slot(Z_HQ), row, slot(Z_HI), slot(Z_HG),
                  pl.BlockSpec(lb_logits.shape, lambda b, t: (0, 0)),
                  pl.BlockSpec((1, HEAD_W), lambda b, t: (0, 0)),
                  state],
        out_specs=[row, state],
        out_shape=[jax.ShapeDtypeStruct((B, L, D_MODEL), BF16),
                   jax.ShapeDtypeStruct((B, N_HEADS, HEAD_W, HEAD_W), F32)],
        scratch_shapes=[pltpu.VMEM((N_HEADS // 2, HEAD_W, PAIR_W), F32),
                        pltpu.VMEM((N_HEADS // 2, C, 2 * max((C // SUB) * C, HEAD_W)), F32),
                        pltpu.VMEM((N_HEADS // 2, C, PAIR_W), F32),
                        pltpu.VMEM((N_HEADS // 2, HEAD_W, PAIR_W), F32)],
        compiler_params=_cparams(("arbitrary", "arbitrary")),
        name="hgrn",
    )(z4, f3, z4, z4, lb_logits, onorm, s0)


def _diff_lambda(lp_ref):
    lp = lp_ref[...]
    a = jnp.sum(lp[0:1] * lp[1:2], axis=-1, keepdims=True)
    b = jnp.sum(lp[2:3] * lp[3:4], axis=-1, keepdims=True)
    return jnp.exp(a) - jnp.exp(b) + LAMBDA_INIT


def _split_maps(q):
    lane = lax.broadcasted_iota(jnp.int32, q.shape, 1)
    return jnp.where(lane < DIFF_DH, q, 0.0), jnp.where(lane >= DIFF_DH, q, 0.0)


def _diff_attn_prompt_kernel(rb_ref, q_ref, k_ref, v_ref, bias_ref, lp_ref, sub_ref, o_ref, kb, vT, acc, m_scr, s_scr,
                             *, T, nk, G, far_bucket):
    hg = pl.program_id(1)
    qi = pl.program_id(2)
    nt = (((1,), (1,)), ((), ()))

    @pl.when(qi == 0)
    def _():
        ones = jnp.ones((ONES_ROWS, T), BF16)
        for g in range(G):
            for j in range(nk):
                kb[g, j] = (k_ref[0, g, j * T:(j + 1) * T, :] * (DIFF_SCALE * LOG2E)).astype(BF16)
                vT[g, j, :HEAD_W, :] = v_ref[0, g, j * T:(j + 1) * T, :].T.astype(BF16)
                vT[g, j, HEAD_W:, :] = ones

    qs = []
    for g in range(G):
        qa, qb = _split_maps(q_ref[0, 0, :, g * HEAD_W:(g + 1) * HEAD_W])
        qs.append((qa, qb))
    acc[...] = jnp.zeros(acc.shape, F32)
    m_scr[...] = jnp.full(m_scr.shape, NEG, F32)

    def tile(j, bias_slot):
        for g in range(G):
            kt = kb[g, j]
            for mp in range(2):
                s_scr[2 * g + mp] = lax.dot_general(kt, qs[g][mp], nt, preferred_element_type=F32)
        for g in range(G):
            vt = vT[g, j]
            if bias_slot is None:
                shift = rb_ref[far_bucket, hg * G + g] * LOG2E
            else:
                bias = bias_ref[g, bias_slot]
            for mp in range(2):
                idx = 2 * g + mp
                m = m_scr[idx]
                s = s_scr[idx]
                if bias_slot is None:
                    mn = jnp.maximum(m, jnp.max(s, axis=0, keepdims=True) + shift)
                    p = jnp.exp2(s - (mn - shift))
                else:
                    s = s + bias
                    mn = jnp.maximum(m, jnp.max(s, axis=0, keepdims=True))
                    p = jnp.exp2(s - mn)
                m_scr[idx] = mn
                acc[idx] = jnp.exp2(m - mn) * acc[idx] + jnp.dot(vt, p.astype(BF16),
                                                                  preferred_element_type=F32)

    def far(j, c):
        tile(j, None)
        return c

    lax.fori_loop(0, qi - 1, far, 0)

    @pl.when(qi >= 1)
    def _():
        tile(qi - 1, 0)

    tile(qi, 1)
    lam = _diff_lambda(lp_ref)
    for g in range(G):
        a1 = acc[2 * g]
        a2 = acc[2 * g + 1]
        o = a1[:HEAD_W] / a1[HEAD_W:HEAD_W + 1] - lam * (a2[:HEAD_W] / a2[HEAD_W:HEAD_W + 1])
        o = (o * lax.rsqrt(jnp.mean(o * o, axis=0, keepdims=True) + RMS_EPS)).T
        o_ref[0, :, g * HEAD_W:(g + 1) * HEAD_W] = (o * sub_ref[...] * (1.0 - LAMBDA_INIT)).astype(BF16)


def _diff_attn_prompt(z4, k4, v4, rel_bias, bias, lp, subln, *, T, G, far_bucket):
    B, H, L, _ = k4.shape
    nk = L // T
    W = G * HEAD_W
    kv_spec = pl.BlockSpec((1, G, L, HEAD_W), lambda b, h, qi: (b, h, 0, 0))
    return pl.pallas_call(
        functools.partial(_diff_attn_prompt_kernel, T=T, nk=nk, G=G, far_bucket=far_bucket),
        grid=(B, H // G, nk),
        in_specs=[pl.BlockSpec(memory_space=pltpu.SMEM),
                  pl.BlockSpec((1, 1, T, W), lambda b, h, qi: (Z_DQ, b, qi, h)),
                  kv_spec, kv_spec,
                  pl.BlockSpec((G, 3, T, T), lambda b, h, qi: (h, 0, 0, 0)),
                  pl.BlockSpec(lp.shape, lambda b, h, qi: (0, 0)),
                  pl.BlockSpec((1, HEAD_W), lambda b, h, qi: (0, 0))],
        out_specs=pl.BlockSpec((1, T, W), lambda b, h, qi: (b, qi, h)),
        out_shape=jax.ShapeDtypeStruct((B, L, D_MODEL), BF16),
        scratch_shapes=[pltpu.VMEM((G, nk, T, HEAD_W), BF16),
                        pltpu.VMEM((G, nk, HEAD_W + ONES_ROWS, T), BF16),
                        pltpu.VMEM((2 * G, HEAD_W + ONES_ROWS, T), F32),
                        pltpu.VMEM((2 * G, 1, T), F32),
                        pltpu.VMEM((2 * G, T, T), F32)],
        compiler_params=_cparams(("arbitrary", "arbitrary", "arbitrary")),
        name="diff_attn_prompt",
    )(rel_bias, z4, k4, v4, bias, lp, subln)


def _diff_attn_sample_kernel(q_ref, kn_ref, vn_ref, kc_ref, vc_ref, bp_ref, bn_ref, lp_ref, sub_ref, o_ref,
                             *, Ls, hb):
    lam = _diff_lambda(lp_ref)
    nt = (((1,), (1,)), ((), ()))
    for hh in range(hb):
        cols = slice(hh * HEAD_W, (hh + 1) * HEAD_W)
        qa, qb = _split_maps(q_ref[0, 0, :, cols].astype(F32) * DIFF_SCALE)
        q2 = jnp.concatenate([qa, qb], axis=0).astype(BF16)
        kp = kc_ref[0, hh].astype(BF16)
        vp = vc_ref[0, hh].astype(BF16)
        kn = kn_ref[0, :, cols].astype(BF16)
        vn = vn_ref[0, :, cols].astype(BF16)
        s = lax.dot_general(q2, kp, nt, preferred_element_type=F32) + bp_ref[hh]
        sn = lax.dot_general(q2, kn, nt, preferred_element_type=F32) + bn_ref[hh]
        m = jnp.maximum(jnp.max(s, axis=-1, keepdims=True), jnp.max(sn, axis=-1, keepdims=True))
        p = jnp.exp(s - m)
        pn = jnp.exp(sn - m)
        l = jnp.sum(p, axis=-1, keepdims=True) + jnp.sum(pn, axis=-1, keepdims=True)
        o2 = (jnp.dot(p.astype(BF16), vp, preferred_element_type=F32)
              + jnp.dot(pn.astype(BF16), vn, preferred_element_type=F32)) / l
        o = o2[:Ls] - lam * o2[Ls:]
        o_ref[0, :, cols] = (_rms(o, sub_ref[...]) * (1.0 - LAMBDA_INIT)).astype(BF16)


def _diff_attn_sample(z4, kn3, vn3, kc, vc, bias_p, bias_n, lp, subln, *, hb):
    B, H, P, _ = kc.shape
    Ls = kn3.shape[1]
    W = hb * HEAD_W
    cache_spec = pl.BlockSpec((1, hb, P, HEAD_W), lambda b, g: (b, g, 0, 0))
    new_spec = pl.BlockSpec((1, Ls, W), lambda b, g: (b, 0, g))
    return pl.pallas_call(
        functools.partial(_diff_attn_sample_kernel, Ls=Ls, hb=hb),
        grid=(B, H // hb),
        in_specs=[pl.BlockSpec((1, 1, Ls, W), lambda b, g: (Z_DQ, b, 0, g)),
                  new_spec, new_spec, cache_spec, cache_spec,
                  pl.BlockSpec((hb, 2 * Ls, P), lambda b, g: (g, 0, 0)),
                  pl.BlockSpec((hb, 2 * Ls, Ls), lambda b, g: (g, 0, 0)),
                  pl.BlockSpec(lp.shape, lambda b, g: (0, 0)),
                  pl.BlockSpec((1, HEAD_W), lambda b, g: (0, 0))],
        out_specs=pl.BlockSpec((1, Ls, W), lambda b, g: (b, 0, g)),
        out_shape=jax.ShapeDtypeStruct((B, Ls, D_MODEL), BF16),
        compiler_params=_cparams(("arbitrary", "arbitrary")),
        name="diff_attn_sample",
    )(z4, kn3, vn3, kc, vc, bias_p, bias_n, lp, subln)


def _mem_kv_kernel(m_ref, g_ref, w_ref, k_ref, v_ref):
    h = _rms(m_ref[0], g_ref[...]).astype(BF16)
    kv = jnp.dot(h, w_ref[...], preferred_element_type=F32)
    for hh in range(MEM_HEADS):
        k_ref[0, hh] = kv[:, hh * MEM_DH:(hh + 1) * MEM_DH]
        v_ref[0, hh] = kv[:, D_MODEL + hh * MEM_DH:D_MODEL + (hh + 1) * MEM_DH]


def _mem_kv(mem, g, w_bf):
    B, M, _ = mem.shape
    out = jax.ShapeDtypeStruct((B, MEM_HEADS, M, MEM_DH), F32)
    spec = pl.BlockSpec((1, MEM_HEADS, M, MEM_DH), lambda b: (b, 0, 0, 0))
    return pl.pallas_call(
        _mem_kv_kernel,
        grid=(B,),
        in_specs=[pl.BlockSpec((1, M, D_MODEL), lambda b: (b, 0, 0)),
                  pl.BlockSpec((1, D_MODEL), lambda b: (0, 0)),
                  pl.BlockSpec(w_bf.shape, lambda b: (0, 0))],
        out_specs=[spec, spec],
        out_shape=[out, out],
        compiler_params=_cparams(("arbitrary",)),
        name="mem_kv",
    )(mem, g, w_bf)


def _mem_attn_kernel(q_ref, k_ref, v_ref, o_ref):
    nt = (((1,), (1,)), ((), ()))
    for hh in range(MEM_HEADS):
        cols = slice(hh * MEM_DH, (hh + 1) * MEM_DH)
        q = q_ref[0, 0, :, cols]
        s = lax.dot_general(q, k_ref[0, hh].astype(BF16), nt, preferred_element_type=F32) * (MEM_DH ** -0.5)
        p = jnp.exp(s - jnp.max(s, axis=-1, keepdims=True))
        p = p / jnp.sum(p, axis=-1, keepdims=True)
        o = jnp.dot(p.astype(BF16), v_ref[0, hh].astype(BF16), preferred_element_type=F32)
        o_ref[0, :, cols] = o.astype(BF16)


def _mem_attn(z4, mk, mv, *, tl):
    _, B, L, _ = z4.shape
    M = mk.shape[2]
    kv_spec = pl.BlockSpec((1, MEM_HEADS, M, MEM_DH), lambda b, t: (b, 0, 0, 0))
    return pl.pallas_call(
        _mem_attn_kernel,
        grid=(B, L // tl),
        in_specs=[pl.BlockSpec((1, 1, tl, D_MODEL), lambda b, t: (Z_MQ, b, t, 0)), kv_spec, kv_spec],
        out_specs=pl.BlockSpec((1, tl, D_MODEL), lambda b, t: (b, t, 0)),
        out_shape=jax.ShapeDtypeStruct((B, L, D_MODEL), BF16),
        compiler_params=_cparams(("arbitrary", "arbitrary")),
        name="mem_attn",
    )(z4, mk, mv)


def _merge_kernel(x_ref, ya_ref, yb_ref, yc_ref, g0_ref, g1_ref, g2_ref, wb_ref, wo_ref, nf_ref,
                  x1_ref, h2_ref):
    merged = None
    for n, (y_ref, gate_ref) in enumerate(((ya_ref, g0_ref), (yb_ref, g1_ref), (yc_ref, g2_ref))):
        proj = jnp.dot(y_ref[...], wb_ref[n], preferred_element_type=F32)
        term = proj * _sigmoid(gate_ref[0].astype(F32))
        merged = term if merged is None else merged + term
    x1 = x_ref[...] + jnp.dot(merged.astype(BF16), wo_ref[...], preferred_element_type=F32)
    x1_ref[...] = x1
    h2_ref[...] = _rms(x1, nf_ref[...]).astype(BF16)


def _merge(x2d, ya, yb, yc, z3, wb_bf, wo_bf, nf, *, tm):
    N = x2d.shape[0]
    row = pl.BlockSpec((tm, D_MODEL), lambda i: (i, 0))
    gate = lambda n: pl.BlockSpec((1, tm, D_MODEL), lambda i: (Z_G0 + n, i, 0))
    return pl.pallas_call(
        _merge_kernel,
        grid=(N // tm,),
        in_specs=[row, row, row, row, gate(0), gate(1), gate(2),
                  pl.BlockSpec(wb_bf.shape, lambda i: (0, 0, 0)),
                  pl.BlockSpec(wo_bf.shape, lambda i: (0, 0)),
                  pl.BlockSpec((1, D_MODEL), lambda i: (0, 0))],
        out_specs=[row, row],
        out_shape=[jax.ShapeDtypeStruct((N, D_MODEL), F32), jax.ShapeDtypeStruct((N, D_MODEL), BF16)],
        compiler_params=_cparams(("arbitrary",)),
        name="merge",
    )(x2d, ya, yb, yc, z3, z3, z3, wb_bf, wo_bf, nf)


def _ffn_kernel(h_ref, x_ref, cp_ref, wu_ref, cw_ref, cb_ref, wd_ref, nf_ref, y_ref, cn_ref, carry, u_scr, act_scr,
                *, tm, cw, d_ff, nb):
    lt = pl.program_id(1)
    rows_per = tm // nb

    if nb == 1:
        @pl.when(lt == 0)
        def _():
            carry[...] = cp_ref[0]

    h = h_ref[0]
    edge = 8
    erow = lax.broadcasted_iota(jnp.int32, (edge, cw), 0)
    for jc in range(d_ff // cw):
        slot = jc % 2
        bases = (jc * cw, d_ff + jc * cw)
        for hf, base in enumerate(bases):
            u_scr[slot, hf] = jnp.dot(h, wu_ref[:, base:base + cw], preferred_element_type=F32)
        halves = []
        for hf, base in enumerate(bases):
            cols = slice(base, base + cw)
            u = u_scr[slot, hf]
            w0, w1, w2, bb = cw_ref[0:1, cols], cw_ref[1:2, cols], cw_ref[2:3, cols], cb_ref[:, cols]
            u1 = pltpu.roll(u, 1, 0)
            u2 = pltpu.roll(u, 2, 0)
            c = bb + w0 * u2 + w1 * u1 + w2 * u
            pieces = []
            for s in range(nb):
                r0 = s * rows_per
                prev = carry if nb == 1 else cp_ref.at[s]
                p2 = prev[0:1, cols]
                p1 = prev[1:2, cols]
                u1e = jnp.where(erow == 0, p1, u1[r0:r0 + edge])
                u2e = jnp.where(erow == 0, p2, jnp.where(erow == 1, p1, u2[r0:r0 + edge]))
                pieces += [bb + w0 * u2e + w1 * u1e + w2 * u[r0:r0 + edge], c[r0 + edge:r0 + rows_per]]
                tail = u[r0 + rows_per - (FFN_CONV - 1):r0 + rows_per, :]
                if nb == 1:
                    carry[:, cols] = tail
                cn_ref[s, :, cols] = tail
            halves.append(jnp.concatenate(pieces, axis=0))
        gate, val = halves
        act_scr[:, jc * cw:(jc + 1) * cw] = (gate * _sigmoid(gate) * val).astype(BF16)
    y = x_ref[0] + jnp.dot(act_scr[...], wd_ref[...], preferred_element_type=F32)
    y_ref[0] = _rms(y, nf_ref[...])


def _ffn(h2, x1, conv_prev, wu_bf, conv_w, conv_b, wd_bf, nf, *, tm, nb=1, cw=256):
    G, R, _ = x1.shape
    d_ff = wd_bf.shape[0]
    assert nb == 1 or tm == R
    row = pl.BlockSpec((1, tm, D_MODEL), lambda b, t: (b, t, 0))
    state = pl.BlockSpec((nb, FFN_CONV - 1, 2 * d_ff), lambda b, t: (b, 0, 0))
    const = lambda a: pl.BlockSpec(a.shape, lambda b, t: (0,) * a.ndim)
    return pl.pallas_call(
        functools.partial(_ffn_kernel, tm=tm, cw=cw, d_ff=d_ff, nb=nb),
        grid=(G, R // tm),
        in_specs=[row, row, state, const(wu_bf), const(conv_w), const(conv_b), const(wd_bf), const(nf)],
        out_specs=[row, state],
        out_shape=[jax.ShapeDtypeStruct((G, R, D_MODEL), F32),
                   jax.ShapeDtypeStruct((G * nb, FFN_CONV - 1, 2 * d_ff), F32)],
        scratch_shapes=[pltpu.VMEM((FFN_CONV - 1, 2 * d_ff), F32),
                        pltpu.VMEM((2, 2, tm, cw), F32),
                        pltpu.VMEM((tm, d_ff), BF16)],
        compiler_params=_cparams(("arbitrary", "arbitrary")),
        name="ffn",
    )(h2, x1, conv_prev, wu_bf, conv_w, conv_b, wd_bf, nf)


def _layer(x, W, *, hgrn_s0, conv_prev, mem_k, mem_v, attn, tm_proj, tm_merge, tm_ffn, tl_mem, head_layout):
    B, L, _ = x.shape
    N = B * L
    x2d = x.reshape(N, D_MODEL)
    z3, f2d, k_new, v_new = _in_proj(x2d, W["norm_attn"], W["w_in"], tm=tm_proj, rows_per_batch=L,
                                     head_layout=head_layout)
    z4 = z3.reshape(N_ZSLOT, B, L, D_MODEL)
    ya, s_new = _hgrn(z4, f2d.reshape(B, L, D_MODEL), W["lb_logits"], W["hgrn_onorm"], hgrn_s0, L=L,
                      tl=min(L, 512))
    yb, k_new, v_new = attn(z4, k_new, v_new)
    yc = _mem_attn(z4, mem_k, mem_v, tl=tl_mem)
    x1, h2 = _merge(x2d, ya.reshape(N, D_MODEL), yb.reshape(N, D_MODEL), yc.reshape(N, D_MODEL), z3,
                    W["w_branch"], W["w_out"], W["norm_ffn"], tm=tm_merge)
    nb = B if L < tm_ffn else 1
    groups = (B // nb, nb * L, D_MODEL)
    y, conv_new = _ffn(h2.reshape(groups), x1.reshape(groups), conv_prev, W["w_up"], W["conv_w"], W["conv_b"],
                       W["w_down"], W["norm_final"], tm=nb * L if nb > 1 else tm_ffn, nb=nb)
    return y.reshape(B, L, D_MODEL), k_new, v_new, s_new, conv_new


def kernel(x_prompt, x_sample, mem_prompt, cache_diff_k, cache_diff_v, cache_mem_k, cache_mem_v, state_hgrn, state_ffn_conv, rel_bias, hgrn_lb_logits, norm_attn, w_in, hgrn_onorm, diff_lambda, diff_subln, mem_norm, w_mem_kv, w_branch, w_out, norm_ffn, w_up, conv_w, conv_b, w_down, norm_final):
    assert w_in.shape[0] == 1, "single-layer trunk"
    Bp, Lp, _ = x_prompt.shape
    Bs, Ls, _ = x_sample.shape
    P = cache_diff_k.shape[3]
    d_ff2 = w_up.shape[2]
    row = lambda a: a.reshape(1, -1)
    W = dict(norm_attn=row(norm_attn[0]), w_in=w_in[0].astype(BF16), lb_logits=hgrn_lb_logits,
             hgrn_onorm=row(hgrn_onorm[0]), w_branch=w_branch[0].astype(BF16), w_out=w_out[0].astype(BF16),
             norm_ffn=row(norm_ffn[0]), w_up=w_up[0].astype(BF16), conv_w=conv_w[0], conv_b=row(conv_b[0]),
             w_down=w_down[0].astype(BF16), norm_final=row(norm_final))
    lp = diff_lambda[0]
    subln = row(diff_subln[0])

    T = 512
    bias_p, far_bucket = _bias_prompt(rel_bias, T)
    mk, mv = _mem_kv(mem_prompt, row(mem_norm[0]), w_mem_kv[0].astype(BF16))

    def attn_prompt(z4, k4, v4):
        yb = _diff_attn_prompt(z4, k4, v4, rel_bias, bias_p, lp, subln, T=T, G=2, far_bucket=far_bucket)
        return yb, k4, v4

    yp, pk, pv, ps, pc = _layer(
        x_prompt, W, hgrn_s0=jnp.zeros((Bp, N_HEADS, HEAD_W, HEAD_W), F32),
        conv_prev=jnp.zeros((Bp, FFN_CONV - 1, d_ff2), F32), mem_k=mk, mem_v=mv, attn=attn_prompt,
        tm_proj=1024, tm_merge=512, tm_ffn=512, tl_mem=512, head_layout=True)

    bias_sp, bias_sn = _bias_sample(rel_bias, Ls, P)

    def attn_sample(z4, k2d, v2d):
        yb = _diff_attn_sample(z4, k2d.reshape(Bs, Ls, D_MODEL), v2d.reshape(Bs, Ls, D_MODEL),
                               cache_diff_k[0], cache_diff_v[0], bias_sp, bias_sn, lp, subln, hb=8)
        heads = lambda a: jnp.transpose(a.reshape(Bs, Ls, N_HEADS, HEAD_W), (0, 2, 1, 3))
        return yb, heads(k2d), heads(v2d)

    ys, sk, sv, ss, sc = _layer(
        x_sample, W, hgrn_s0=state_hgrn[0], conv_prev=state_ffn_conv[0], mem_k=cache_mem_k[0],
        mem_v=cache_mem_v[0], attn=attn_sample,
        tm_proj=Bs * Ls, tm_merge=Bs * Ls, tm_ffn=512, tl_mem=Ls, head_layout=False)

    return (yp, ys, pk[None], pv[None], ps[None], pc[None], mk[None], mv[None],
            sk[None], sv[None], ss[None], sc[None])
```

```python
import functools
import math

import numpy as np
import jax
import jax.numpy as jnp
from jax import lax
from jax.experimental import pallas as pl
from jax.experimental.pallas import tpu as pltpu

F32 = jnp.float32
BF16 = jnp.bfloat16

D_MODEL = 1024
CHUNK = 64
SUB = 16
HEAD_W = 128
N_HEADS = D_MODEL // HEAD_W
PAIR_W = 2 * HEAD_W
DIFF_DH = 64
DIFF_SCALE = DIFF_DH ** -0.5
MEM_HEADS = 4
MEM_DH = D_MODEL // MEM_HEADS
REL_BUCKETS = 32
REL_MAX_DIST = 128
N_BRANCH = 3
FFN_CONV = 3
RMS_EPS = 1e-6
LAMBDA_INIT = 0.8 - 0.6 * math.exp(-0.3 * 0)
NEG = -1e30
LOG2E = math.log2(math.e)
ONES_ROWS = 16

SEC_HQ, SEC_HF, SEC_HI, SEC_HG, SEC_DQ, SEC_DK, SEC_DV, SEC_MQ, SEC_G0 = range(9)
N_SEC = 11
Z_HQ, Z_HI, Z_HG, Z_DQ, Z_MQ, Z_G0 = 0, 1, 2, 3, 4, 5
N_ZSLOT = 8

V7X_VMEM_LIMIT = 56 * 1024 * 1024


def _cparams(sem, vmem=V7X_VMEM_LIMIT):
    return pltpu.CompilerParams(dimension_semantics=sem, vmem_limit_bytes=vmem)


def _sigmoid(x):
    return 0.5 * jnp.tanh(0.5 * x) + 0.5


def _rms(x, g):
    ms = jnp.mean(x * x, axis=-1, keepdims=True)
    return x * lax.rsqrt(ms + RMS_EPS) * g


def _np_bucket(rel):
    nb = REL_BUCKETS // 2
    ret = np.where(rel > 0, nb, 0)
    n = np.abs(rel)
    max_exact = nb // 2
    large = max_exact + (np.log(np.maximum(n, 1).astype(np.float32) / max_exact)
                         / math.log(REL_MAX_DIST / max_exact) * (nb - max_exact)).astype(np.int32)
    large = np.minimum(large, nb - 1)
    return ret + np.where(n < max_exact, n, large)


def _bucket_segments(lo, hi):
    rel = np.arange(lo, hi + 1, dtype=np.int32)
    b = _np_bucket(rel)
    change = np.nonzero(np.diff(b))[0]
    return int(b[0]), [(int(rel[i + 1]), int(b[i + 1])) for i in change]


def _bias_from_rel(rel, rb_ref, h, first_bucket, segs):
    val = jnp.full(rel.shape, rb_ref[first_bucket, h], F32)
    for lo, bk in segs:
        val = jnp.where(rel >= lo, rb_ref[bk, h], val)
    return val


def _bias_prompt_kernel(rb_ref, o_ref, *, T, first_bucket, segs):
    h = pl.program_id(0)
    kk = lax.broadcasted_iota(jnp.int32, (T, T), 0)
    qq = lax.broadcasted_iota(jnp.int32, (T, T), 1)
    o_ref[0, 0] = _bias_from_rel(kk - qq - T, rb_ref, h, first_bucket, segs) * LOG2E
    diag = _bias_from_rel(kk - qq, rb_ref, h, first_bucket, segs) * LOG2E
    o_ref[0, 1] = jnp.where(jnp.bitwise_and(kk, -CHUNK) <= qq, diag, NEG)
    o_ref[0, 2] = jnp.full((T, T), NEG, F32)


def _bias_prompt(rel_bias, T):
    H = rel_bias.shape[1]
    first_bucket, segs = _bucket_segments(-2 * T, T)
    assert all(lo > -T for lo, _ in segs), "bias must be constant beyond one tile"
    tiles = pl.pallas_call(
        functools.partial(_bias_prompt_kernel, T=T, first_bucket=first_bucket, segs=segs),
        grid=(H,),
        in_specs=[pl.BlockSpec(memory_space=pltpu.SMEM)],
        out_specs=pl.BlockSpec((1, 3, T, T), lambda h: (h, 0, 0, 0)),
        out_shape=jax.ShapeDtypeStruct((H, 3, T, T), F32),
        compiler_params=_cparams(("arbitrary",)),
        name="bias_prompt",
    )(rel_bias)
    return tiles, first_bucket


def _bias_sample_kernel(rb_ref, bp_ref, bn_ref, *, Ls, P, first_bucket, segs):
    h = pl.program_id(0)
    qq = jnp.bitwise_and(lax.broadcasted_iota(jnp.int32, (2 * Ls, P), 0), Ls - 1)
    kk = lax.broadcasted_iota(jnp.int32, (2 * Ls, P), 1)
    bp_ref[0] = _bias_from_rel(kk - P - qq, rb_ref, h, first_bucket, segs)
    qn = jnp.bitwise_and(lax.broadcasted_iota(jnp.int32, (2 * Ls, Ls), 0), Ls - 1)
    kn = lax.broadcasted_iota(jnp.int32, (2 * Ls, Ls), 1)
    bn_ref[0] = _bias_from_rel(kn - qn, rb_ref, h, first_bucket, segs)


def _bias_sample(rel_bias, Ls, P):
    H = rel_bias.shape[1]
    first_bucket, segs = _bucket_segments(-(P + Ls), Ls)
    return pl.pallas_call(
        functools.partial(_bias_sample_kernel, Ls=Ls, P=P, first_bucket=first_bucket, segs=segs),
        grid=(H,),
        in_specs=[pl.BlockSpec(memory_space=pltpu.SMEM)],
        out_specs=[pl.BlockSpec((1, 2 * Ls, P), lambda h: (h, 0, 0)),
                   pl.BlockSpec((1, 2 * Ls, Ls), lambda h: (h, 0, 0))],
        out_shape=[jax.ShapeDtypeStruct((H, 2 * Ls, P), F32),
                   jax.ShapeDtypeStruct((H, 2 * Ls, Ls), F32)],
        compiler_params=_cparams(("arbitrary",)),
        name="bias_sample",
    )(rel_bias)


def _in_proj_kernel(x_ref, g_ref, w_ref, z_ref, f_ref, k_ref, v_ref, h_scr, *, head_layout):
    sec = pl.program_id(1)

    @pl.when(sec == 0)
    def _():
        h_scr[...] = _rms(x_ref[...], g_ref[...]).astype(BF16)

    def section():
        return jnp.dot(h_scr[...], w_ref[...], preferred_element_type=F32)

    def store_heads(ref, acc):
        if head_layout:
            for hh in range(N_HEADS):
                ref[0, hh] = acc[:, hh * HEAD_W:(hh + 1) * HEAD_W]
        else:
            ref[...] = acc

    @pl.when(sec == SEC_HF)
    def _():
        f_ref[...] = section()

    @pl.when(sec == SEC_DK)
    def _():
        store_heads(k_ref, section())

    @pl.when(sec == SEC_DV)
    def _():
        store_heads(v_ref, section())

    @pl.when((sec != SEC_HF) & (sec != SEC_DK) & (sec != SEC_DV))
    def _():
        z_ref[0] = section().astype(BF16)


def _zslot(sec):
    return (sec - (sec >= SEC_HF).astype(jnp.int32) - (sec >= SEC_DK).astype(jnp.int32)
            - (sec >= SEC_DV).astype(jnp.int32))


def _in_proj(x2d, g, w_bf, *, tm, rows_per_batch, head_layout):
    N = x2d.shape[0]
    nt = N // tm
    if head_layout:
        B = N // rows_per_batch
        nlt = rows_per_batch // tm
        kv_shape = jax.ShapeDtypeStruct((B, N_HEADS, rows_per_batch, HEAD_W), F32)
        kv_spec = pl.BlockSpec((1, N_HEADS, tm, HEAD_W), lambda i, j: (i // nlt, 0, i % nlt, 0))
    else:
        kv_shape = jax.ShapeDtypeStruct((N, D_MODEL), F32)
        kv_spec = pl.BlockSpec((tm, D_MODEL), lambda i, j: (i, 0))
    return pl.pallas_call(
        functools.partial(_in_proj_kernel, head_layout=head_layout),
        grid=(nt, N_SEC),
        in_specs=[pl.BlockSpec((tm, D_MODEL), lambda i, j: (i, 0)),
                  pl.BlockSpec((1, D_MODEL), lambda i, j: (0, 0)),
                  pl.BlockSpec((D_MODEL, D_MODEL), lambda i, j: (0, j))],
        out_specs=[pl.BlockSpec((1, tm, D_MODEL), lambda i, j: (_zslot(j), i, 0)),
                   pl.BlockSpec((tm, D_MODEL), lambda i, j: (i, 0)),
                   kv_spec, kv_spec],
        out_shape=[jax.ShapeDtypeStruct((N_ZSLOT, N, D_MODEL), BF16),
                   jax.ShapeDtypeStruct((N, D_MODEL), F32),
                   kv_shape, kv_shape],
        scratch_shapes=[pltpu.VMEM((tm, D_MODEL), BF16)],
        compiler_params=_cparams(("arbitrary", "arbitrary")),
        name="in_proj",
    )(x2d, g, w_bf)


def _block_diag(a, b):
    za = jnp.zeros((a.shape[0], b.shape[1]), a.dtype)
    zb = jnp.zeros((b.shape[0], a.shape[1]), b.dtype)
    return jnp.concatenate([jnp.concatenate([a, za], axis=1), jnp.concatenate([zb, b], axis=1)], axis=0)


def _hgrn_kernel(q_ref, f_ref, i_ref, g_ref, lbl_ref, on_ref, s0_ref, ya_ref, sn_ref,
                 st_scr, sc_scr, o_scr, upd_scr, *, TL, C):
    lt = pl.program_id(1)
    nsub = C // SUB
    lbl = lbl_ref[...]
    e = jnp.exp(lbl - jnp.max(lbl, axis=0, keepdims=True))
    lb_all = e[0:1] / jnp.sum(e, axis=0, keepdims=True)

    @pl.when(lt == 0)
    def _():
        st_scr[...] = s0_ref[0]

    crow = lax.broadcasted_iota(jnp.int32, (C, D_MODEL), 0)
    width = max(nsub * C, HEAD_W)
    wrow = lax.broadcasted_iota(jnp.int32, (C, width), 0)
    wcol = lax.broadcasted_iota(jnp.int32, (C, width), 1)
    base = jnp.bitwise_and(wrow, -SUB) * (C // SUB)
    keep = (wcol >= base) & (wcol <= base + wrow)
    keep2 = jnp.concatenate([keep, keep], axis=1)
    onorm = on_ref[...]
    nt = (((1,), (1,)), ((), ()))

    def chunk(c, carry):
        r0 = pl.multiple_of(c * C, C)
        f = f_ref[0, pl.ds(r0, C), :]
        fg = lb_all + (1.0 - lb_all) * _sigmoid(f)
        logf = jnp.log(fg)
        kk = 1.0 - fg
        b = logf
        sh = 1
        while sh < C:
            if sh % 8 == 0:
                shifted = jnp.concatenate([jnp.zeros((sh, D_MODEL), F32), b[:C - sh]], axis=0)
            else:
                shifted = jnp.where(crow >= sh, pltpu.roll(b, sh, 0), 0.0)
            b = b + shifted
            sh *= 2
        r_blocks = [jnp.zeros((SUB, D_MODEL), F32)]
        for i in range(1, nsub):
            r_blocks.append(jnp.broadcast_to(b[SUB * i - 1:SUB * i, :], (SUB, D_MODEL)))
        r = r_blocks[0] if nsub == 1 else jnp.concatenate(r_blocks, axis=0)
        b_last = b[C - 1:C, :]
        q = q_ref[0, 0, pl.ds(r0, C), :].astype(F32)
        q_inter = (q * jnp.exp(b)).astype(BF16)
        q_intra = (q * jnp.exp(b - r)).astype(BF16)
        k_state = kk * jnp.exp(b_last - b)
        k_sub = []
        for i in range(nsub):
            n = SUB * (i + 1)
            ki = (kk[:n] * jnp.exp(r[SUB * i:SUB * i + 1, :] - b[:n])).astype(BF16)
            if n < C:
                ki = jnp.concatenate([ki, jnp.zeros((C - n, D_MODEL), BF16)], axis=0)
            k_sub.append(ki)
        st_decay = jnp.exp(b_last)
        g = g_ref[0, 0, pl.ds(r0, C), :].astype(F32)
        gate = g * _sigmoid(g)
        pairs = range(N_HEADS // 2)
        pcs = [slice(p * PAIR_W, (p + 1) * PAIR_W) for p in pairs]
        hcs = [[slice((2 * p + a) * HEAD_W, (2 * p + a + 1) * HEAD_W) for a in range(2)] for p in pairs]
        zrows = jnp.zeros((HEAD_W - C, HEAD_W), BF16)
        vpads = []
        for p in pairs:
            k4 = []
            for a in range(2):
                pieces = [k[:, hcs[p][a]] for k in k_sub]
                if nsub * C < width:
                    pieces.append(jnp.zeros((width - nsub * C, HEAD_W), BF16))
                k4.append(jnp.concatenate(pieces, axis=0))
            vpads.append([jnp.concatenate([i_ref[0, 0, pl.ds(r0, C), hcs[p][a]], zrows], axis=0)
                          for a in range(2)])
            sc_scr[p] = lax.dot_general(q_intra[:, pcs[p]], _block_diag(k4[0], k4[1]), nt,
                                        preferred_element_type=F32)
        dcols = []
        for p in pairs:
            kT = []
            for a in range(2):
                kext = jnp.concatenate([k_state[:, hcs[p][a]],
                                        jnp.broadcast_to(st_decay[:, hcs[p][a]], (8, HEAD_W)),
                                        jnp.zeros((HEAD_W - C - 8, HEAD_W), F32)], axis=0).T
                dcols.append(kext[:, C:C + 1])
                kT.append(kext.astype(BF16))
            upd_scr[p] = jnp.dot(jnp.concatenate(kT, axis=1), _block_diag(vpads[p][0], vpads[p][1]),
                                 preferred_element_type=F32)
        for p in pairs:
            w = jnp.where(keep2, sc_scr[p], 0.0)
            lhs = []
            rhs = []
            for a in range(2):
                t = w[:, a * width:a * width + HEAD_W]
                for i in range(1, width // HEAD_W):
                    t = t + w[:, a * width + i * HEAD_W:a * width + (i + 1) * HEAD_W]
                if nsub > 1 and C < HEAD_W:
                    assert HEAD_W == 2 * C
                    t = t + pltpu.roll(t, C, 1)
                lhs += [t.astype(BF16), q_inter[:, hcs[p][a]]]
                rhs.append(jnp.concatenate([vpads[p][a], st_scr[2 * p + a].astype(BF16)], axis=0))
            o_scr[p] = jnp.dot(jnp.concatenate(lhs, axis=1), _block_diag(rhs[0], rhs[1]),
                               preferred_element_type=F32)
        for p in pairs:
            for a in range(2):
                hh = 2 * p + a
                st_scr[hh] = st_scr[hh] * dcols[hh] + upd_scr[p, :, a * HEAD_W:(a + 1) * HEAD_W]
        for p in pairs:
            for a in range(2):
                o = o_scr[p, :, a * HEAD_W:(a + 1) * HEAD_W]
                ya_ref[0, pl.ds(r0, C), hcs[p][a]] = (_rms(o, onorm) * gate[:, hcs[p][a]]).astype(BF16)
        return carry

    lax.fori_loop(0, TL // C, chunk, 0)

    @pl.when(lt == pl.num_programs(1) - 1)
    def _():
        sn_ref[0] = st_scr[...]


def _hgrn(z4, f3, lb_logits, onorm, s0, *, L, tl):
    B = z4.shape[1]
    C = min(CHUNK, L)
    slot = lambda s: pl.BlockSpec((1, 1, tl, D_MODEL), lambda b, t: (s, b, t, 0))
    row = pl.BlockSpec((1, tl, D_MODEL), lambda b, t: (b, t, 0))
    state = pl.BlockSpec((1, N_HEADS, HEAD_W, HEAD_W), lambda b, t: (b, 0, 0, 0))
    return pl.pallas_call(
        functools.partial(_hgrn_kernel, TL=tl, C=C),
        grid=(B, L // tl),
        in_specs=[slot(Z_HQ), row, slot(Z_HI), slot(Z_HG),
                  pl.BlockSpec(lb_logits.shape, lambda b, t: (0, 0)),
                  pl.BlockSpec((1, HEAD_W), lambda b, t: (0, 0)),
                  state],
        out_specs=[row, state],
        out_shape=[jax.ShapeDtypeStruct((B, L, D_MODEL), BF16),
                   jax.ShapeDtypeStruct((B, N_HEADS, HEAD_W, HEAD_W), F32)],
        scratch_shapes=[pltpu.VMEM((N_HEADS, HEAD_W, HEAD_W), F32),
                        pltpu.VMEM((N_HEADS // 2, C, 2 * max((C // SUB) * C, HEAD_W)), F32),
                        pltpu.VMEM((N_HEADS // 2, C, PAIR_W), F32),
                        pltpu.VMEM((N_HEADS // 2, HEAD_W, PAIR_W), F32)],
        compiler_params=_cparams(("arbitrary", "arbitrary")),
        name="hgrn",
    )(z4, f3, z4, z4, lb_logits, onorm, s0)


def _diff_lambda(lp_ref):
    lp = lp_ref[...]
    a = jnp.sum(lp[0:1] * lp[1:2], axis=-1, keepdims=True)
    b = jnp.sum(lp[2:3] * lp[3:4], axis=-1, keepdims=True)
    return jnp.exp(a) - jnp.exp(b) + LAMBDA_INIT


def _split_maps(q):
    lane = lax.broadcasted_iota(jnp.int32, q.shape, 1)
    return jnp.where(lane < DIFF_DH, q, 0.0), jnp.where(lane >= DIFF_DH, q, 0.0)


def _diff_attn_prompt_kernel(rb_ref, q_ref, k_ref, v_ref, bias_ref, lp_ref, sub_ref, o_ref, kb, vT, acc, m_scr, s_scr,
                             *, T, nk, G, far_bucket):
    hg = pl.program_id(1)
    qi = pl.program_id(2)
    nt = (((1,), (1,)), ((), ()))

    @pl.when(qi == 0)
    def _():
        ones = jnp.ones((ONES_ROWS, T), BF16)
        for g in range(G):
            for j in range(nk):
                kb[g, j] = (k_ref[0, g, j * T:(j + 1) * T, :] * (DIFF_SCALE * LOG2E)).astype(BF16)
                vT[g, j, :HEAD_W, :] = v_ref[0, g, j * T:(j + 1) * T, :].T.astype(BF16)
                vT[g, j, HEAD_W:, :] = ones

    qs = []
    for g in range(G):
        qa, qb = _split_maps(q_ref[0, 0, :, g * HEAD_W:(g + 1) * HEAD_W])
        qs.append((qa, qb))
    acc[...] = jnp.zeros(acc.shape, F32)
    m_scr[...] = jnp.full(m_scr.shape, NEG, F32)

    def tile(j, bias_slot):
        for g in range(G):
            kt = kb[g, j]
            for mp in range(2):
                s_scr[2 * g + mp] = lax.dot_general(kt, qs[g][mp], nt, preferred_element_type=F32)
        for g in range(G):
            vt = vT[g, j]
            if bias_slot is None:
                shift = rb_ref[far_bucket, hg * G + g] * LOG2E
            else:
                bias = bias_ref[g, bias_slot]
            for mp in range(2):
                idx = 2 * g + mp
                m = m_scr[idx]
                s = s_scr[idx]
                if bias_slot is None:
                    mn = jnp.maximum(m, jnp.max(s, axis=0, keepdims=True) + shift)
                    p = jnp.exp2(s - (mn - shift))
                else:
                    s = s + bias
                    mn = jnp.maximum(m, jnp.max(s, axis=0, keepdims=True))
                    p = jnp.exp2(s - mn)
                m_scr[idx] = mn
                acc[idx] = jnp.exp2(m - mn) * acc[idx] + jnp.dot(vt, p.astype(BF16),
                                                                  preferred_element_type=F32)

    def far(j, c):
        tile(j, None)
        return c

    lax.fori_loop(0, qi - 1, far, 0)

    @pl.when(qi >= 1)
    def _():
        tile(qi - 1, 0)

    tile(qi, 1)
    lam = _diff_lambda(lp_ref)
    for g in range(G):
        a1 = acc[2 * g]
        a2 = acc[2 * g + 1]
        o = a1[:HEAD_W] / a1[HEAD_W:HEAD_W + 1] - lam * (a2[:HEAD_W] / a2[HEAD_W:HEAD_W + 1])
        o = (o * lax.rsqrt(jnp.mean(o * o, axis=0, keepdims=True) + RMS_EPS)).T
        o_ref[0, :, g * HEAD_W:(g + 1) * HEAD_W] = (o * sub_ref[...] * (1.0 - LAMBDA_INIT)).astype(BF16)


def _diff_attn_prompt(z4, k4, v4, rel_bias, bias, lp, subln, *, T, G, far_bucket):
    B, H, L, _ = k4.shape
    nk = L // T
    W = G * HEAD_W
    kv_spec = pl.BlockSpec((1, G, L, HEAD_W), lambda b, h, qi: (b, h, 0, 0))
    return pl.pallas_call(
        functools.partial(_diff_attn_prompt_kernel, T=T, nk=nk, G=G, far_bucket=far_bucket),
        grid=(B, H // G, nk),
        in_specs=[pl.BlockSpec(memory_space=pltpu.SMEM),
                  pl.BlockSpec((1, 1, T, W), lambda b, h, qi: (Z_DQ, b, qi, h)),
                  kv_spec, kv_spec,
                  pl.BlockSpec((G, 3, T, T), lambda b, h, qi: (h, 0, 0, 0)),
                  pl.BlockSpec(lp.shape, lambda b, h, qi: (0, 0)),
                  pl.BlockSpec((1, HEAD_W), lambda b, h, qi: (0, 0))],
        out_specs=pl.BlockSpec((1, T, W), lambda b, h, qi: (b, qi, h)),
        out_shape=jax.ShapeDtypeStruct((B, L, D_MODEL), BF16),
        scratch_shapes=[pltpu.VMEM((G, nk, T, HEAD_W), BF16),
                        pltpu.VMEM((G, nk, HEAD_W + ONES_ROWS, T), BF16),
                        pltpu.VMEM((2 * G, HEAD_W + ONES_ROWS, T), F32),
                        pltpu.VMEM((2 * G, 1, T), F32),
                        pltpu.VMEM((2 * G, T, T), F32)],
        compiler_params=_cparams(("arbitrary", "arbitrary", "arbitrary")),
        name="diff_attn_prompt",
    )(rel_bias, z4, k4, v4, bias, lp, subln)


def _diff_attn_sample_kernel(q_ref, kn_ref, vn_ref, kc_ref, vc_ref, bp_ref, bn_ref, lp_ref, sub_ref, o_ref,
                             *, Ls, hb):
    lam = _diff_lambda(lp_ref)
    nt = (((1,), (1,)), ((), ()))
    for hh in range(hb):
        cols = slice(hh * HEAD_W, (hh + 1) * HEAD_W)
        qa, qb = _split_maps(q_ref[0, 0, :, cols].astype(F32) * DIFF_SCALE)
        q2 = jnp.concatenate([qa, qb], axis=0).astype(BF16)
        kp = kc_ref[0, hh].astype(BF16)
        vp = vc_ref[0, hh].astype(BF16)
        kn = kn_ref[0, :, cols].astype(BF16)
        vn = vn_ref[0, :, cols].astype(BF16)
        s = lax.dot_general(q2, kp, nt, preferred_element_type=F32) + bp_ref[hh]
        sn = lax.dot_general(q2, kn, nt, preferred_element_type=F32) + bn_ref[hh]
        m = jnp.maximum(jnp.max(s, axis=-1, keepdims=True), jnp.max(sn, axis=-1, keepdims=True))
        p = jnp.exp(s - m)
        pn = jnp.exp(sn - m)
        l = jnp.sum(p, axis=-1, keepdims=True) + jnp.sum(pn, axis=-1, keepdims=True)
        o2 = (jnp.dot(p.astype(BF16), vp, preferred_element_type=F32)
              + jnp.dot(pn.astype(BF16), vn, preferred_element_type=F32)) / l
        o = o2[:Ls] - lam * o2[Ls:]
        o_ref[0, :, cols] = (_rms(o, sub_ref[...]) * (1.0 - LAMBDA_INIT)).astype(BF16)


def _diff_attn_sample(z4, kn3, vn3, kc, vc, bias_p, bias_n, lp, subln, *, hb):
    B, H, P, _ = kc.shape
    Ls = kn3.shape[1]
    W = hb * HEAD_W
    cache_spec = pl.BlockSpec((1, hb, P, HEAD_W), lambda b, g: (b, g, 0, 0))
    new_spec = pl.BlockSpec((1, Ls, W), lambda b, g: (b, 0, g))
    return pl.pallas_call(
        functools.partial(_diff_attn_sample_kernel, Ls=Ls, hb=hb),
        grid=(B, H // hb),
        in_specs=[pl.BlockSpec((1, 1, Ls, W), lambda b, g: (Z_DQ, b, 0, g)),
                  new_spec, new_spec, cache_spec, cache_spec,
                  pl.BlockSpec((hb, 2 * Ls, P), lambda b, g: (g, 0, 0)),
                  pl.BlockSpec((hb, 2 * Ls, Ls), lambda b, g: (g, 0, 0)),
                  pl.BlockSpec(lp.shape, lambda b, g: (0, 0)),
                  pl.BlockSpec((1, HEAD_W), lambda b, g: (0, 0))],
        out_specs=pl.BlockSpec((1, Ls, W), lambda b, g: (b, 0, g)),
        out_shape=jax.ShapeDtypeStruct((B, Ls, D_MODEL), BF16),
        compiler_params=_cparams(("arbitrary", "arbitrary")),
        name="diff_attn_sample",
    )(z4, kn3, vn3, kc, vc, bias_p, bias_n, lp, subln)


def _mem_kv_kernel(m_ref, g_ref, w_ref, k_ref, v_ref):
    h = _rms(m_ref[0], g_ref[...]).astype(BF16)
    kv = jnp.dot(h, w_ref[...], preferred_element_type=F32)
    for hh in range(MEM_HEADS):
        k_ref[0, hh] = kv[:, hh * MEM_DH:(hh + 1) * MEM_DH]
        v_ref[0, hh] = kv[:, D_MODEL + hh * MEM_DH:D_MODEL + (hh + 1) * MEM_DH]


def _mem_kv(mem, g, w_bf):
    B, M, _ = mem.shape
    out = jax.ShapeDtypeStruct((B, MEM_HEADS, M, MEM_DH), F32)
    spec = pl.BlockSpec((1, MEM_HEADS, M, MEM_DH), lambda b: (b, 0, 0, 0))
    return pl.pallas_call(
        _mem_kv_kernel,
        grid=(B,),
        in_specs=[pl.BlockSpec((1, M, D_MODEL), lambda b: (b, 0, 0)),
                  pl.BlockSpec((1, D_MODEL), lambda b: (0, 0)),
                  pl.BlockSpec(w_bf.shape, lambda b: (0, 0))],
        out_specs=[spec, spec],
        out_shape=[out, out],
        compiler_params=_cparams(("arbitrary",)),
        name="mem_kv",
    )(mem, g, w_bf)


def _mem_attn_kernel(q_ref, k_ref, v_ref, o_ref, kb, vb, s_scr):
    nt = (((1,), (1,)), ((), ()))

    @pl.when(pl.program_id(1) == 0)
    def _():
        kb[...] = k_ref[0].astype(BF16)
        vb[...] = v_ref[0].astype(BF16)

    for hh in range(MEM_HEADS):
        cols = slice(hh * MEM_DH, (hh + 1) * MEM_DH)
        s_scr[hh] = lax.dot_general(q_ref[0, 0, :, cols], kb[hh], nt, preferred_element_type=F32)
    for hh in range(MEM_HEADS):
        cols = slice(hh * MEM_DH, (hh + 1) * MEM_DH)
        s = s_scr[hh] * (MEM_DH ** -0.5)
        p = jnp.exp(s - jnp.max(s, axis=-1, keepdims=True))
        p = p / jnp.sum(p, axis=-1, keepdims=True)
        o_ref[0, :, cols] = jnp.dot(p.astype(BF16), vb[hh], preferred_element_type=F32).astype(BF16)


def _mem_attn(z4, mk, mv, *, tl):
    _, B, L, _ = z4.shape
    M = mk.shape[2]
    kv_spec = pl.BlockSpec((1, MEM_HEADS, M, MEM_DH), lambda b, t: (b, 0, 0, 0))
    return pl.pallas_call(
        _mem_attn_kernel,
        grid=(B, L // tl),
        in_specs=[pl.BlockSpec((1, 1, tl, D_MODEL), lambda b, t: (Z_MQ, b, t, 0)), kv_spec, kv_spec],
        out_specs=pl.BlockSpec((1, tl, D_MODEL), lambda b, t: (b, t, 0)),
        out_shape=jax.ShapeDtypeStruct((B, L, D_MODEL), BF16),
        scratch_shapes=[pltpu.VMEM((MEM_HEADS, M, MEM_DH), BF16),
                        pltpu.VMEM((MEM_HEADS, M, MEM_DH), BF16),
                        pltpu.VMEM((MEM_HEADS, tl, M), F32)],
        compiler_params=_cparams(("arbitrary", "arbitrary")),
        name="mem_attn",
    )(z4, mk, mv)


def _merge_kernel(x_ref, ya_ref, yb_ref, yc_ref, g0_ref, g1_ref, g2_ref, wb_ref, wo_ref, nf_ref,
                  x1_ref, h2_ref):
    merged = None
    for n, (y_ref, gate_ref) in enumerate(((ya_ref, g0_ref), (yb_ref, g1_ref), (yc_ref, g2_ref))):
        proj = jnp.dot(y_ref[...], wb_ref[n], preferred_element_type=F32)
        term = proj * _sigmoid(gate_ref[0].astype(F32))
        merged = term if merged is None else merged + term
    x1 = x_ref[...] + jnp.dot(merged.astype(BF16), wo_ref[...], preferred_element_type=F32)
    x1_ref[...] = x1
    h2_ref[...] = _rms(x1, nf_ref[...]).astype(BF16)


def _merge(x2d, ya, yb, yc, z3, wb_bf, wo_bf, nf, *, tm):
    N = x2d.shape[0]
    row = pl.BlockSpec((tm, D_MODEL), lambda i: (i, 0))
    gate = lambda n: pl.BlockSpec((1, tm, D_MODEL), lambda i: (Z_G0 + n, i, 0))
    return pl.pallas_call(
        _merge_kernel,
        grid=(N // tm,),
        in_specs=[row, row, row, row, gate(0), gate(1), gate(2),
                  pl.BlockSpec(wb_bf.shape, lambda i: (0, 0, 0)),
                  pl.BlockSpec(wo_bf.shape, lambda i: (0, 0)),
                  pl.BlockSpec((1, D_MODEL), lambda i: (0, 0))],
        out_specs=[row, row],
        out_shape=[jax.ShapeDtypeStruct((N, D_MODEL), F32), jax.ShapeDtypeStruct((N, D_MODEL), BF16)],
        compiler_params=_cparams(("arbitrary",)),
        name="merge",
    )(x2d, ya, yb, yc, z3, z3, z3, wb_bf, wo_bf, nf)


def _ffn_kernel(h_ref, x_ref, cp_ref, wu_ref, cw_ref, cb_ref, wd_ref, nf_ref, y_ref, cn_ref, carry, u_scr, act_scr,
                *, tm, cw, d_ff, nb):
    lt = pl.program_id(1)
    rows_per = tm // nb

    if nb == 1:
        @pl.when(lt == 0)
        def _():
            carry[...] = cp_ref[0]

    h = h_ref[0]
    edge = 8
    erow = lax.broadcasted_iota(jnp.int32, (edge, cw), 0)
    for jc in range(d_ff // cw):
        slot = jc % 2
        bases = (jc * cw, d_ff + jc * cw)
        for hf, base in enumerate(bases):
            u_scr[slot, hf] = jnp.dot(h, wu_ref[:, base:base + cw], preferred_element_type=F32)
        halves = []
        for hf, base in enumerate(bases):
            cols = slice(base, base + cw)
            u = u_scr[slot, hf]
            w0, w1, w2, bb = cw_ref[0:1, cols], cw_ref[1:2, cols], cw_ref[2:3, cols], cb_ref[:, cols]
            u1 = pltpu.roll(u, 1, 0)
            u2 = pltpu.roll(u, 2, 0)
            c = bb + w0 * u2 + w1 * u1 + w2 * u
            pieces = []
            for s in range(nb):
                r0 = s * rows_per
                prev = carry if nb == 1 else cp_ref.at[s]
                p2 = prev[0:1, cols]
                p1 = prev[1:2, cols]
                u1e = jnp.where(erow == 0, p1, u1[r0:r0 + edge])
                u2e = jnp.where(erow == 0, p2, jnp.where(erow == 1, p1, u2[r0:r0 + edge]))
                pieces += [bb + w0 * u2e + w1 * u1e + w2 * u[r0:r0 + edge], c[r0 + edge:r0 + rows_per]]
                tail = u[r0 + rows_per - (FFN_CONV - 1):r0 + rows_per, :]
                if nb == 1:
                    carry[:, cols] = tail
                cn_ref[s, :, cols] = tail
            halves.append(jnp.concatenate(pieces, axis=0))
        gate, val = halves
        act_scr[:, jc * cw:(jc + 1) * cw] = (gate * _sigmoid(gate) * val).astype(BF16)
    y = x_ref[0] + jnp.dot(act_scr[...], wd_ref[...], preferred_element_type=F32)
    y_ref[0] = _rms(y, nf_ref[...])


def _ffn(h2, x1, conv_prev, wu_bf, conv_w, conv_b, wd_bf, nf, *, tm, nb=1, cw=256):
    G, R, _ = x1.shape
    d_ff = wd_bf.shape[0]
    assert nb == 1 or tm == R
    row = pl.BlockSpec((1, tm, D_MODEL), lambda b, t: (b, t, 0))
    state = pl.BlockSpec((nb, FFN_CONV - 1, 2 * d_ff), lambda b, t: (b, 0, 0))
    const = lambda a: pl.BlockSpec(a.shape, lambda b, t: (0,) * a.ndim)
    return pl.pallas_call(
        functools.partial(_ffn_kernel, tm=tm, cw=cw, d_ff=d_ff, nb=nb),
        grid=(G, R // tm),
        in_specs=[row, row, state, const(wu_bf), const(conv_w), const(conv_b), const(wd_bf), const(nf)],
        out_specs=[row, state],
        out_shape=[jax.ShapeDtypeStruct((G, R, D_MODEL), F32),
                   jax.ShapeDtypeStruct((G * nb, FFN_CONV - 1, 2 * d_ff), F32)],
        scratch_shapes=[pltpu.VMEM((FFN_CONV - 1, 2 * d_ff), F32),
                        pltpu.VMEM((2, 2, tm, cw), F32),
                        pltpu.VMEM((tm, d_ff), BF16)],
        compiler_params=_cparams(("arbitrary", "arbitrary")),
        name="ffn",
    )(h2, x1, conv_prev, wu_bf, conv_w, conv_b, wd_bf, nf)


def _layer(x, W, *, hgrn_s0, conv_prev, mem_k, mem_v, attn, tm_proj, tm_merge, tm_ffn, tl_mem, head_layout):
    B, L, _ = x.shape
    N = B * L
    x2d = x.reshape(N, D_MODEL)
    z3, f2d, k_new, v_new = _in_proj(x2d, W["norm_attn"], W["w_in"], tm=tm_proj, rows_per_batch=L,
                                     head_layout=head_layout)
    z4 = z3.reshape(N_ZSLOT, B, L, D_MODEL)
    ya, s_new = _hgrn(z4, f2d.reshape(B, L, D_MODEL), W["lb_logits"], W["hgrn_onorm"], hgrn_s0, L=L,
                      tl=min(L, 512))
    yb, k_new, v_new = attn(z4, k_new, v_new)
    yc = _mem_attn(z4, mem_k, mem_v, tl=tl_mem)
    x1, h2 = _merge(x2d, ya.reshape(N, D_MODEL), yb.reshape(N, D_MODEL), yc.reshape(N, D_MODEL), z3,
                    W["w_branch"], W["w_out"], W["norm_ffn"], tm=tm_merge)
    nb = B if L < tm_ffn else 1
    groups = (B // nb, nb * L, D_MODEL)
    y, conv_new = _ffn(h2.reshape(groups), x1.reshape(groups), conv_prev, W["w_up"], W["conv_w"], W["conv_b"],
                       W["w_down"], W["norm_final"], tm=nb * L if nb > 1 else tm_ffn, nb=nb)
    return y.reshape(B, L, D_MODEL), k_new, v_new, s_new, conv_new


def kernel(x_prompt, x_sample, mem_prompt, cache_diff_k, cache_diff_v, cache_mem_k, cache_mem_v, state_hgrn, state_ffn_conv, rel_bias, hgrn_lb_logits, norm_attn, w_in, hgrn_onorm, diff_lambda, diff_subln, mem_norm, w_mem_kv, w_branch, w_out, norm_ffn, w_up, conv_w, conv_b, w_down, norm_final):
    assert w_in.shape[0] == 1, "single-layer trunk"
    Bp, Lp, _ = x_prompt.shape
    Bs, Ls, _ = x_sample.shape
    P = cache_diff_k.shape[3]
    d_ff2 = w_up.shape[2]
    row = lambda a: a.reshape(1, -1)
    W = dict(norm_attn=row(norm_attn[0]), w_in=w_in[0].astype(BF16), lb_logits=hgrn_lb_logits,
             hgrn_onorm=row(hgrn_onorm[0]), w_branch=w_branch[0].astype(BF16), w_out=w_out[0].astype(BF16),
             norm_ffn=row(norm_ffn[0]), w_up=w_up[0].astype(BF16), conv_w=conv_w[0], conv_b=row(conv_b[0]),
             w_down=w_down[0].astype(BF16), norm_final=row(norm_final))
    lp = diff_lambda[0]
    subln = row(diff_subln[0])

    T = 512
    bias_p, far_bucket = _bias_prompt(rel_bias, T)
    mk, mv = _mem_kv(mem_prompt, row(mem_norm[0]), w_mem_kv[0].astype(BF16))

    def attn_prompt(z4, k4, v4):
        yb = _diff_attn_prompt(z4, k4, v4, rel_bias, bias_p, lp, subln, T=T, G=2, far_bucket=far_bucket)
        return yb, k4, v4

    yp, pk, pv, ps, pc = _layer(
        x_prompt, W, hgrn_s0=jnp.zeros((Bp, N_HEADS, HEAD_W, HEAD_W), F32),
        conv_prev=jnp.zeros((Bp, FFN_CONV - 1, d_ff2), F32), mem_k=mk, mem_v=mv, attn=attn_prompt,
        tm_proj=1024, tm_merge=512, tm_ffn=512, tl_mem=512, head_layout=True)

    bias_sp, bias_sn = _bias_sample(rel_bias, Ls, P)

    def attn_sample(z4, k2d, v2d):
        yb = _diff_attn_sample(z4, k2d.reshape(Bs, Ls, D_MODEL), v2d.reshape(Bs, Ls, D_MODEL),
                               cache_diff_k[0], cache_diff_v[0], bias_sp, bias_sn, lp, subln, hb=8)
        heads = lambda a: jnp.transpose(a.reshape(Bs, Ls, N_HEADS, HEAD_W), (0, 2, 1, 3))
        return yb, heads(k2d), heads(v2d)

    ys, sk, sv, ss, sc = _layer(
        x_sample, W, hgrn_s0=state_hgrn[0], conv_prev=state_ffn_conv[0], mem_k=cache_mem_k[0],
        mem_v=cache_mem_v[0], attn=attn_sample,
        tm_proj=Bs * Ls, tm_merge=Bs * Ls, tm_ffn=512, tl_mem=Ls, head_layout=False)

    return (yp, ys, pk[None], pv[None], ps[None], pc[None], mk[None], mv[None],
            sk[None], sv[None], ss[None], sc[None])
```

```python
import functools
import math

import numpy as np
import jax
import jax.numpy as jnp
from jax import lax
from jax.experimental import pallas as pl
from jax.experimental.pallas import tpu as pltpu

F32 = jnp.float32
BF16 = jnp.bfloat16

D_MODEL = 1024
CHUNK = 64
SUB = 16
HEAD_W = 128
N_HEADS = D_MODEL // HEAD_W
PAIR_W = 2 * HEAD_W
DIFF_DH = 64
DIFF_SCALE = DIFF_DH ** -0.5
MEM_HEADS = 4
MEM_DH = D_MODEL // MEM_HEADS
REL_BUCKETS = 32
REL_MAX_DIST = 128
N_BRANCH = 3
FFN_CONV = 3
RMS_EPS = 1e-6
LAMBDA_INIT = 0.8 - 0.6 * math.exp(-0.3 * 0)
NEG = -1e30
LOG2E = math.log2(math.e)
ONES_ROWS = 16

SEC_HQ, SEC_HF, SEC_HI, SEC_HG, SEC_DQ, SEC_DK, SEC_DV, SEC_MQ, SEC_G0 = range(9)
N_SEC = 11
Z_HQ, Z_HI, Z_HG, Z_DQ, Z_MQ, Z_G0 = 0, 1, 2, 3, 4, 5
N_ZSLOT = 8

V7X_VMEM_LIMIT = 56 * 1024 * 1024


def _cparams(sem, vmem=V7X_VMEM_LIMIT):
    return pltpu.CompilerParams(dimension_semantics=sem, vmem_limit_bytes=vmem)


def _sigmoid(x):
    return 0.5 * jnp.tanh(0.5 * x) + 0.5


def _rms(x, g):
    ms = jnp.mean(x * x, axis=-1, keepdims=True)
    return x * lax.rsqrt(ms + RMS_EPS) * g


def _np_bucket(rel):
    nb = REL_BUCKETS // 2
    ret = np.where(rel > 0, nb, 0)
    n = np.abs(rel)
    max_exact = nb // 2
    large = max_exact + (np.log(np.maximum(n, 1).astype(np.float32) / max_exact)
                         / math.log(REL_MAX_DIST / max_exact) * (nb - max_exact)).astype(np.int32)
    large = np.minimum(large, nb - 1)
    return ret + np.where(n < max_exact, n, large)


def _bucket_segments(lo, hi):
    rel = np.arange(lo, hi + 1, dtype=np.int32)
    b = _np_bucket(rel)
    change = np.nonzero(np.diff(b))[0]
    return int(b[0]), [(int(rel[i + 1]), int(b[i + 1])) for i in change]


def _bias_from_rel(rel, rb_ref, h, first_bucket, segs):
    val = jnp.full(rel.shape, rb_ref[first_bucket, h], F32)
    for lo, bk in segs:
        val = jnp.where(rel >= lo, rb_ref[bk, h], val)
    return val


def _bias_prompt_kernel(rb_ref, o_ref, *, T, first_bucket, segs):
    h = pl.program_id(0)
    kk = lax.broadcasted_iota(jnp.int32, (T, T), 0)
    qq = lax.broadcasted_iota(jnp.int32, (T, T), 1)
    o_ref[0, 0] = _bias_from_rel(kk - qq - T, rb_ref, h, first_bucket, segs) * LOG2E
    diag = _bias_from_rel(kk - qq, rb_ref, h, first_bucket, segs) * LOG2E
    o_ref[0, 1] = jnp.where(jnp.bitwise_and(kk, -CHUNK) <= qq, diag, NEG)


def _bias_prompt(rel_bias, T):
    H = rel_bias.shape[1]
    first_bucket, segs = _bucket_segments(-2 * T, T)
    assert all(lo > -T for lo, _ in segs), "bias must be constant beyond one tile"
    tiles = pl.pallas_call(
        functools.partial(_bias_prompt_kernel, T=T, first_bucket=first_bucket, segs=segs),
        grid=(H,),
        in_specs=[pl.BlockSpec(memory_space=pltpu.SMEM)],
        out_specs=pl.BlockSpec((1, 2, T, T), lambda h: (h, 0, 0, 0)),
        out_shape=jax.ShapeDtypeStruct((H, 2, T, T), F32),
        compiler_params=_cparams(("arbitrary",)),
        name="bias_prompt",
    )(rel_bias)
    return tiles, first_bucket


def _bias_sample_kernel(rb_ref, bp_ref, bn_ref, *, Ls, P, first_bucket, segs):
    h = pl.program_id(0)
    qq = jnp.bitwise_and(lax.broadcasted_iota(jnp.int32, (2 * Ls, P), 0), Ls - 1)
    kk = lax.broadcasted_iota(jnp.int32, (2 * Ls, P), 1)
    bp_ref[0] = _bias_from_rel(kk - P - qq, rb_ref, h, first_bucket, segs)
    qn = jnp.bitwise_and(lax.broadcasted_iota(jnp.int32, (2 * Ls, Ls), 0), Ls - 1)
    kn = lax.broadcasted_iota(jnp.int32, (2 * Ls, Ls), 1)
    bn_ref[0] = _bias_from_rel(kn - qn, rb_ref, h, first_bucket, segs)


def _bias_sample(rel_bias, Ls, P):
    H = rel_bias.shape[1]
    first_bucket, segs = _bucket_segments(-(P + Ls), Ls)
    return pl.pallas_call(
        functools.partial(_bias_sample_kernel, Ls=Ls, P=P, first_bucket=first_bucket, segs=segs),
        grid=(H,),
        in_specs=[pl.BlockSpec(memory_space=pltpu.SMEM)],
        out_specs=[pl.BlockSpec((1, 2 * Ls, P), lambda h: (h, 0, 0)),
                   pl.BlockSpec((1, 2 * Ls, Ls), lambda h: (h, 0, 0))],
        out_shape=[jax.ShapeDtypeStruct((H, 2 * Ls, P), F32),
                   jax.ShapeDtypeStruct((H, 2 * Ls, Ls), F32)],
        compiler_params=_cparams(("arbitrary",)),
        name="bias_sample",
    )(rel_bias)


def _in_proj_kernel(x_ref, g_ref, w_ref, z_ref, f_ref, k_ref, v_ref, h_scr, *, head_layout):
    sec = pl.program_id(1)

    @pl.when(sec == 0)
    def _():
        h_scr[...] = _rms(x_ref[...], g_ref[...]).astype(BF16)

    def section():
        return jnp.dot(h_scr[...], w_ref[...], preferred_element_type=F32)

    def store_heads(ref, acc):
        if head_layout:
            for hh in range(N_HEADS):
                ref[0, hh] = acc[:, hh * HEAD_W:(hh + 1) * HEAD_W]
        else:
            ref[...] = acc

    @pl.when(sec == SEC_HF)
    def _():
        f_ref[...] = section()

    @pl.when(sec == SEC_DK)
    def _():
        store_heads(k_ref, section())

    @pl.when(sec == SEC_DV)
    def _():
        store_heads(v_ref, section())

    @pl.when((sec != SEC_HF) & (sec != SEC_DK) & (sec != SEC_DV))
    def _():
        z_ref[0] = section().astype(BF16)


def _zslot(sec):
    return (sec - (sec >= SEC_HF).astype(jnp.int32) - (sec >= SEC_DK).astype(jnp.int32)
            - (sec >= SEC_DV).astype(jnp.int32))


def _in_proj(x2d, g, w_bf, *, tm, rows_per_batch, head_layout):
    N = x2d.shape[0]
    nt = N // tm
    if head_layout:
        B = N // rows_per_batch
        nlt = rows_per_batch // tm
        kv_shape = jax.ShapeDtypeStruct((B, N_HEADS, rows_per_batch, HEAD_W), F32)
        kv_spec = pl.BlockSpec((1, N_HEADS, tm, HEAD_W), lambda i, j: (i // nlt, 0, i % nlt, 0))
    else:
        kv_shape = jax.ShapeDtypeStruct((N, D_MODEL), F32)
        kv_spec = pl.BlockSpec((tm, D_MODEL), lambda i, j: (i, 0))
    return pl.pallas_call(
        functools.partial(_in_proj_kernel, head_layout=head_layout),
        grid=(nt, N_SEC),
        in_specs=[pl.BlockSpec((tm, D_MODEL), lambda i, j: (i, 0)),
                  pl.BlockSpec((1, D_MODEL), lambda i, j: (0, 0)),
                  pl.BlockSpec((D_MODEL, D_MODEL), lambda i, j: (0, j))],
        out_specs=[pl.BlockSpec((1, tm, D_MODEL), lambda i, j: (_zslot(j), i, 0)),
                   pl.BlockSpec((tm, D_MODEL), lambda i, j: (i, 0)),
                   kv_spec, kv_spec],
        out_shape=[jax.ShapeDtypeStruct((N_ZSLOT, N, D_MODEL), BF16),
                   jax.ShapeDtypeStruct((N, D_MODEL), F32),
                   kv_shape, kv_shape],
        scratch_shapes=[pltpu.VMEM((tm, D_MODEL), BF16)],
        compiler_params=_cparams(("arbitrary", "arbitrary")),
        name="in_proj",
    )(x2d, g, w_bf)


def _block_diag(a, b):
    za = jnp.zeros((a.shape[0], b.shape[1]), a.dtype)
    zb = jnp.zeros((b.shape[0], a.shape[1]), b.dtype)
    return jnp.concatenate([jnp.concatenate([a, za], axis=1), jnp.concatenate([zb, b], axis=1)], axis=0)


def _hgrn_kernel(q_ref, f_ref, i_ref, g_ref, lbl_ref, on_ref, s0_ref, ya_ref, sn_ref,
                 st_scr, sc_scr, o_scr, upd_scr, *, TL, C):
    lt = pl.program_id(1)
    nsub = C // SUB
    lbl = lbl_ref[...]
    e = jnp.exp(lbl - jnp.max(lbl, axis=0, keepdims=True))
    lb_all = e[0:1] / jnp.sum(e, axis=0, keepdims=True)

    @pl.when(lt == 0)
    def _():
        st_scr[...] = s0_ref[0]

    crow = lax.broadcasted_iota(jnp.int32, (C, D_MODEL), 0)
    width = max(nsub * C, HEAD_W)
    wrow = lax.broadcasted_iota(jnp.int32, (C, width), 0)
    wcol = lax.broadcasted_iota(jnp.int32, (C, width), 1)
    base = jnp.bitwise_and(wrow, -SUB) * (C // SUB)
    keep = (wcol >= base) & (wcol <= base + wrow)
    keep2 = jnp.concatenate([keep, keep], axis=1)
    onorm = on_ref[...]
    nt = (((1,), (1,)), ((), ()))

    def chunk(c, carry):
        r0 = pl.multiple_of(c * C, C)
        f = f_ref[0, pl.ds(r0, C), :]
        fg = lb_all + (1.0 - lb_all) * _sigmoid(f)
        logf = jnp.log(fg)
        kk = 1.0 - fg
        b = logf
        sh = 1
        while sh < C:
            if sh % 8 == 0:
                shifted = jnp.concatenate([jnp.zeros((sh, D_MODEL), F32), b[:C - sh]], axis=0)
            else:
                shifted = jnp.where(crow >= sh, pltpu.roll(b, sh, 0), 0.0)
            b = b + shifted
            sh *= 2
        r_blocks = [jnp.zeros((SUB, D_MODEL), F32)]
        for i in range(1, nsub):
            r_blocks.append(jnp.broadcast_to(b[SUB * i - 1:SUB * i, :], (SUB, D_MODEL)))
        r = r_blocks[0] if nsub == 1 else jnp.concatenate(r_blocks, axis=0)
        b_last = b[C - 1:C, :]
        q = q_ref[0, 0, pl.ds(r0, C), :].astype(F32)
        q_inter = (q * jnp.exp(b)).astype(BF16)
        q_intra = (q * jnp.exp(b - r)).astype(BF16)
        k_state = kk * jnp.exp(b_last - b)
        k_sub = []
        for i in range(nsub):
            n = SUB * (i + 1)
            ki = (kk[:n] * jnp.exp(r[SUB * i:SUB * i + 1, :] - b[:n])).astype(BF16)
            if n < C:
                ki = jnp.concatenate([ki, jnp.zeros((C - n, D_MODEL), BF16)], axis=0)
            k_sub.append(ki)
        st_decay = jnp.exp(b_last)
        g = g_ref[0, 0, pl.ds(r0, C), :].astype(F32)
        gate = g * _sigmoid(g)
        pairs = range(N_HEADS // 2)
        pcs = [slice(p * PAIR_W, (p + 1) * PAIR_W) for p in pairs]
        hcs = [[slice((2 * p + a) * HEAD_W, (2 * p + a + 1) * HEAD_W) for a in range(2)] for p in pairs]
        zrows = jnp.zeros((HEAD_W - C, HEAD_W), BF16)
        vpads = []
        for p in pairs:
            k4 = []
            for a in range(2):
                pieces = [k[:, hcs[p][a]] for k in k_sub]
                if nsub * C < width:
                    pieces.append(jnp.zeros((width - nsub * C, HEAD_W), BF16))
                k4.append(jnp.concatenate(pieces, axis=0))
            vpads.append([jnp.concatenate([i_ref[0, 0, pl.ds(r0, C), hcs[p][a]], zrows], axis=0)
                          for a in range(2)])
            sc_scr[p] = lax.dot_general(q_intra[:, pcs[p]], _block_diag(k4[0], k4[1]), nt,
                                        preferred_element_type=F32)
        dcols = []
        for p in pairs:
            kT = []
            for a in range(2):
                kext = jnp.concatenate([k_state[:, hcs[p][a]],
                                        jnp.broadcast_to(st_decay[:, hcs[p][a]], (8, HEAD_W)),
                                        jnp.zeros((HEAD_W - C - 8, HEAD_W), F32)], axis=0).T
                dcols.append(kext[:, C:C + 1])
                kT.append(kext.astype(BF16))
            upd_scr[p] = jnp.dot(jnp.concatenate(kT, axis=1), _block_diag(vpads[p][0], vpads[p][1]),
                                 preferred_element_type=F32)
        for p in pairs:
            w = jnp.where(keep2, sc_scr[p], 0.0)
            lhs = []
            rhs = []
            for a in range(2):
                t = w[:, a * width:a * width + HEAD_W]
                for i in range(1, width // HEAD_W):
                    t = t + w[:, a * width + i * HEAD_W:a * width + (i + 1) * HEAD_W]
                if nsub > 1 and C < HEAD_W:
                    assert HEAD_W == 2 * C
                    t = t + pltpu.roll(t, C, 1)
                lhs += [t.astype(BF16), q_inter[:, hcs[p][a]]]
                rhs.append(jnp.concatenate([vpads[p][a], st_scr[2 * p + a].astype(BF16)], axis=0))
            o_scr[p] = jnp.dot(jnp.concatenate(lhs, axis=1), _block_diag(rhs[0], rhs[1]),
                               preferred_element_type=F32)
        for p in pairs:
            for a in range(2):
                hh = 2 * p + a
                st_scr[hh] = st_scr[hh] * dcols[hh] + upd_scr[p, :, a * HEAD_W:(a + 1) * HEAD_W]
        for p in pairs:
            for a in range(2):
                o = o_scr[p, :, a * HEAD_W:(a + 1) * HEAD_W]
                ya_ref[0, pl.ds(r0, C), hcs[p][a]] = (_rms(o, onorm) * gate[:, hcs[p][a]]).astype(BF16)
        return carry

    lax.fori_loop(0, TL // C, chunk, 0)

    @pl.when(lt == pl.num_programs(1) - 1)
    def _():
        sn_ref[0] = st_scr[...]


def _hgrn(z4, f3, lb_logits, onorm, s0, *, L, tl):
    B = z4.shape[1]
    C = min(CHUNK, L)
    slot = lambda s: pl.BlockSpec((1, 1, tl, D_MODEL), lambda b, t: (s, b, t, 0))
    row = pl.BlockSpec((1, tl, D_MODEL), lambda b, t: (b, t, 0))
    state = pl.BlockSpec((1, N_HEADS, HEAD_W, HEAD_W), lambda b, t: (b, 0, 0, 0))
    return pl.pallas_call(
        functools.partial(_hgrn_kernel, TL=tl, C=C),
        grid=(B, L // tl),
        in_specs=[slot(Z_HQ), row, slot(Z_HI), slot(Z_HG),
                  pl.BlockSpec(lb_logits.shape, lambda b, t: (0, 0)),
                  pl.BlockSpec((1, HEAD_W), lambda b, t: (0, 0)),
                  state],
        out_specs=[row, state],
        out_shape=[jax.ShapeDtypeStruct((B, L, D_MODEL), BF16),
                   jax.ShapeDtypeStruct((B, N_HEADS, HEAD_W, HEAD_W), F32)],
        scratch_shapes=[pltpu.VMEM((N_HEADS, HEAD_W, HEAD_W), F32),
                        pltpu.VMEM((N_HEADS // 2, C, 2 * max((C // SUB) * C, HEAD_W)), F32),
                        pltpu.VMEM((N_HEADS // 2, C, PAIR_W), F32),
                        pltpu.VMEM((N_HEADS // 2, HEAD_W, PAIR_W), F32)],
        compiler_params=_cparams(("arbitrary", "arbitrary")),
        name="hgrn",
    )(z4, f3, z4, z4, lb_logits, onorm, s0)


def _diff_lambda(lp_ref):
    lp = lp_ref[...]
    a = jnp.sum(lp[0:1] * lp[1:2], axis=-1, keepdims=True)
    b = jnp.sum(lp[2:3] * lp[3:4], axis=-1, keepdims=True)
    return jnp.exp(a) - jnp.exp(b) + LAMBDA_INIT


def _split_maps(q):
    lane = lax.broadcasted_iota(jnp.int32, q.shape, 1)
    return jnp.where(lane < DIFF_DH, q, 0.0), jnp.where(lane >= DIFF_DH, q, 0.0)


def _diff_attn_prompt_kernel(rb_ref, q_ref, k_ref, v_ref, bias_ref, lp_ref, sub_ref, o_ref, kb, vT, acc, m_scr, s_scr,
                             *, T, nk, G, far_bucket):
    hg = pl.program_id(1)
    nt = (((1,), (1,)), ((), ()))

    ones = jnp.ones((ONES_ROWS, T), BF16)
    for g in range(G):
        for j in range(nk):
            kb[g, j] = (k_ref[0, g, j * T:(j + 1) * T, :] * (DIFF_SCALE * LOG2E)).astype(BF16)
            vT[g, j, :HEAD_W, :] = v_ref[0, g, j * T:(j + 1) * T, :].T.astype(BF16)
            vT[g, j, HEAD_W:, :] = ones
    lam = _diff_lambda(lp_ref)

    def tile(qs, j, bias_slot, first, slot):
        for g in range(G):
            kt = kb[g, j]
            for mp in range(2):
                s_scr[slot, 2 * g + mp] = lax.dot_general(kt, qs[g][mp], nt, preferred_element_type=F32)
        for g in range(G):
            vt = vT[g, j]
            if bias_slot is None:
                shift = rb_ref[far_bucket, hg * G + g] * LOG2E
            else:
                bias = bias_ref[g, bias_slot]
            for mp in range(2):
                idx = 2 * g + mp
                s = s_scr[slot, idx]
                if bias_slot is None:
                    tmax = jnp.max(s, axis=0, keepdims=True) + shift
                else:
                    s = s + bias
                    tmax = jnp.max(s, axis=0, keepdims=True)
                if first:
                    mn = tmax
                else:
                    m = m_scr[idx]
                    mn = jnp.maximum(m, tmax)
                p = jnp.exp2(s - (mn - shift)) if bias_slot is None else jnp.exp2(s - mn)
                m_scr[idx] = mn
                pv = jnp.dot(vt, p.astype(BF16), preferred_element_type=F32)
                acc[idx] = pv if first else jnp.exp2(m - mn) * acc[idx] + pv

    n_tiles = 0
    for qi in range(nk):
        rows = slice(qi * T, (qi + 1) * T)
        qs = [_split_maps(q_ref[0, 0, rows, g * HEAD_W:(g + 1) * HEAD_W]) for g in range(G)]
        order = [(j, None) for j in range(qi - 1)] + ([(qi - 1, 0)] if qi >= 1 else []) + [(qi, 1)]
        for t, (j, bias_slot) in enumerate(order):
            tile(qs, j, bias_slot, t == 0, n_tiles % 2)
            n_tiles += 1
        for g in range(G):
            a1 = acc[2 * g]
            a2 = acc[2 * g + 1]
            o = (a1[:HEAD_W] * (1.0 / a1[HEAD_W:HEAD_W + 1])
                 - a2[:HEAD_W] * (lam / a2[HEAD_W:HEAD_W + 1]))
            o = (o * lax.rsqrt(jnp.mean(o * o, axis=0, keepdims=True) + RMS_EPS)).T
            o_ref[0, rows, g * HEAD_W:(g + 1) * HEAD_W] = (o * sub_ref[...] * (1.0 - LAMBDA_INIT)).astype(BF16)


def _diff_attn_prompt(z4, k4, v4, rel_bias, bias, lp, subln, *, T, G, far_bucket):
    B, H, L, _ = k4.shape
    nk = L // T
    W = G * HEAD_W
    kv_spec = pl.BlockSpec((1, G, L, HEAD_W), lambda b, h: (b, h, 0, 0))
    return pl.pallas_call(
        functools.partial(_diff_attn_prompt_kernel, T=T, nk=nk, G=G, far_bucket=far_bucket),
        grid=(B, H // G),
        in_specs=[pl.BlockSpec(memory_space=pltpu.SMEM),
                  pl.BlockSpec((1, 1, L, W), lambda b, h: (Z_DQ, b, 0, h)),
                  kv_spec, kv_spec,
                  pl.BlockSpec((G, 2, T, T), lambda b, h: (h, 0, 0, 0)),
                  pl.BlockSpec(lp.shape, lambda b, h: (0, 0)),
                  pl.BlockSpec((1, HEAD_W), lambda b, h: (0, 0))],
        out_specs=pl.BlockSpec((1, L, W), lambda b, h: (b, 0, h)),
        out_shape=jax.ShapeDtypeStruct((B, L, D_MODEL), BF16),
        scratch_shapes=[pltpu.VMEM((G, nk, T, HEAD_W), BF16),
                        pltpu.VMEM((G, nk, HEAD_W + ONES_ROWS, T), BF16),
                        pltpu.VMEM((2 * G, HEAD_W + ONES_ROWS, T), F32),
                        pltpu.VMEM((2 * G, 1, T), F32),
                        pltpu.VMEM((2, 2 * G, T, T), F32)],
        compiler_params=_cparams(("arbitrary", "arbitrary")),
        name="diff_attn_prompt",
    )(rel_bias, z4, k4, v4, bias, lp, subln)


def _diff_attn_sample_kernel(q_ref, kn_ref, vn_ref, kc_ref, vc_ref, bp_ref, bn_ref, lp_ref, sub_ref, o_ref,
                             *, Ls, hb):
    lam = _diff_lambda(lp_ref)
    nt = (((1,), (1,)), ((), ()))
    for hh in range(hb):
        cols = slice(hh * HEAD_W, (hh + 1) * HEAD_W)
        qa, qb = _split_maps(q_ref[0, 0, :, cols].astype(F32) * DIFF_SCALE)
        q2 = jnp.concatenate([qa, qb], axis=0).astype(BF16)
        kp = kc_ref[0, hh].astype(BF16)
        vp = vc_ref[0, hh].astype(BF16)
        kn = kn_ref[0, :, cols].astype(BF16)
        vn = vn_ref[0, :, cols].astype(BF16)
        s = lax.dot_general(q2, kp, nt, preferred_element_type=F32) + bp_ref[hh]
        sn = lax.dot_general(q2, kn, nt, preferred_element_type=F32) + bn_ref[hh]
        m = jnp.maximum(jnp.max(s, axis=-1, keepdims=True), jnp.max(sn, axis=-1, keepdims=True))
        p = jnp.exp(s - m)
        pn = jnp.exp(sn - m)
        l = jnp.sum(p, axis=-1, keepdims=True) + jnp.sum(pn, axis=-1, keepdims=True)
        o2 = (jnp.dot(p.astype(BF16), vp, preferred_element_type=F32)
              + jnp.dot(pn.astype(BF16), vn, preferred_element_type=F32)) / l
        o = o2[:Ls] - lam * o2[Ls:]
        o_ref[0, :, cols] = (_rms(o, sub_ref[...]) * (1.0 - LAMBDA_INIT)).astype(BF16)


def _diff_attn_sample(z4, kn3, vn3, kc, vc, bias_p, bias_n, lp, subln, *, hb):
    B, H, P, _ = kc.shape
    Ls = kn3.shape[1]
    W = hb * HEAD_W
    cache_spec = pl.BlockSpec((1, hb, P, HEAD_W), lambda b, g: (b, g, 0, 0))
    new_spec = pl.BlockSpec((1, Ls, W), lambda b, g: (b, 0, g))
    return pl.pallas_call(
        functools.partial(_diff_attn_sample_kernel, Ls=Ls, hb=hb),
        grid=(B, H // hb),
        in_specs=[pl.BlockSpec((1, 1, Ls, W), lambda b, g: (Z_DQ, b, 0, g)),
                  new_spec, new_spec, cache_spec, cache_spec,
                  pl.BlockSpec((hb, 2 * Ls, P), lambda b, g: (g, 0, 0)),
                  pl.BlockSpec((hb, 2 * Ls, Ls), lambda b, g: (g, 0, 0)),
                  pl.BlockSpec(lp.shape, lambda b, g: (0, 0)),
                  pl.BlockSpec((1, HEAD_W), lambda b, g: (0, 0))],
        out_specs=pl.BlockSpec((1, Ls, W), lambda b, g: (b, 0, g)),
        out_shape=jax.ShapeDtypeStruct((B, Ls, D_MODEL), BF16),
        compiler_params=_cparams(("arbitrary", "arbitrary")),
        name="diff_attn_sample",
    )(z4, kn3, vn3, kc, vc, bias_p, bias_n, lp, subln)


def _mem_kv_kernel(m_ref, g_ref, w_ref, k_ref, v_ref):
    h = _rms(m_ref[0], g_ref[...]).astype(BF16)
    kv = jnp.dot(h, w_ref[...], preferred_element_type=F32)
    for hh in range(MEM_HEADS):
        k_ref[0, hh] = kv[:, hh * MEM_DH:(hh + 1) * MEM_DH]
        v_ref[0, hh] = kv[:, D_MODEL + hh * MEM_DH:D_MODEL + (hh + 1) * MEM_DH]


def _mem_kv(mem, g, w_bf):
    B, M, _ = mem.shape
    out = jax.ShapeDtypeStruct((B, MEM_HEADS, M, MEM_DH), F32)
    spec = pl.BlockSpec((1, MEM_HEADS, M, MEM_DH), lambda b: (b, 0, 0, 0))
    return pl.pallas_call(
        _mem_kv_kernel,
        grid=(B,),
        in_specs=[pl.BlockSpec((1, M, D_MODEL), lambda b: (b, 0, 0)),
                  pl.BlockSpec((1, D_MODEL), lambda b: (0, 0)),
                  pl.BlockSpec(w_bf.shape, lambda b: (0, 0))],
        out_specs=[spec, spec],
        out_shape=[out, out],
        compiler_params=_cparams(("arbitrary",)),
        name="mem_kv",
    )(mem, g, w_bf)


def _mem_attn_kernel(q_ref, k_ref, v_ref, o_ref, kb, vb, s_scr):
    nt = (((1,), (1,)), ((), ()))

    @pl.when(pl.program_id(1) == 0)
    def _():
        kb[...] = k_ref[0].astype(BF16)
        vb[...] = v_ref[0].astype(BF16)

    for hh in range(MEM_HEADS):
        cols = slice(hh * MEM_DH, (hh + 1) * MEM_DH)
        s_scr[hh] = lax.dot_general(q_ref[0, 0, :, cols], kb[hh], nt, preferred_element_type=F32)
    for hh in range(MEM_HEADS):
        cols = slice(hh * MEM_DH, (hh + 1) * MEM_DH)
        s = s_scr[hh] * (MEM_DH ** -0.5)
        p = jnp.exp(s - jnp.max(s, axis=-1, keepdims=True))
        p = p / jnp.sum(p, axis=-1, keepdims=True)
        o_ref[0, :, cols] = jnp.dot(p.astype(BF16), vb[hh], preferred_element_type=F32).astype(BF16)


def _mem_attn(z4, mk, mv, *, tl):
    _, B, L, _ = z4.shape
    M = mk.shape[2]
    kv_spec = pl.BlockSpec((1, MEM_HEADS, M, MEM_DH), lambda b, t: (b, 0, 0, 0))
    return pl.pallas_call(
        _mem_attn_kernel,
        grid=(B, L // tl),
        in_specs=[pl.BlockSpec((1, 1, tl, D_MODEL), lambda b, t: (Z_MQ, b, t, 0)), kv_spec, kv_spec],
        out_specs=pl.BlockSpec((1, tl, D_MODEL), lambda b, t: (b, t, 0)),
        out_shape=jax.ShapeDtypeStruct((B, L, D_MODEL), BF16),
        scratch_shapes=[pltpu.VMEM((MEM_HEADS, M, MEM_DH), BF16),
                        pltpu.VMEM((MEM_HEADS, M, MEM_DH), BF16),
                        pltpu.VMEM((MEM_HEADS, tl, M), F32)],
        compiler_params=_cparams(("arbitrary", "arbitrary")),
        name="mem_attn",
    )(z4, mk, mv)


def _merge_kernel(x_ref, ya_ref, yb_ref, yc_ref, g0_ref, g1_ref, g2_ref, wb_ref, wo_ref, nf_ref,
                  x1_ref, h2_ref):
    merged = None
    for n, (y_ref, gate_ref) in enumerate(((ya_ref, g0_ref), (yb_ref, g1_ref), (yc_ref, g2_ref))):
        proj = jnp.dot(y_ref[...], wb_ref[n], preferred_element_type=F32)
        term = proj * _sigmoid(gate_ref[0].astype(F32))
        merged = term if merged is None else merged + term
    x1 = x_ref[...] + jnp.dot(merged.astype(BF16), wo_ref[...], preferred_element_type=F32)
    x1_ref[...] = x1
    h2_ref[...] = _rms(x1, nf_ref[...]).astype(BF16)


def _merge(x2d, ya, yb, yc, z3, wb_bf, wo_bf, nf, *, tm):
    N = x2d.shape[0]
    row = pl.BlockSpec((tm, D_MODEL), lambda i: (i, 0))
    gate = lambda n: pl.BlockSpec((1, tm, D_MODEL), lambda i: (Z_G0 + n, i, 0))
    return pl.pallas_call(
        _merge_kernel,
        grid=(N // tm,),
        in_specs=[row, row, row, row, gate(0), gate(1), gate(2),
                  pl.BlockSpec(wb_bf.shape, lambda i: (0, 0, 0)),
                  pl.BlockSpec(wo_bf.shape, lambda i: (0, 0)),
                  pl.BlockSpec((1, D_MODEL), lambda i: (0, 0))],
        out_specs=[row, row],
        out_shape=[jax.ShapeDtypeStruct((N, D_MODEL), F32), jax.ShapeDtypeStruct((N, D_MODEL), BF16)],
        compiler_params=_cparams(("arbitrary",)),
        name="merge",
    )(x2d, ya, yb, yc, z3, z3, z3, wb_bf, wo_bf, nf)


def _ffn_kernel(h_ref, x_ref, cp_ref, wu_ref, cw_ref, cb_ref, wd_ref, nf_ref, y_ref, cn_ref, carry, u_scr, act_scr,
                *, tm, cw, d_ff, nb):
    lt = pl.program_id(1)
    rows_per = tm // nb

    if nb == 1:
        @pl.when(lt == 0)
        def _():
            carry[...] = cp_ref[0]

    h = h_ref[0]
    edge = 8
    erow = lax.broadcasted_iota(jnp.int32, (edge, cw), 0)
    for jc in range(d_ff // cw):
        slot = jc % 2
        bases = (jc * cw, d_ff + jc * cw)
        for hf, base in enumerate(bases):
            u_scr[slot, hf] = jnp.dot(h, wu_ref[:, base:base + cw], preferred_element_type=F32)
        halves = []
        for hf, base in enumerate(bases):
            cols = slice(base, base + cw)
            u = u_scr[slot, hf]
            w0, w1, w2, bb = cw_ref[0:1, cols], cw_ref[1:2, cols], cw_ref[2:3, cols], cb_ref[:, cols]
            u1 = pltpu.roll(u, 1, 0)
            u2 = pltpu.roll(u, 2, 0)
            c = bb + w0 * u2 + w1 * u1 + w2 * u
            pieces = []
            for s in range(nb):
                r0 = s * rows_per
                prev = carry if nb == 1 else cp_ref.at[s]
                p2 = prev[0:1, cols]
                p1 = prev[1:2, cols]
                u1e = jnp.where(erow == 0, p1, u1[r0:r0 + edge])
                u2e = jnp.where(erow == 0, p2, jnp.where(erow == 1, p1, u2[r0:r0 + edge]))
                pieces += [bb + w0 * u2e + w1 * u1e + w2 * u[r0:r0 + edge], c[r0 + edge:r0 + rows_per]]
                tail = u[r0 + rows_per - (FFN_CONV - 1):r0 + rows_per, :]
                if nb == 1:
                    carry[:, cols] = tail
                cn_ref[s, :, cols] = tail
            halves.append(jnp.concatenate(pieces, axis=0))
        gate, val = halves
        act_scr[:, jc * cw:(jc + 1) * cw] = (gate * _sigmoid(gate) * val).astype(BF16)
    y = x_ref[0] + jnp.dot(act_scr[...], wd_ref[...], preferred_element_type=F32)
    y_ref[0] = _rms(y, nf_ref[...])


def _ffn(h2, x1, conv_prev, wu_bf, conv_w, conv_b, wd_bf, nf, *, tm, nb=1, cw=256):
    G, R, _ = x1.shape
    d_ff = wd_bf.shape[0]
    assert nb == 1 or tm == R
    row = pl.BlockSpec((1, tm, D_MODEL), lambda b, t: (b, t, 0))
    state = pl.BlockSpec((nb, FFN_CONV - 1, 2 * d_ff), lambda b, t: (b, 0, 0))
    const = lambda a: pl.BlockSpec(a.shape, lambda b, t: (0,) * a.ndim)
    return pl.pallas_call(
        functools.partial(_ffn_kernel, tm=tm, cw=cw, d_ff=d_ff, nb=nb),
        grid=(G, R // tm),
        in_specs=[row, row, state, const(wu_bf), const(conv_w), const(conv_b), const(wd_bf), const(nf)],
        out_specs=[row, state],
        out_shape=[jax.ShapeDtypeStruct((G, R, D_MODEL), F32),
                   jax.ShapeDtypeStruct((G * nb, FFN_CONV - 1, 2 * d_ff), F32)],
        scratch_shapes=[pltpu.VMEM((FFN_CONV - 1, 2 * d_ff), F32),
                        pltpu.VMEM((2, 2, tm, cw), F32),
                        pltpu.VMEM((tm, d_ff), BF16)],
        compiler_params=_cparams(("arbitrary", "arbitrary")),
        name="ffn",
    )(h2, x1, conv_prev, wu_bf, conv_w, conv_b, wd_bf, nf)


def _layer(x, W, *, hgrn_s0, conv_prev, mem_k, mem_v, attn, tm_proj, tm_merge, tm_ffn, tl_mem, head_layout):
    B, L, _ = x.shape
    N = B * L
    x2d = x.reshape(N, D_MODEL)
    z3, f2d, k_new, v_new = _in_proj(x2d, W["norm_attn"], W["w_in"], tm=tm_proj, rows_per_batch=L,
                                     head_layout=head_layout)
    z4 = z3.reshape(N_ZSLOT, B, L, D_MODEL)
    ya, s_new = _hgrn(z4, f2d.reshape(B, L, D_MODEL), W["lb_logits"], W["hgrn_onorm"], hgrn_s0, L=L,
                      tl=min(L, 512))
    yb, k_new, v_new = attn(z4, k_new, v_new)
    yc = _mem_attn(z4, mem_k, mem_v, tl=tl_mem)
    x1, h2 = _merge(x2d, ya.reshape(N, D_MODEL), yb.reshape(N, D_MODEL), yc.reshape(N, D_MODEL), z3,
                    W["w_branch"], W["w_out"], W["norm_ffn"], tm=tm_merge)
    nb = B if L < tm_ffn else 1
    groups = (B // nb, nb * L, D_MODEL)
    y, conv_new = _ffn(h2.reshape(groups), x1.reshape(groups), conv_prev, W["w_up"], W["conv_w"], W["conv_b"],
                       W["w_down"], W["norm_final"], tm=nb * L if nb > 1 else tm_ffn, nb=nb)
    return y.reshape(B, L, D_MODEL), k_new, v_new, s_new, conv_new


def kernel(x_prompt, x_sample, mem_prompt, cache_diff_k, cache_diff_v, cache_mem_k, cache_mem_v, state_hgrn, state_ffn_conv, rel_bias, hgrn_lb_logits, norm_attn, w_in, hgrn_onorm, diff_lambda, diff_subln, mem_norm, w_mem_kv, w_branch, w_out, norm_ffn, w_up, conv_w, conv_b, w_down, norm_final):
    assert w_in.shape[0] == 1, "single-layer trunk"
    Bp, Lp, _ = x_prompt.shape
    Bs, Ls, _ = x_sample.shape
    P = cache_diff_k.shape[3]
    d_ff2 = w_up.shape[2]
    row = lambda a: a.reshape(1, -1)
    W = dict(norm_attn=row(norm_attn[0]), w_in=w_in[0].astype(BF16), lb_logits=hgrn_lb_logits,
             hgrn_onorm=row(hgrn_onorm[0]), w_branch=w_branch[0].astype(BF16), w_out=w_out[0].astype(BF16),
             norm_ffn=row(norm_ffn[0]), w_up=w_up[0].astype(BF16), conv_w=conv_w[0], conv_b=row(conv_b[0]),
             w_down=w_down[0].astype(BF16), norm_final=row(norm_final))
    lp = diff_lambda[0]
    subln = row(diff_subln[0])

    T = 512
    bias_p, far_bucket = _bias_prompt(rel_bias, T)
    mk, mv = _mem_kv(mem_prompt, row(mem_norm[0]), w_mem_kv[0].astype(BF16))

    def attn_prompt(z4, k4, v4):
        yb = _diff_attn_prompt(z4, k4, v4, rel_bias, bias_p, lp, subln, T=T, G=2, far_bucket=far_bucket)
        return yb, k4, v4

    yp, pk, pv, ps, pc = _layer(
        x_prompt, W, hgrn_s0=jnp.zeros((Bp, N_HEADS, HEAD_W, HEAD_W), F32),
        conv_prev=jnp.zeros((Bp, FFN_CONV - 1, d_ff2), F32), mem_k=mk, mem_v=mv, attn=attn_prompt,
        tm_proj=1024, tm_merge=512, tm_ffn=512, tl_mem=512, head_layout=True)

    bias_sp, bias_sn = _bias_sample(rel_bias, Ls, P)

    def attn_sample(z4, k2d, v2d):
        yb = _diff_attn_sample(z4, k2d.reshape(Bs, Ls, D_MODEL), v2d.reshape(Bs, Ls, D_MODEL),
                               cache_diff_k[0], cache_diff_v[0], bias_sp, bias_sn, lp, subln, hb=8)
        heads = lambda a: jnp.transpose(a.reshape(Bs, Ls, N_HEADS, HEAD_W), (0, 2, 1, 3))
        return yb, heads(k2d), heads(v2d)

    ys, sk, sv, ss, sc = _layer(
        x_sample, W, hgrn_s0=state_hgrn[0], conv_prev=state_ffn_conv[0], mem_k=cache_mem_k[0],
        mem_v=cache_mem_v[0], attn=attn_sample,
        tm_proj=Bs * Ls, tm_merge=Bs * Ls, tm_ffn=512, tl_mem=Ls, head_layout=False)

    return (yp, ys, pk[None], pv[None], ps[None], pc[None], mk[None], mv[None],
            sk[None], sv[None], ss[None], sc[None])
```

```python
import functools
import math

import numpy as np
import jax
import jax.numpy as jnp
from jax import lax
from jax.experimental import pallas as pl
from jax.experimental.pallas import tpu as pltpu

F32 = jnp.float32
BF16 = jnp.bfloat16

D_MODEL = 1024
CHUNK = 64
SUB = 16
HEAD_W = 128
N_HEADS = D_MODEL // HEAD_W
PAIR_W = 2 * HEAD_W
DIFF_DH = 64
DIFF_SCALE = DIFF_DH ** -0.5
MEM_HEADS = 4
MEM_DH = D_MODEL // MEM_HEADS
REL_BUCKETS = 32
REL_MAX_DIST = 128
N_BRANCH = 3
FFN_CONV = 3
RMS_EPS = 1e-6
LAMBDA_INIT = 0.8 - 0.6 * math.exp(-0.3 * 0)
NEG = -1e30
LOG2E = math.log2(math.e)
ONES_ROWS = 16

SEC_HQ, SEC_HF, SEC_HI, SEC_HG, SEC_DQ, SEC_DK, SEC_DV, SEC_MQ, SEC_G0 = range(9)
N_SEC = 11
Z_HQ, Z_HI, Z_HG, Z_DQ, Z_MQ, Z_G0 = 0, 1, 2, 3, 4, 5
N_ZSLOT = 8

V7X_VMEM_LIMIT = 56 * 1024 * 1024


def _cparams(sem, vmem=V7X_VMEM_LIMIT):
    return pltpu.CompilerParams(dimension_semantics=sem, vmem_limit_bytes=vmem)


def _sigmoid(x):
    return 0.5 * jnp.tanh(0.5 * x) + 0.5


def _rms(x, g):
    ms = jnp.mean(x * x, axis=-1, keepdims=True)
    return x * lax.rsqrt(ms + RMS_EPS) * g


def _np_bucket(rel):
    nb = REL_BUCKETS // 2
    ret = np.where(rel > 0, nb, 0)
    n = np.abs(rel)
    max_exact = nb // 2
    large = max_exact + (np.log(np.maximum(n, 1).astype(np.float32) / max_exact)
                         / math.log(REL_MAX_DIST / max_exact) * (nb - max_exact)).astype(np.int32)
    large = np.minimum(large, nb - 1)
    return ret + np.where(n < max_exact, n, large)


def _bucket_segments(lo, hi):
    rel = np.arange(lo, hi + 1, dtype=np.int32)
    b = _np_bucket(rel)
    change = np.nonzero(np.diff(b))[0]
    return int(b[0]), [(int(rel[i + 1]), int(b[i + 1])) for i in change]


def _bias_from_rel(rel, rb_ref, h, first_bucket, segs):
    val = jnp.full(rel.shape, rb_ref[first_bucket, h], F32)
    for lo, bk in segs:
        val = jnp.where(rel >= lo, rb_ref[bk, h], val)
    return val


def _bias_prompt_kernel(rb_ref, o_ref, *, T, first_bucket, segs):
    h = pl.program_id(0)
    kk = lax.broadcasted_iota(jnp.int32, (T, T), 0)
    qq = lax.broadcasted_iota(jnp.int32, (T, T), 1)
    o_ref[0, 0] = _bias_from_rel(kk - qq - T, rb_ref, h, first_bucket, segs) * LOG2E
    diag = _bias_from_rel(kk - qq, rb_ref, h, first_bucket, segs) * LOG2E
    o_ref[0, 1] = jnp.where(jnp.bitwise_and(kk, -CHUNK) <= qq, diag, NEG)


def _bias_prompt(rel_bias, T):
    H = rel_bias.shape[1]
    first_bucket, segs = _bucket_segments(-2 * T, T)
    assert all(lo > -T for lo, _ in segs), "bias must be constant beyond one tile"
    tiles = pl.pallas_call(
        functools.partial(_bias_prompt_kernel, T=T, first_bucket=first_bucket, segs=segs),
        grid=(H,),
        in_specs=[pl.BlockSpec(memory_space=pltpu.SMEM)],
        out_specs=pl.BlockSpec((1, 2, T, T), lambda h: (h, 0, 0, 0)),
        out_shape=jax.ShapeDtypeStruct((H, 2, T, T), F32),
        compiler_params=_cparams(("arbitrary",)),
        name="bias_prompt",
    )(rel_bias)
    return tiles, first_bucket


def _bias_sample_kernel(rb_ref, bp_ref, bn_ref, *, Ls, P, first_bucket, segs):
    h = pl.program_id(0)
    qq = jnp.bitwise_and(lax.broadcasted_iota(jnp.int32, (2 * Ls, P), 0), Ls - 1)
    kk = lax.broadcasted_iota(jnp.int32, (2 * Ls, P), 1)
    bp_ref[0] = _bias_from_rel(kk - P - qq, rb_ref, h, first_bucket, segs)
    qn = jnp.bitwise_and(lax.broadcasted_iota(jnp.int32, (2 * Ls, Ls), 0), Ls - 1)
    kn = lax.broadcasted_iota(jnp.int32, (2 * Ls, Ls), 1)
    bn_ref[0] = _bias_from_rel(kn - qn, rb_ref, h, first_bucket, segs)


def _bias_sample(rel_bias, Ls, P):
    H = rel_bias.shape[1]
    first_bucket, segs = _bucket_segments(-(P + Ls), Ls)
    return pl.pallas_call(
        functools.partial(_bias_sample_kernel, Ls=Ls, P=P, first_bucket=first_bucket, segs=segs),
        grid=(H,),
        in_specs=[pl.BlockSpec(memory_space=pltpu.SMEM)],
        out_specs=[pl.BlockSpec((1, 2 * Ls, P), lambda h: (h, 0, 0)),
                   pl.BlockSpec((1, 2 * Ls, Ls), lambda h: (h, 0, 0))],
        out_shape=[jax.ShapeDtypeStruct((H, 2 * Ls, P), F32),
                   jax.ShapeDtypeStruct((H, 2 * Ls, Ls), F32)],
        compiler_params=_cparams(("arbitrary",)),
        name="bias_sample",
    )(rel_bias)


Z_SLOT_OF_SEC = {SEC_HQ: Z_HQ, SEC_HI: Z_HI, SEC_HG: Z_HG, SEC_DQ: Z_DQ, SEC_MQ: Z_MQ,
                 SEC_G0: Z_G0, SEC_G0 + 1: Z_G0 + 1, SEC_G0 + 2: Z_G0 + 2}


def _in_proj_kernel(x_ref, g_ref, w_ref, z_ref, f_ref, k_ref, v_ref, *, head_layout):
    h = _rms(x_ref[...], g_ref[...]).astype(BF16)

    def store_heads(ref, acc):
        if head_layout:
            for hh in range(N_HEADS):
                ref[0, hh] = acc[:, hh * HEAD_W:(hh + 1) * HEAD_W]
        else:
            ref[...] = acc

    for sec in range(N_SEC):
        acc = jnp.dot(h, w_ref[:, sec * D_MODEL:(sec + 1) * D_MODEL], preferred_element_type=F32)
        if sec == SEC_HF:
            f_ref[...] = acc
        elif sec == SEC_DK:
            store_heads(k_ref, acc)
        elif sec == SEC_DV:
            store_heads(v_ref, acc)
        else:
            z_ref[Z_SLOT_OF_SEC[sec]] = acc.astype(BF16)


def _in_proj(x2d, g, w_bf, *, tm, rows_per_batch, head_layout):
    N = x2d.shape[0]
    if head_layout:
        B = N // rows_per_batch
        nlt = rows_per_batch // tm
        kv_shape = jax.ShapeDtypeStruct((B, N_HEADS, rows_per_batch, HEAD_W), F32)
        kv_spec = pl.BlockSpec((1, N_HEADS, tm, HEAD_W), lambda i: (i // nlt, 0, i % nlt, 0))
    else:
        kv_shape = jax.ShapeDtypeStruct((N, D_MODEL), F32)
        kv_spec = pl.BlockSpec((tm, D_MODEL), lambda i: (i, 0))
    return pl.pallas_call(
        functools.partial(_in_proj_kernel, head_layout=head_layout),
        grid=(N // tm,),
        in_specs=[pl.BlockSpec((tm, D_MODEL), lambda i: (i, 0)),
                  pl.BlockSpec((1, D_MODEL), lambda i: (0, 0)),
                  pl.BlockSpec(w_bf.shape, lambda i: (0, 0))],
        out_specs=[pl.BlockSpec((N_ZSLOT, tm, D_MODEL), lambda i: (0, i, 0)),
                   pl.BlockSpec((tm, D_MODEL), lambda i: (i, 0)),
                   kv_spec, kv_spec],
        out_shape=[jax.ShapeDtypeStruct((N_ZSLOT, N, D_MODEL), BF16),
                   jax.ShapeDtypeStruct((N, D_MODEL), F32),
                   kv_shape, kv_shape],
        compiler_params=_cparams(("arbitrary",)),
        name="in_proj",
    )(x2d, g, w_bf)


def _block_diag(a, b):
    za = jnp.zeros((a.shape[0], b.shape[1]), a.dtype)
    zb = jnp.zeros((b.shape[0], a.shape[1]), b.dtype)
    return jnp.concatenate([jnp.concatenate([a, za], axis=1), jnp.concatenate([zb, b], axis=1)], axis=0)


def _hgrn_kernel(q_ref, f_ref, i_ref, g_ref, lbl_ref, on_ref, s0_ref, ya_ref, sn_ref,
                 st_scr, sc_scr, o_scr, upd_scr, *, TL, C):
    lt = pl.program_id(1)
    nsub = C // SUB
    lbl = lbl_ref[...]
    e = jnp.exp(lbl - jnp.max(lbl, axis=0, keepdims=True))
    lb_all = e[0:1] / jnp.sum(e, axis=0, keepdims=True)

    @pl.when(lt == 0)
    def _():
        st_scr[...] = s0_ref[0]

    crow = lax.broadcasted_iota(jnp.int32, (C, D_MODEL), 0)
    width = max(nsub * C, HEAD_W)
    wrow = lax.broadcasted_iota(jnp.int32, (C, width), 0)
    wcol = lax.broadcasted_iota(jnp.int32, (C, width), 1)
    base = jnp.bitwise_and(wrow, -SUB) * (C // SUB)
    keep = (wcol >= base) & (wcol <= base + wrow)
    keep2 = jnp.concatenate([keep, keep], axis=1)
    onorm = on_ref[...]
    nt = (((1,), (1,)), ((), ()))

    def chunk(c, carry):
        r0 = pl.multiple_of(c * C, C)
        f = f_ref[0, pl.ds(r0, C), :]
        fg = lb_all + (1.0 - lb_all) * _sigmoid(f)
        logf = jnp.log(fg)
        kk = 1.0 - fg
        b = logf
        sh = 1
        while sh < C:
            if sh % 8 == 0:
                shifted = jnp.concatenate([jnp.zeros((sh, D_MODEL), F32), b[:C - sh]], axis=0)
            else:
                shifted = jnp.where(crow >= sh, pltpu.roll(b, sh, 0), 0.0)
            b = b + shifted
            sh *= 2
        r_blocks = [jnp.zeros((SUB, D_MODEL), F32)]
        for i in range(1, nsub):
            r_blocks.append(jnp.broadcast_to(b[SUB * i - 1:SUB * i, :], (SUB, D_MODEL)))
        r = r_blocks[0] if nsub == 1 else jnp.concatenate(r_blocks, axis=0)
        b_last = b[C - 1:C, :]
        q = q_ref[0, 0, pl.ds(r0, C), :].astype(F32)
        q_inter = (q * jnp.exp(b)).astype(BF16)
        q_intra = (q * jnp.exp(b - r)).astype(BF16)
        k_state = kk * jnp.exp(b_last - b)
        k_sub = []
        for i in range(nsub):
            n = SUB * (i + 1)
            ki = (kk[:n] * jnp.exp(r[SUB * i:SUB * i + 1, :] - b[:n])).astype(BF16)
            if n < C:
                ki = jnp.concatenate([ki, jnp.zeros((C - n, D_MODEL), BF16)], axis=0)
            k_sub.append(ki)
        st_decay = jnp.exp(b_last)
        g = g_ref[0, 0, pl.ds(r0, C), :].astype(F32)
        gate = g * _sigmoid(g)
        pairs = range(N_HEADS // 2)
        pcs = [slice(p * PAIR_W, (p + 1) * PAIR_W) for p in pairs]
        hcs = [[slice((2 * p + a) * HEAD_W, (2 * p + a + 1) * HEAD_W) for a in range(2)] for p in pairs]
        zrows = jnp.zeros((HEAD_W - C, HEAD_W), BF16)
        vpads = []
        for p in pairs:
            k4 = []
            for a in range(2):
                pieces = [k[:, hcs[p][a]] for k in k_sub]
                if nsub * C < width:
                    pieces.append(jnp.zeros((width - nsub * C, HEAD_W), BF16))
                k4.append(jnp.concatenate(pieces, axis=0))
            vpads.append([jnp.concatenate([i_ref[0, 0, pl.ds(r0, C), hcs[p][a]], zrows], axis=0)
                          for a in range(2)])
            sc_scr[p] = lax.dot_general(q_intra[:, pcs[p]], _block_diag(k4[0], k4[1]), nt,
                                        preferred_element_type=F32)
        dcols = []
        for p in pairs:
            kT = []
            for a in range(2):
                kext = jnp.concatenate([k_state[:, hcs[p][a]],
                                        jnp.broadcast_to(st_decay[:, hcs[p][a]], (8, HEAD_W)),
                                        jnp.zeros((HEAD_W - C - 8, HEAD_W), F32)], axis=0).T
                dcols.append(kext[:, C:C + 1])
                kT.append(kext.astype(BF16))
            upd_scr[p] = jnp.dot(jnp.concatenate(kT, axis=1), _block_diag(vpads[p][0], vpads[p][1]),
                                 preferred_element_type=F32)
        for p in pairs:
            w = jnp.where(keep2, sc_scr[p], 0.0)
            lhs = []
            rhs = []
            for a in range(2):
                t = w[:, a * width:a * width + HEAD_W]
                for i in range(1, width // HEAD_W):
                    t = t + w[:, a * width + i * HEAD_W:a * width + (i + 1) * HEAD_W]
                if nsub > 1 and C < HEAD_W:
                    assert HEAD_W == 2 * C
                    t = t + pltpu.roll(t, C, 1)
                lhs += [t.astype(BF16), q_inter[:, hcs[p][a]]]
                rhs.append(jnp.concatenate([vpads[p][a], st_scr[2 * p + a].astype(BF16)], axis=0))
            o_scr[p] = jnp.dot(jnp.concatenate(lhs, axis=1), _block_diag(rhs[0], rhs[1]),
                               preferred_element_type=F32)
        for p in pairs:
            for a in range(2):
                hh = 2 * p + a
                st_scr[hh] = st_scr[hh] * dcols[hh] + upd_scr[p, :, a * HEAD_W:(a + 1) * HEAD_W]
        for p in pairs:
            for a in range(2):
                o = o_scr[p, :, a * HEAD_W:(a + 1) * HEAD_W]
                ya_ref[0, pl.ds(r0, C), hcs[p][a]] = (_rms(o, onorm) * gate[:, hcs[p][a]]).astype(BF16)
        return carry

    lax.fori_loop(0, TL // C, chunk, 0)

    @pl.when(lt == pl.num_programs(1) - 1)
    def _():
        sn_ref[0] = st_scr[...]


def _hgrn(z4, f3, lb_logits, onorm, s0, *, L, tl):
    B = z4.shape[1]
    C = min(CHUNK, L)
    slot = lambda s: pl.BlockSpec((1, 1, tl, D_MODEL), lambda b, t: (s, b, t, 0))
    row = pl.BlockSpec((1, tl, D_MODEL), lambda b, t: (b, t, 0))
    state = pl.BlockSpec((1, N_HEADS, HEAD_W, HEAD_W), lambda b, t: (b, 0, 0, 0))
    return pl.pallas_call(
        functools.partial(_hgrn_kernel, TL=tl, C=C),
        grid=(B, L // tl),
        in_specs=[slot(Z_HQ), row, slot(Z_HI), slot(Z_HG),
                  pl.BlockSpec(lb_logits.shape, lambda b, t: (0, 0)),
                  pl.BlockSpec((1, HEAD_W), lambda b, t: (0, 0)),
                  state],
        out_specs=[row, state],
        out_shape=[jax.ShapeDtypeStruct((B, L, D_MODEL), BF16),
                   jax.ShapeDtypeStruct((B, N_HEADS, HEAD_W, HEAD_W), F32)],
        scratch_shapes=[pltpu.VMEM((N_HEADS, HEAD_W, HEAD_W), F32),
                        pltpu.VMEM((N_HEADS // 2, C, 2 * max((C // SUB) * C, HEAD_W)), F32),
                        pltpu.VMEM((N_HEADS // 2, C, PAIR_W), F32),
                        pltpu.VMEM((N_HEADS // 2, HEAD_W, PAIR_W), F32)],
        compiler_params=_cparams(("arbitrary", "arbitrary")),
        name="hgrn",
    )(z4, f3, z4, z4, lb_logits, onorm, s0)


def _diff_lambda(lp_ref):
    lp = lp_ref[...]
    a = jnp.sum(lp[0:1] * lp[1:2], axis=-1, keepdims=True)
    b = jnp.sum(lp[2:3] * lp[3:4], axis=-1, keepdims=True)
    return jnp.exp(a) - jnp.exp(b) + LAMBDA_INIT


def _split_maps(q):
    lane = lax.broadcasted_iota(jnp.int32, q.shape, 1)
    return jnp.where(lane < DIFF_DH, q, 0.0), jnp.where(lane >= DIFF_DH, q, 0.0)


def _diff_attn_prompt_kernel(rb_ref, q_ref, k_ref, v_ref, bias_ref, lp_ref, sub_ref, o_ref, kb, vT, acc, m_scr, s_scr,
                             *, T, nk, G, far_bucket):
    hg = pl.program_id(1)
    nt = (((1,), (1,)), ((), ()))

    ones = jnp.ones((ONES_ROWS, T), BF16)
    for g in range(G):
        for j in range(nk):
            kb[g, j] = (k_ref[0, g, j * T:(j + 1) * T, :] * (DIFF_SCALE * LOG2E)).astype(BF16)
            vT[g, j, :HEAD_W, :] = v_ref[0, g, j * T:(j + 1) * T, :].T.astype(BF16)
            vT[g, j, HEAD_W:, :] = ones
    lam = _diff_lambda(lp_ref)

    def tile(qs, j, bias_slot, first, slot):
        for g in range(G):
            kt = kb[g, j]
            for mp in range(2):
                s_scr[slot, 2 * g + mp] = lax.dot_general(kt, qs[g][mp], nt, preferred_element_type=F32)
        for g in range(G):
            vt = vT[g, j]
            if bias_slot is None:
                shift = rb_ref[far_bucket, hg * G + g] * LOG2E
            else:
                bias = bias_ref[g, bias_slot]
            for mp in range(2):
                idx = 2 * g + mp
                s = s_scr[slot, idx]
                if bias_slot is None:
                    tmax = jnp.max(s, axis=0, keepdims=True) + shift
                else:
                    s = s + bias
                    tmax = jnp.max(s, axis=0, keepdims=True)
                if first:
                    mn = tmax
                else:
                    m = m_scr[idx]
                    mn = jnp.maximum(m, tmax)
                p = jnp.exp2(s - (mn - shift)) if bias_slot is None else jnp.exp2(s - mn)
                m_scr[idx] = mn
                pv = jnp.dot(vt, p.astype(BF16), preferred_element_type=F32)
                acc[idx] = pv if first else jnp.exp2(m - mn) * acc[idx] + pv

    n_tiles = 0
    for qi in range(nk):
        rows = slice(qi * T, (qi + 1) * T)
        qs = [_split_maps(q_ref[0, 0, rows, g * HEAD_W:(g + 1) * HEAD_W]) for g in range(G)]
        order = [(j, None) for j in range(qi - 1)] + ([(qi - 1, 0)] if qi >= 1 else []) + [(qi, 1)]
        for t, (j, bias_slot) in enumerate(order):
            tile(qs, j, bias_slot, t == 0, n_tiles % 2)
            n_tiles += 1
        for g in range(G):
            a1 = acc[2 * g]
            a2 = acc[2 * g + 1]
            o = (a1[:HEAD_W] * (1.0 / a1[HEAD_W:HEAD_W + 1])
                 - a2[:HEAD_W] * (lam / a2[HEAD_W:HEAD_W + 1]))
            o = (o * lax.rsqrt(jnp.mean(o * o, axis=0, keepdims=True) + RMS_EPS)).T
            o_ref[0, rows, g * HEAD_W:(g + 1) * HEAD_W] = (o * sub_ref[...] * (1.0 - LAMBDA_INIT)).astype(BF16)


def _diff_attn_prompt(z4, k4, v4, rel_bias, bias, lp, subln, *, T, G, far_bucket):
    B, H, L, _ = k4.shape
    nk = L // T
    W = G * HEAD_W
    kv_spec = pl.BlockSpec((1, G, L, HEAD_W), lambda b, h: (b, h, 0, 0))
    return pl.pallas_call(
        functools.partial(_diff_attn_prompt_kernel, T=T, nk=nk, G=G, far_bucket=far_bucket),
        grid=(B, H // G),
        in_specs=[pl.BlockSpec(memory_space=pltpu.SMEM),
                  pl.BlockSpec((1, 1, L, W), lambda b, h: (Z_DQ, b, 0, h)),
                  kv_spec, kv_spec,
                  pl.BlockSpec((G, 2, T, T), lambda b, h: (h, 0, 0, 0)),
                  pl.BlockSpec(lp.shape, lambda b, h: (0, 0)),
                  pl.BlockSpec((1, HEAD_W), lambda b, h: (0, 0))],
        out_specs=pl.BlockSpec((1, L, W), lambda b, h: (b, 0, h)),
        out_shape=jax.ShapeDtypeStruct((B, L, D_MODEL), BF16),
        scratch_shapes=[pltpu.VMEM((G, nk, T, HEAD_W), BF16),
                        pltpu.VMEM((G, nk, HEAD_W + ONES_ROWS, T), BF16),
                        pltpu.VMEM((2 * G, HEAD_W + ONES_ROWS, T), F32),
                        pltpu.VMEM((2 * G, 1, T), F32),
                        pltpu.VMEM((2, 2 * G, T, T), F32)],
        compiler_params=_cparams(("arbitrary", "arbitrary")),
        name="diff_attn_prompt",
    )(rel_bias, z4, k4, v4, bias, lp, subln)


def _diff_attn_sample_kernel(q_ref, kn_ref, vn_ref, kc_ref, vc_ref, bp_ref, bn_ref, lp_ref, sub_ref, o_ref,
                             *, Ls, hb):
    lam = _diff_lambda(lp_ref)
    nt = (((1,), (1,)), ((), ()))
    for hh in range(hb):
        cols = slice(hh * HEAD_W, (hh + 1) * HEAD_W)
        qa, qb = _split_maps(q_ref[0, 0, :, cols].astype(F32) * DIFF_SCALE)
        q2 = jnp.concatenate([qa, qb], axis=0).astype(BF16)
        kp = kc_ref[0, hh].astype(BF16)
        vp = vc_ref[0, hh].astype(BF16)
        kn = kn_ref[0, :, cols].astype(BF16)
        vn = vn_ref[0, :, cols].astype(BF16)
        s = lax.dot_general(q2, kp, nt, preferred_element_type=F32) + bp_ref[hh]
        sn = lax.dot_general(q2, kn, nt, preferred_element_type=F32) + bn_ref[hh]
        m = jnp.maximum(jnp.max(s, axis=-1, keepdims=True), jnp.max(sn, axis=-1, keepdims=True))
        p = jnp.exp(s - m)
        pn = jnp.exp(sn - m)
        l = jnp.sum(p, axis=-1, keepdims=True) + jnp.sum(pn, axis=-1, keepdims=True)
        o2 = (jnp.dot(p.astype(BF16), vp, preferred_element_type=F32)
              + jnp.dot(pn.astype(BF16), vn, preferred_element_type=F32)) / l
        o = o2[:Ls] - lam * o2[Ls:]
        o_ref[0, :, cols] = (_rms(o, sub_ref[...]) * (1.0 - LAMBDA_INIT)).astype(BF16)


def _diff_attn_sample(z4, kn3, vn3, kc, vc, bias_p, bias_n, lp, subln, *, hb):
    B, H, P, _ = kc.shape
    Ls = kn3.shape[1]
    W = hb * HEAD_W
    cache_spec = pl.BlockSpec((1, hb, P, HEAD_W), lambda b, g: (b, g, 0, 0))
    new_spec = pl.BlockSpec((1, Ls, W), lambda b, g: (b, 0, g))
    return pl.pallas_call(
        functools.partial(_diff_attn_sample_kernel, Ls=Ls, hb=hb),
        grid=(B, H // hb),
        in_specs=[pl.BlockSpec((1, 1, Ls, W), lambda b, g: (Z_DQ, b, 0, g)),
                  new_spec, new_spec, cache_spec, cache_spec,
                  pl.BlockSpec((hb, 2 * Ls, P), lambda b, g: (g, 0, 0)),
                  pl.BlockSpec((hb, 2 * Ls, Ls), lambda b, g: (g, 0, 0)),
                  pl.BlockSpec(lp.shape, lambda b, g: (0, 0)),
                  pl.BlockSpec((1, HEAD_W), lambda b, g: (0, 0))],
        out_specs=pl.BlockSpec((1, Ls, W), lambda b, g: (b, 0, g)),
        out_shape=jax.ShapeDtypeStruct((B, Ls, D_MODEL), BF16),
        compiler_params=_cparams(("arbitrary", "arbitrary")),
        name="diff_attn_sample",
    )(z4, kn3, vn3, kc, vc, bias_p, bias_n, lp, subln)


def _mem_kv_kernel(m_ref, g_ref, w_ref, k_ref, v_ref):
    h = _rms(m_ref[0], g_ref[...]).astype(BF16)
    kv = jnp.dot(h, w_ref[...], preferred_element_type=F32)
    for hh in range(MEM_HEADS):
        k_ref[0, hh] = kv[:, hh * MEM_DH:(hh + 1) * MEM_DH]
        v_ref[0, hh] = kv[:, D_MODEL + hh * MEM_DH:D_MODEL + (hh + 1) * MEM_DH]


def _mem_kv(mem, g, w_bf):
    B, M, _ = mem.shape
    out = jax.ShapeDtypeStruct((B, MEM_HEADS, M, MEM_DH), F32)
    spec = pl.BlockSpec((1, MEM_HEADS, M, MEM_DH), lambda b: (b, 0, 0, 0))
    return pl.pallas_call(
        _mem_kv_kernel,
        grid=(B,),
        in_specs=[pl.BlockSpec((1, M, D_MODEL), lambda b: (b, 0, 0)),
                  pl.BlockSpec((1, D_MODEL), lambda b: (0, 0)),
                  pl.BlockSpec(w_bf.shape, lambda b: (0, 0))],
        out_specs=[spec, spec],
        out_shape=[out, out],
        compiler_params=_cparams(("arbitrary",)),
        name="mem_kv",
    )(mem, g, w_bf)


def _mem_attn_kernel(q_ref, k_ref, v_ref, o_ref, kb, vb, s_scr):
    nt = (((1,), (1,)), ((), ()))

    @pl.when(pl.program_id(1) == 0)
    def _():
        kb[...] = k_ref[0].astype(BF16)
        vb[...] = v_ref[0].astype(BF16)

    for hh in range(MEM_HEADS):
        cols = slice(hh * MEM_DH, (hh + 1) * MEM_DH)
        s_scr[hh] = lax.dot_general(q_ref[0, 0, :, cols], kb[hh], nt, preferred_element_type=F32)
    for hh in range(MEM_HEADS):
        cols = slice(hh * MEM_DH, (hh + 1) * MEM_DH)
        s = s_scr[hh] * (MEM_DH ** -0.5)
        p = jnp.exp(s - jnp.max(s, axis=-1, keepdims=True))
        p = p / jnp.sum(p, axis=-1, keepdims=True)
        o_ref[0, :, cols] = jnp.dot(p.astype(BF16), vb[hh], preferred_element_type=F32).astype(BF16)


def _mem_attn(z4, mk, mv, *, tl):
    _, B, L, _ = z4.shape
    M = mk.shape[2]
    kv_spec = pl.BlockSpec((1, MEM_HEADS, M, MEM_DH), lambda b, t: (b, 0, 0, 0))
    return pl.pallas_call(
        _mem_attn_kernel,
        grid=(B, L // tl),
        in_specs=[pl.BlockSpec((1, 1, tl, D_MODEL), lambda b, t: (Z_MQ, b, t, 0)), kv_spec, kv_spec],
        out_specs=pl.BlockSpec((1, tl, D_MODEL), lambda b, t: (b, t, 0)),
        out_shape=jax.ShapeDtypeStruct((B, L, D_MODEL), BF16),
        scratch_shapes=[pltpu.VMEM((MEM_HEADS, M, MEM_DH), BF16),
                        pltpu.VMEM((MEM_HEADS, M, MEM_DH), BF16),
                        pltpu.VMEM((MEM_HEADS, tl, M), F32)],
        compiler_params=_cparams(("arbitrary", "arbitrary")),
        name="mem_attn",
    )(z4, mk, mv)


def _merge_kernel(x_ref, ya_ref, yb_ref, yc_ref, g0_ref, g1_ref, g2_ref, wb_ref, wo_ref, nf_ref,
                  x1_ref, h2_ref):
    merged = None
    for n, (y_ref, gate_ref) in enumerate(((ya_ref, g0_ref), (yb_ref, g1_ref), (yc_ref, g2_ref))):
        proj = jnp.dot(y_ref[...], wb_ref[n], preferred_element_type=F32)
        term = proj * _sigmoid(gate_ref[0].astype(F32))
        merged = term if merged is None else merged + term
    x1 = x_ref[...] + jnp.dot(merged.astype(BF16), wo_ref[...], preferred_element_type=F32)
    x1_ref[...] = x1
    h2_ref[...] = _rms(x1, nf_ref[...]).astype(BF16)


def _merge(x2d, ya, yb, yc, z3, wb_bf, wo_bf, nf, *, tm):
    N = x2d.shape[0]
    row = pl.BlockSpec((tm, D_MODEL), lambda i: (i, 0))
    gate = lambda n: pl.BlockSpec((1, tm, D_MODEL), lambda i: (Z_G0 + n, i, 0))
    return pl.pallas_call(
        _merge_kernel,
        grid=(N // tm,),
        in_specs=[row, row, row, row, gate(0), gate(1), gate(2),
                  pl.BlockSpec(wb_bf.shape, lambda i: (0, 0, 0)),
                  pl.BlockSpec(wo_bf.shape, lambda i: (0, 0)),
                  pl.BlockSpec((1, D_MODEL), lambda i: (0, 0))],
        out_specs=[row, row],
        out_shape=[jax.ShapeDtypeStruct((N, D_MODEL), F32), jax.ShapeDtypeStruct((N, D_MODEL), BF16)],
        compiler_params=_cparams(("arbitrary",)),
        name="merge",
    )(x2d, ya, yb, yc, z3, z3, z3, wb_bf, wo_bf, nf)


def _ffn_kernel(h_ref, x_ref, cp_ref, wu_ref, cw_ref, cb_ref, wd_ref, nf_ref, y_ref, cn_ref, carry, u_scr, act_scr,
                *, tm, cw, d_ff, nb):
    lt = pl.program_id(1)
    rows_per = tm // nb

    if nb == 1:
        @pl.when(lt == 0)
        def _():
            carry[...] = cp_ref[0]

    h = h_ref[0]
    edge = 8
    erow = lax.broadcasted_iota(jnp.int32, (edge, cw), 0)
    for jc in range(d_ff // cw):
        slot = jc % 2
        bases = (jc * cw, d_ff + jc * cw)
        for hf, base in enumerate(bases):
            u_scr[slot, hf] = jnp.dot(h, wu_ref[:, base:base + cw], preferred_element_type=F32)
        halves = []
        for hf, base in enumerate(bases):
            cols = slice(base, base + cw)
            u = u_scr[slot, hf]
            w0, w1, w2, bb = cw_ref[0:1, cols], cw_ref[1:2, cols], cw_ref[2:3, cols], cb_ref[:, cols]
            u1 = pltpu.roll(u, 1, 0)
            u2 = pltpu.roll(u, 2, 0)
            c = bb + w0 * u2 + w1 * u1 + w2 * u
            pieces = []
            for s in range(nb):
                r0 = s * rows_per
                prev = carry if nb == 1 else cp_ref.at[s]
                p2 = prev[0:1, cols]
                p1 = prev[1:2, cols]
                u1e = jnp.where(erow == 0, p1, u1[r0:r0 + edge])
                u2e = jnp.where(erow == 0, p2, jnp.where(erow == 1, p1, u2[r0:r0 + edge]))
                pieces += [bb + w0 * u2e + w1 * u1e + w2 * u[r0:r0 + edge], c[r0 + edge:r0 + rows_per]]
                tail = u[r0 + rows_per - (FFN_CONV - 1):r0 + rows_per, :]
                if nb == 1:
                    carry[:, cols] = tail
                cn_ref[s, :, cols] = tail
            halves.append(jnp.concatenate(pieces, axis=0))
        gate, val = halves
        act_scr[:, jc * cw:(jc + 1) * cw] = (gate * _sigmoid(gate) * val).astype(BF16)
    y = x_ref[0] + jnp.dot(act_scr[...], wd_ref[...], preferred_element_type=F32)
    y_ref[0] = _rms(y, nf_ref[...])


def _ffn(h2, x1, conv_prev, wu_bf, conv_w, conv_b, wd_bf, nf, *, tm, nb=1, cw=256):
    G, R, _ = x1.shape
    d_ff = wd_bf.shape[0]
    assert nb == 1 or tm == R
    row = pl.BlockSpec((1, tm, D_MODEL), lambda b, t: (b, t, 0))
    state = pl.BlockSpec((nb, FFN_CONV - 1, 2 * d_ff), lambda b, t: (b, 0, 0))
    const = lambda a: pl.BlockSpec(a.shape, lambda b, t: (0,) * a.ndim)
    return pl.pallas_call(
        functools.partial(_ffn_kernel, tm=tm, cw=cw, d_ff=d_ff, nb=nb),
        grid=(G, R // tm),
        in_specs=[row, row, state, const(wu_bf), const(conv_w), const(conv_b), const(wd_bf), const(nf)],
        out_specs=[row, state],
        out_shape=[jax.ShapeDtypeStruct((G, R, D_MODEL), F32),
                   jax.ShapeDtypeStruct((G * nb, FFN_CONV - 1, 2 * d_ff), F32)],
        scratch_shapes=[pltpu.VMEM((FFN_CONV - 1, 2 * d_ff), F32),
                        pltpu.VMEM((2, 2, tm, cw), F32),
                        pltpu.VMEM((tm, d_ff), BF16)],
        compiler_params=_cparams(("arbitrary", "arbitrary")),
        name="ffn",
    )(h2, x1, conv_prev, wu_bf, conv_w, conv_b, wd_bf, nf)


def _layer(x, W, *, hgrn_s0, conv_prev, mem_k, mem_v, attn, tm_proj, tm_merge, tm_ffn, tl_mem, head_layout):
    B, L, _ = x.shape
    N = B * L
    x2d = x.reshape(N, D_MODEL)
    z3, f2d, k_new, v_new = _in_proj(x2d, W["norm_attn"], W["w_in"], tm=tm_proj, rows_per_batch=L,
                                     head_layout=head_layout)
    z4 = z3.reshape(N_ZSLOT, B, L, D_MODEL)
    ya, s_new = _hgrn(z4, f2d.reshape(B, L, D_MODEL), W["lb_logits"], W["hgrn_onorm"], hgrn_s0, L=L,
                      tl=min(L, 512))
    yb, k_new, v_new = attn(z4, k_new, v_new)
    yc = _mem_attn(z4, mem_k, mem_v, tl=tl_mem)
    x1, h2 = _merge(x2d, ya.reshape(N, D_MODEL), yb.reshape(N, D_MODEL), yc.reshape(N, D_MODEL), z3,
                    W["w_branch"], W["w_out"], W["norm_ffn"], tm=tm_merge)
    nb = B if L < tm_ffn else 1
    groups = (B // nb, nb * L, D_MODEL)
    y, conv_new = _ffn(h2.reshape(groups), x1.reshape(groups), conv_prev, W["w_up"], W["conv_w"], W["conv_b"],
                       W["w_down"], W["norm_final"], tm=nb * L if nb > 1 else tm_ffn, nb=nb)
    return y.reshape(B, L, D_MODEL), k_new, v_new, s_new, conv_new


def kernel(x_prompt, x_sample, mem_prompt, cache_diff_k, cache_diff_v, cache_mem_k, cache_mem_v, state_hgrn, state_ffn_conv, rel_bias, hgrn_lb_logits, norm_attn, w_in, hgrn_onorm, diff_lambda, diff_subln, mem_norm, w_mem_kv, w_branch, w_out, norm_ffn, w_up, conv_w, conv_b, w_down, norm_final):
    assert w_in.shape[0] == 1, "single-layer trunk"
    Bp, Lp, _ = x_prompt.shape
    Bs, Ls, _ = x_sample.shape
    P = cache_diff_k.shape[3]
    d_ff2 = w_up.shape[2]
    row = lambda a: a.reshape(1, -1)
    W = dict(norm_attn=row(norm_attn[0]), w_in=w_in[0].astype(BF16), lb_logits=hgrn_lb_logits,
             hgrn_onorm=row(hgrn_onorm[0]), w_branch=w_branch[0].astype(BF16), w_out=w_out[0].astype(BF16),
             norm_ffn=row(norm_ffn[0]), w_up=w_up[0].astype(BF16), conv_w=conv_w[0], conv_b=row(conv_b[0]),
             w_down=w_down[0].astype(BF16), norm_final=row(norm_final))
    lp = diff_lambda[0]
    subln = row(diff_subln[0])

    T = 512
    bias_p, far_bucket = _bias_prompt(rel_bias, T)
    mk, mv = _mem_kv(mem_prompt, row(mem_norm[0]), w_mem_kv[0].astype(BF16))

    def attn_prompt(z4, k4, v4):
        yb = _diff_attn_prompt(z4, k4, v4, rel_bias, bias_p, lp, subln, T=T, G=2, far_bucket=far_bucket)
        return yb, k4, v4

    yp, pk, pv, ps, pc = _layer(
        x_prompt, W, hgrn_s0=jnp.zeros((Bp, N_HEADS, HEAD_W, HEAD_W), F32),
        conv_prev=jnp.zeros((Bp, FFN_CONV - 1, d_ff2), F32), mem_k=mk, mem_v=mv, attn=attn_prompt,
        tm_proj=256, tm_merge=512, tm_ffn=512, tl_mem=512, head_layout=True)

    bias_sp, bias_sn = _bias_sample(rel_bias, Ls, P)

    def attn_sample(z4, k2d, v2d):
        yb = _diff_attn_sample(z4, k2d.reshape(Bs, Ls, D_MODEL), v2d.reshape(Bs, Ls, D_MODEL),
                               cache_diff_k[0], cache_diff_v[0], bias_sp, bias_sn, lp, subln, hb=8)
        heads = lambda a: jnp.transpose(a.reshape(Bs, Ls, N_HEADS, HEAD_W), (0, 2, 1, 3))
        return yb, heads(k2d), heads(v2d)

    ys, sk, sv, ss, sc = _layer(
        x_sample, W, hgrn_s0=state_hgrn[0], conv_prev=state_ffn_conv[0], mem_k=cache_mem_k[0],
        mem_v=cache_mem_v[0], attn=attn_sample,
        tm_proj=Bs * Ls, tm_merge=Bs * Ls, tm_ffn=512, tl_mem=Ls, head_layout=False)

    return (yp, ys, pk[None], pv[None], ps[None], pc[None], mk[None], mv[None],
            sk[None], sv[None], ss[None], sc[None])
```

```python
import functools
import math

import numpy as np
import jax
import jax.numpy as jnp
from jax import lax
from jax.experimental import pallas as pl
from jax.experimental.pallas import tpu as pltpu

F32 = jnp.float32
BF16 = jnp.bfloat16

D_MODEL = 1024
CHUNK = 64
SUB = 16
HEAD_W = 128
N_HEADS = D_MODEL // HEAD_W
PAIR_W = 2 * HEAD_W
DIFF_DH = 64
DIFF_SCALE = DIFF_DH ** -0.5
MEM_HEADS = 4
MEM_DH = D_MODEL // MEM_HEADS
REL_BUCKETS = 32
REL_MAX_DIST = 128
N_BRANCH = 3
FFN_CONV = 3
RMS_EPS = 1e-6
LAMBDA_INIT = 0.8 - 0.6 * math.exp(-0.3 * 0)
NEG = -1e30
LOG2E = math.log2(math.e)
ONES_ROWS = 16

SEC_HQ, SEC_HF, SEC_HI, SEC_HG, SEC_DQ, SEC_DK, SEC_DV, SEC_MQ, SEC_G0 = range(9)
N_SEC = 11
Z_HQ, Z_HI, Z_HG, Z_DQ, Z_MQ, Z_G0 = 0, 1, 2, 3, 4, 5
N_ZSLOT = 8

V7X_VMEM_LIMIT = 56 * 1024 * 1024


def _cparams(sem, vmem=V7X_VMEM_LIMIT):
    return pltpu.CompilerParams(dimension_semantics=sem, vmem_limit_bytes=vmem)


def _sigmoid(x):
    return 0.5 * jnp.tanh(0.5 * x) + 0.5


def _rms(x, g):
    ms = jnp.mean(x * x, axis=-1, keepdims=True)
    return x * lax.rsqrt(ms + RMS_EPS) * g


def _np_bucket(rel):
    nb = REL_BUCKETS // 2
    ret = np.where(rel > 0, nb, 0)
    n = np.abs(rel)
    max_exact = nb // 2
    large = max_exact + (np.log(np.maximum(n, 1).astype(np.float32) / max_exact)
                         / math.log(REL_MAX_DIST / max_exact) * (nb - max_exact)).astype(np.int32)
    large = np.minimum(large, nb - 1)
    return ret + np.where(n < max_exact, n, large)


def _bucket_segments(lo, hi):
    rel = np.arange(lo, hi + 1, dtype=np.int32)
    b = _np_bucket(rel)
    change = np.nonzero(np.diff(b))[0]
    return int(b[0]), [(int(rel[i + 1]), int(b[i + 1])) for i in change]


def _bias_from_rel(rel, rb_ref, h, first_bucket, segs):
    val = jnp.full(rel.shape, rb_ref[first_bucket, h], F32)
    for lo, bk in segs:
        val = jnp.where(rel >= lo, rb_ref[bk, h], val)
    return val


def _bias_prompt_kernel(rb_ref, o_ref, *, T, first_bucket, segs):
    h = pl.program_id(0)
    kk = lax.broadcasted_iota(jnp.int32, (T, T), 0)
    qq = lax.broadcasted_iota(jnp.int32, (T, T), 1)
    o_ref[0, 0] = _bias_from_rel(kk - qq - T, rb_ref, h, first_bucket, segs) * LOG2E
    diag = _bias_from_rel(kk - qq, rb_ref, h, first_bucket, segs) * LOG2E
    o_ref[0, 1] = jnp.where(jnp.bitwise_and(kk, -CHUNK) <= qq, diag, NEG)


def _bias_prompt(rel_bias, T):
    H = rel_bias.shape[1]
    first_bucket, segs = _bucket_segments(-2 * T, T)
    assert all(lo > -T for lo, _ in segs), "bias must be constant beyond one tile"
    tiles = pl.pallas_call(
        functools.partial(_bias_prompt_kernel, T=T, first_bucket=first_bucket, segs=segs),
        grid=(H,),
        in_specs=[pl.BlockSpec(memory_space=pltpu.SMEM)],
        out_specs=pl.BlockSpec((1, 2, T, T), lambda h: (h, 0, 0, 0)),
        out_shape=jax.ShapeDtypeStruct((H, 2, T, T), F32),
        compiler_params=_cparams(("arbitrary",)),
        name="bias_prompt",
    )(rel_bias)
    return tiles, first_bucket


def _bias_sample_kernel(rb_ref, bp_ref, bn_ref, *, Ls, P, first_bucket, segs):
    h = pl.program_id(0)
    qq = jnp.bitwise_and(lax.broadcasted_iota(jnp.int32, (2 * Ls, P), 0), Ls - 1)
    kk = lax.broadcasted_iota(jnp.int32, (2 * Ls, P), 1)
    bp_ref[0] = _bias_from_rel(kk - P - qq, rb_ref, h, first_bucket, segs)
    qn = jnp.bitwise_and(lax.broadcasted_iota(jnp.int32, (2 * Ls, Ls), 0), Ls - 1)
    kn = lax.broadcasted_iota(jnp.int32, (2 * Ls, Ls), 1)
    bn_ref[0] = _bias_from_rel(kn - qn, rb_ref, h, first_bucket, segs)


def _bias_sample(rel_bias, Ls, P):
    H = rel_bias.shape[1]
    first_bucket, segs = _bucket_segments(-(P + Ls), Ls)
    return pl.pallas_call(
        functools.partial(_bias_sample_kernel, Ls=Ls, P=P, first_bucket=first_bucket, segs=segs),
        grid=(H,),
        in_specs=[pl.BlockSpec(memory_space=pltpu.SMEM)],
        out_specs=[pl.BlockSpec((1, 2 * Ls, P), lambda h: (h, 0, 0)),
                   pl.BlockSpec((1, 2 * Ls, Ls), lambda h: (h, 0, 0))],
        out_shape=[jax.ShapeDtypeStruct((H, 2 * Ls, P), F32),
                   jax.ShapeDtypeStruct((H, 2 * Ls, Ls), F32)],
        compiler_params=_cparams(("arbitrary",)),
        name="bias_sample",
    )(rel_bias)


Z_SLOT_OF_SEC = {SEC_HQ: Z_HQ, SEC_HI: Z_HI, SEC_HG: Z_HG, SEC_DQ: Z_DQ, SEC_MQ: Z_MQ,
                 SEC_G0: Z_G0, SEC_G0 + 1: Z_G0 + 1, SEC_G0 + 2: Z_G0 + 2}


def _in_proj_kernel(x_ref, g_ref, w_ref, z_ref, f_ref, k_ref, v_ref, *, head_layout):
    h = _rms(x_ref[...], g_ref[...]).astype(BF16)

    def store_heads(ref, acc):
        if head_layout:
            for hh in range(N_HEADS):
                ref[0, hh] = acc[:, hh * HEAD_W:(hh + 1) * HEAD_W]
        else:
            ref[...] = acc

    for sec in range(N_SEC):
        acc = jnp.dot(h, w_ref[:, sec * D_MODEL:(sec + 1) * D_MODEL], preferred_element_type=F32)
        if sec == SEC_HF:
            f_ref[...] = acc
        elif sec == SEC_DK:
            store_heads(k_ref, acc)
        elif sec == SEC_DV:
            store_heads(v_ref, acc)
        else:
            z_ref[Z_SLOT_OF_SEC[sec]] = acc.astype(BF16)


def _in_proj(x2d, g, w_bf, *, tm, rows_per_batch, head_layout):
    N = x2d.shape[0]
    if head_layout:
        B = N // rows_per_batch
        nlt = rows_per_batch // tm
        kv_shape = jax.ShapeDtypeStruct((B, N_HEADS, rows_per_batch, HEAD_W), F32)
        kv_spec = pl.BlockSpec((1, N_HEADS, tm, HEAD_W), lambda i: (i // nlt, 0, i % nlt, 0))
    else:
        kv_shape = jax.ShapeDtypeStruct((N, D_MODEL), F32)
        kv_spec = pl.BlockSpec((tm, D_MODEL), lambda i: (i, 0))
    return pl.pallas_call(
        functools.partial(_in_proj_kernel, head_layout=head_layout),
        grid=(N // tm,),
        in_specs=[pl.BlockSpec((tm, D_MODEL), lambda i: (i, 0)),
                  pl.BlockSpec((1, D_MODEL), lambda i: (0, 0)),
                  pl.BlockSpec(w_bf.shape, lambda i: (0, 0))],
        out_specs=[pl.BlockSpec((N_ZSLOT, tm, D_MODEL), lambda i: (0, i, 0)),
                   pl.BlockSpec((tm, D_MODEL), lambda i: (i, 0)),
                   kv_spec, kv_spec],
        out_shape=[jax.ShapeDtypeStruct((N_ZSLOT, N, D_MODEL), BF16),
                   jax.ShapeDtypeStruct((N, D_MODEL), F32),
                   kv_shape, kv_shape],
        compiler_params=_cparams(("arbitrary",)),
        name="in_proj",
    )(x2d, g, w_bf)


def _block_diag(a, b):
    za = jnp.zeros((a.shape[0], b.shape[1]), a.dtype)
    zb = jnp.zeros((b.shape[0], a.shape[1]), b.dtype)
    return jnp.concatenate([jnp.concatenate([a, za], axis=1), jnp.concatenate([zb, b], axis=1)], axis=0)


def _hgrn_kernel(q_ref, f_ref, i_ref, g_ref, lbl_ref, on_ref, s0_ref, ya_ref, sn_ref,
                 st_scr, sc_scr, o_scr, upd_scr, *, TL, C):
    lt = pl.program_id(1)
    nsub = C // SUB
    lbl = lbl_ref[...]
    e = jnp.exp(lbl - jnp.max(lbl, axis=0, keepdims=True))
    lb_all = e[0:1] / jnp.sum(e, axis=0, keepdims=True)

    @pl.when(lt == 0)
    def _():
        st_scr[...] = s0_ref[0]

    crow = lax.broadcasted_iota(jnp.int32, (C, D_MODEL), 0)
    width = max(nsub * C, HEAD_W)
    wrow = lax.broadcasted_iota(jnp.int32, (C, width), 0)
    wcol = lax.broadcasted_iota(jnp.int32, (C, width), 1)
    base = jnp.bitwise_and(wrow, -SUB) * (C // SUB)
    keep = (wcol >= base) & (wcol <= base + wrow)
    keep2 = jnp.concatenate([keep, keep], axis=1)
    onorm = on_ref[...]
    nt = (((1,), (1,)), ((), ()))

    def chunk(c, carry):
        r0 = pl.multiple_of(c * C, C)
        f = f_ref[0, pl.ds(r0, C), :]
        fg = lb_all + (1.0 - lb_all) * _sigmoid(f)
        logf = jnp.log(fg)
        kk = 1.0 - fg
        b = logf
        sh = 1
        while sh < C:
            if sh % 8 == 0:
                shifted = jnp.concatenate([jnp.zeros((sh, D_MODEL), F32), b[:C - sh]], axis=0)
            else:
                shifted = jnp.where(crow >= sh, pltpu.roll(b, sh, 0), 0.0)
            b = b + shifted
            sh *= 2
        r_blocks = [jnp.zeros((SUB, D_MODEL), F32)]
        for i in range(1, nsub):
            r_blocks.append(jnp.broadcast_to(b[SUB * i - 1:SUB * i, :], (SUB, D_MODEL)))
        r = r_blocks[0] if nsub == 1 else jnp.concatenate(r_blocks, axis=0)
        b_last = b[C - 1:C, :]
        q = q_ref[0, 0, pl.ds(r0, C), :].astype(F32)
        q_inter = (q * jnp.exp(b)).astype(BF16)
        q_intra = (q * jnp.exp(b - r)).astype(BF16)
        k_state = kk * jnp.exp(b_last - b)
        k_sub = []
        for i in range(nsub):
            n = SUB * (i + 1)
            ki = (kk[:n] * jnp.exp(r[SUB * i:SUB * i + 1, :] - b[:n])).astype(BF16)
            if n < C:
                ki = jnp.concatenate([ki, jnp.zeros((C - n, D_MODEL), BF16)], axis=0)
            k_sub.append(ki)
        st_decay = jnp.exp(b_last)
        g = g_ref[0, 0, pl.ds(r0, C), :].astype(F32)
        gate = g * _sigmoid(g)
        pairs = range(N_HEADS // 2)
        pcs = [slice(p * PAIR_W, (p + 1) * PAIR_W) for p in pairs]
        hcs = [[slice((2 * p + a) * HEAD_W, (2 * p + a + 1) * HEAD_W) for a in range(2)] for p in pairs]
        zrows = jnp.zeros((HEAD_W - C, HEAD_W), BF16)
        vpads = []
        for p in pairs:
            k4 = []
            for a in range(2):
                pieces = [k[:, hcs[p][a]] for k in k_sub]
                if nsub * C < width:
                    pieces.append(jnp.zeros((width - nsub * C, HEAD_W), BF16))
                k4.append(jnp.concatenate(pieces, axis=0))
            vpads.append([jnp.concatenate([i_ref[0, 0, pl.ds(r0, C), hcs[p][a]], zrows], axis=0)
                          for a in range(2)])
            sc_scr[p] = lax.dot_general(q_intra[:, pcs[p]], _block_diag(k4[0], k4[1]), nt,
                                        preferred_element_type=F32)
        dcols = []
        for p in pairs:
            kT = []
            for a in range(2):
                kext = jnp.concatenate([k_state[:, hcs[p][a]],
                                        jnp.broadcast_to(st_decay[:, hcs[p][a]], (8, HEAD_W)),
                                        jnp.zeros((HEAD_W - C - 8, HEAD_W), F32)], axis=0).T
                dcols.append(kext[:, C:C + 1])
                kT.append(kext.astype(BF16))
            upd_scr[p] = jnp.dot(jnp.concatenate(kT, axis=1), _block_diag(vpads[p][0], vpads[p][1]),
                                 preferred_element_type=F32)
        for p in pairs:
            w = jnp.where(keep2, sc_scr[p], 0.0)
            lhs = []
            rhs = []
            for a in range(2):
                t = w[:, a * width:a * width + HEAD_W]
                for i in range(1, width // HEAD_W):
                    t = t + w[:, a * width + i * HEAD_W:a * width + (i + 1) * HEAD_W]
                if nsub > 1 and C < HEAD_W:
                    assert HEAD_W == 2 * C
                    t = t + pltpu.roll(t, C, 1)
                lhs += [t.astype(BF16), q_inter[:, hcs[p][a]]]
                rhs.append(jnp.concatenate([vpads[p][a], st_scr[2 * p + a].astype(BF16)], axis=0))
            o_scr[p] = jnp.dot(jnp.concatenate(lhs, axis=1), _block_diag(rhs[0], rhs[1]),
                               preferred_element_type=F32)
        for p in pairs:
            for a in range(2):
                hh = 2 * p + a
                st_scr[hh] = st_scr[hh] * dcols[hh] + upd_scr[p, :, a * HEAD_W:(a + 1) * HEAD_W]
        for p in pairs:
            for a in range(2):
                o = o_scr[p, :, a * HEAD_W:(a + 1) * HEAD_W]
                ya_ref[0, pl.ds(r0, C), hcs[p][a]] = (_rms(o, onorm) * gate[:, hcs[p][a]]).astype(BF16)
        return carry

    lax.fori_loop(0, TL // C, chunk, 0)

    @pl.when(lt == pl.num_programs(1) - 1)
    def _():
        sn_ref[0] = st_scr[...]


def _hgrn(z4, f3, lb_logits, onorm, s0, *, L, tl):
    B = z4.shape[1]
    C = min(CHUNK, L)
    slot = lambda s: pl.BlockSpec((1, 1, tl, D_MODEL), lambda b, t: (s, b, t, 0))
    row = pl.BlockSpec((1, tl, D_MODEL), lambda b, t: (b, t, 0))
    state = pl.BlockSpec((1, N_HEADS, HEAD_W, HEAD_W), lambda b, t: (b, 0, 0, 0))
    return pl.pallas_call(
        functools.partial(_hgrn_kernel, TL=tl, C=C),
        grid=(B, L // tl),
        in_specs=[slot(Z_HQ), row, slot(Z_HI), slot(Z_HG),
                  pl.BlockSpec(lb_logits.shape, lambda b, t: (0, 0)),
                  pl.BlockSpec((1, HEAD_W), lambda b, t: (0, 0)),
                  state],
        out_specs=[row, state],
        out_shape=[jax.ShapeDtypeStruct((B, L, D_MODEL), BF16),
                   jax.ShapeDtypeStruct((B, N_HEADS, HEAD_W, HEAD_W), F32)],
        scratch_shapes=[pltpu.VMEM((N_HEADS, HEAD_W, HEAD_W), F32),
                        pltpu.VMEM((N_HEADS // 2, C, 2 * max((C // SUB) * C, HEAD_W)), F32),
                        pltpu.VMEM((N_HEADS // 2, C, PAIR_W), F32),
                        pltpu.VMEM((N_HEADS // 2, HEAD_W, PAIR_W), F32)],
        compiler_params=_cparams(("arbitrary", "arbitrary")),
        name="hgrn",
    )(z4, f3, z4, z4, lb_logits, onorm, s0)


def _diff_lambda(lp_ref):
    lp = lp_ref[...]
    a = jnp.sum(lp[0:1] * lp[1:2], axis=-1, keepdims=True)
    b = jnp.sum(lp[2:3] * lp[3:4], axis=-1, keepdims=True)
    return jnp.exp(a) - jnp.exp(b) + LAMBDA_INIT


def _split_maps(q):
    lane = lax.broadcasted_iota(jnp.int32, q.shape, 1)
    return jnp.where(lane < DIFF_DH, q, 0.0), jnp.where(lane >= DIFF_DH, q, 0.0)


def _diff_attn_prompt_kernel(rb_ref, q_ref, k_ref, v_ref, bias_ref, lp_ref, sub_ref, o_ref, kb, vT, acc, m_scr, s_scr, q_scr,
                             *, T, nk, G, far_bucket):
    hg = pl.program_id(1)
    nt = (((1,), (1,)), ((), ()))

    ones = jnp.ones((ONES_ROWS, T), BF16)
    for g in range(G):
        for j in range(nk):
            kb[g, j] = (k_ref[0, g, j * T:(j + 1) * T, :] * (DIFF_SCALE * LOG2E)).astype(BF16)
            vT[g, j, :HEAD_W, :] = v_ref[0, g, j * T:(j + 1) * T, :].T.astype(BF16)
            vT[g, j, HEAD_W:, :] = ones
        qa, qb = _split_maps(q_ref[0, 0, :, g * HEAD_W:(g + 1) * HEAD_W])
        q_scr[2 * g] = qa
        q_scr[2 * g + 1] = qb
    lam = _diff_lambda(lp_ref)

    def tile(qi, j, keys, queries, bias_slot, first, slot):
        kr = slice(*keys)
        qc = slice(*queries)
        nkk, nqq = keys[1] - keys[0], queries[1] - queries[0]
        qrows = slice(qi * T + queries[0], qi * T + queries[1])
        for g in range(G):
            kt = kb[g, j, kr, :]
            for mp in range(2):
                s_scr[slot, 2 * g + mp, :nkk, :nqq] = lax.dot_general(
                    kt, q_scr[2 * g + mp, qrows, :], nt, preferred_element_type=F32)
        for g in range(G):
            vt = vT[g, j, :, kr]
            if bias_slot is None:
                shift = rb_ref[far_bucket, hg * G + g] * LOG2E
            else:
                bias = bias_ref[g, bias_slot, kr, qc]
            for mp in range(2):
                idx = 2 * g + mp
                s = s_scr[slot, idx, :nkk, :nqq]
                if bias_slot is None:
                    tmax = jnp.max(s, axis=0, keepdims=True) + shift
                else:
                    s = s + bias
                    tmax = jnp.max(s, axis=0, keepdims=True)
                if first:
                    mn = tmax
                else:
                    m = m_scr[idx, :, qc]
                    mn = jnp.maximum(m, tmax)
                p = jnp.exp2(s - (mn - shift)) if bias_slot is None else jnp.exp2(s - mn)
                m_scr[idx, :, qc] = mn
                pv = jnp.dot(vt, p.astype(BF16), preferred_element_type=F32)
                acc[idx, :, qc] = pv if first else jnp.exp2(m - mn) * acc[idx, :, qc] + pv

    n_tiles = 0
    half = T // 2
    assert half % CHUNK == 0
    for qi in range(nk):
        rows = slice(qi * T, (qi + 1) * T)
        order = [(j, (0, T), (0, T), None) for j in range(qi - 1)]
        if qi >= 1:
            order.append((qi - 1, (0, T), (0, T), 0))
        order += [(qi, (0, half), (0, T), 1), (qi, (half, T), (half, T), 1)]
        for t, (j, keys, queries, bias_slot) in enumerate(order):
            tile(qi, j, keys, queries, bias_slot, t == 0, n_tiles % 2)
            n_tiles += 1
        for g in range(G):
            a1 = acc[2 * g]
            a2 = acc[2 * g + 1]
            o = (a1[:HEAD_W] * (1.0 / a1[HEAD_W:HEAD_W + 1])
                 - a2[:HEAD_W] * (lam / a2[HEAD_W:HEAD_W + 1]))
            o = (o * lax.rsqrt(jnp.mean(o * o, axis=0, keepdims=True) + RMS_EPS)).T
            o_ref[0, rows, g * HEAD_W:(g + 1) * HEAD_W] = (o * sub_ref[...] * (1.0 - LAMBDA_INIT)).astype(BF16)


def _diff_attn_prompt(z4, k4, v4, rel_bias, bias, lp, subln, *, T, G, far_bucket):
    B, H, L, _ = k4.shape
    nk = L // T
    W = G * HEAD_W
    kv_spec = pl.BlockSpec((1, G, L, HEAD_W), lambda b, h: (b, h, 0, 0))
    return pl.pallas_call(
        functools.partial(_diff_attn_prompt_kernel, T=T, nk=nk, G=G, far_bucket=far_bucket),
        grid=(B, H // G),
        in_specs=[pl.BlockSpec(memory_space=pltpu.SMEM),
                  pl.BlockSpec((1, 1, L, W), lambda b, h: (Z_DQ, b, 0, h)),
                  kv_spec, kv_spec,
                  pl.BlockSpec((G, 2, T, T), lambda b, h: (h, 0, 0, 0)),
                  pl.BlockSpec(lp.shape, lambda b, h: (0, 0)),
                  pl.BlockSpec((1, HEAD_W), lambda b, h: (0, 0))],
        out_specs=pl.BlockSpec((1, L, W), lambda b, h: (b, 0, h)),
        out_shape=jax.ShapeDtypeStruct((B, L, D_MODEL), BF16),
        scratch_shapes=[pltpu.VMEM((G, nk, T, HEAD_W), BF16),
                        pltpu.VMEM((G, nk, HEAD_W + ONES_ROWS, T), BF16),
                        pltpu.VMEM((2 * G, HEAD_W + ONES_ROWS, T), F32),
                        pltpu.VMEM((2 * G, 1, T), F32),
                        pltpu.VMEM((2, 2 * G, T, T), F32),
                        pltpu.VMEM((2 * G, L, HEAD_W), BF16)],
        compiler_params=_cparams(("arbitrary", "arbitrary")),
        name="diff_attn_prompt",
    )(rel_bias, z4, k4, v4, bias, lp, subln)


def _diff_attn_sample_kernel(q_ref, kn_ref, vn_ref, kc_ref, vc_ref, bp_ref, bn_ref, lp_ref, sub_ref, o_ref,
                             *, Ls, hb):
    lam = _diff_lambda(lp_ref)
    nt = (((1,), (1,)), ((), ()))
    for hh in range(hb):
        cols = slice(hh * HEAD_W, (hh + 1) * HEAD_W)
        qa, qb = _split_maps(q_ref[0, 0, :, cols].astype(F32) * DIFF_SCALE)
        q2 = jnp.concatenate([qa, qb], axis=0).astype(BF16)
        kp = kc_ref[0, hh].astype(BF16)
        vp = vc_ref[0, hh].astype(BF16)
        kn = kn_ref[0, :, cols].astype(BF16)
        vn = vn_ref[0, :, cols].astype(BF16)
        s = lax.dot_general(q2, kp, nt, preferred_element_type=F32) + bp_ref[hh]
        sn = lax.dot_general(q2, kn, nt, preferred_element_type=F32) + bn_ref[hh]
        m = jnp.maximum(jnp.max(s, axis=-1, keepdims=True), jnp.max(sn, axis=-1, keepdims=True))
        p = jnp.exp(s - m)
        pn = jnp.exp(sn - m)
        l = jnp.sum(p, axis=-1, keepdims=True) + jnp.sum(pn, axis=-1, keepdims=True)
        o2 = (jnp.dot(p.astype(BF16), vp, preferred_element_type=F32)
              + jnp.dot(pn.astype(BF16), vn, preferred_element_type=F32)) / l
        o = o2[:Ls] - lam * o2[Ls:]
        o_ref[0, :, cols] = (_rms(o, sub_ref[...]) * (1.0 - LAMBDA_INIT)).astype(BF16)


def _diff_attn_sample(z4, kn3, vn3, kc, vc, bias_p, bias_n, lp, subln, *, hb):
    B, H, P, _ = kc.shape
    Ls = kn3.shape[1]
    W = hb * HEAD_W
    cache_spec = pl.BlockSpec((1, hb, P, HEAD_W), lambda b, g: (b, g, 0, 0))
    new_spec = pl.BlockSpec((1, Ls, W), lambda b, g: (b, 0, g))
    return pl.pallas_call(
        functools.partial(_diff_attn_sample_kernel, Ls=Ls, hb=hb),
        grid=(B, H // hb),
        in_specs=[pl.BlockSpec((1, 1, Ls, W), lambda b, g: (Z_DQ, b, 0, g)),
                  new_spec, new_spec, cache_spec, cache_spec,
                  pl.BlockSpec((hb, 2 * Ls, P), lambda b, g: (g, 0, 0)),
                  pl.BlockSpec((hb, 2 * Ls, Ls), lambda b, g: (g, 0, 0)),
                  pl.BlockSpec(lp.shape, lambda b, g: (0, 0)),
                  pl.BlockSpec((1, HEAD_W), lambda b, g: (0, 0))],
        out_specs=pl.BlockSpec((1, Ls, W), lambda b, g: (b, 0, g)),
        out_shape=jax.ShapeDtypeStruct((B, Ls, D_MODEL), BF16),
        compiler_params=_cparams(("arbitrary", "arbitrary")),
        name="diff_attn_sample",
    )(z4, kn3, vn3, kc, vc, bias_p, bias_n, lp, subln)


def _mem_kv_kernel(m_ref, g_ref, w_ref, k_ref, v_ref):
    h = _rms(m_ref[0], g_ref[...]).astype(BF16)
    kv = jnp.dot(h, w_ref[...], preferred_element_type=F32)
    for hh in range(MEM_HEADS):
        k_ref[0, hh] = kv[:, hh * MEM_DH:(hh + 1) * MEM_DH]
        v_ref[0, hh] = kv[:, D_MODEL + hh * MEM_DH:D_MODEL + (hh + 1) * MEM_DH]


def _mem_kv(mem, g, w_bf):
    B, M, _ = mem.shape
    out = jax.ShapeDtypeStruct((B, MEM_HEADS, M, MEM_DH), F32)
    spec = pl.BlockSpec((1, MEM_HEADS, M, MEM_DH), lambda b: (b, 0, 0, 0))
    return pl.pallas_call(
        _mem_kv_kernel,
        grid=(B,),
        in_specs=[pl.BlockSpec((1, M, D_MODEL), lambda b: (b, 0, 0)),
                  pl.BlockSpec((1, D_MODEL), lambda b: (0, 0)),
                  pl.BlockSpec(w_bf.shape, lambda b: (0, 0))],
        out_specs=[spec, spec],
        out_shape=[out, out],
        compiler_params=_cparams(("arbitrary",)),
        name="mem_kv",
    )(mem, g, w_bf)


def _mem_attn_kernel(q_ref, k_ref, v_ref, o_ref, kb, vb, s_scr):
    nt = (((1,), (1,)), ((), ()))

    @pl.when(pl.program_id(1) == 0)
    def _():
        kb[...] = k_ref[0].astype(BF16)
        vb[...] = v_ref[0].astype(BF16)

    for hh in range(MEM_HEADS):
        cols = slice(hh * MEM_DH, (hh + 1) * MEM_DH)
        s_scr[hh] = lax.dot_general(q_ref[0, 0, :, cols], kb[hh], nt, preferred_element_type=F32)
    for hh in range(MEM_HEADS):
        cols = slice(hh * MEM_DH, (hh + 1) * MEM_DH)
        s = s_scr[hh] * (MEM_DH ** -0.5)
        p = jnp.exp(s - jnp.max(s, axis=-1, keepdims=True))
        p = p / jnp.sum(p, axis=-1, keepdims=True)
        o_ref[0, :, cols] = jnp.dot(p.astype(BF16), vb[hh], preferred_element_type=F32).astype(BF16)


def _mem_attn(z4, mk, mv, *, tl):
    _, B, L, _ = z4.shape
    M = mk.shape[2]
    kv_spec = pl.BlockSpec((1, MEM_HEADS, M, MEM_DH), lambda b, t: (b, 0, 0, 0))
    return pl.pallas_call(
        _mem_attn_kernel,
        grid=(B, L // tl),
        in_specs=[pl.BlockSpec((1, 1, tl, D_MODEL), lambda b, t: (Z_MQ, b, t, 0)), kv_spec, kv_spec],
        out_specs=pl.BlockSpec((1, tl, D_MODEL), lambda b, t: (b, t, 0)),
        out_shape=jax.ShapeDtypeStruct((B, L, D_MODEL), BF16),
        scratch_shapes=[pltpu.VMEM((MEM_HEADS, M, MEM_DH), BF16),
                        pltpu.VMEM((MEM_HEADS, M, MEM_DH), BF16),
                        pltpu.VMEM((MEM_HEADS, tl, M), F32)],
        compiler_params=_cparams(("arbitrary", "arbitrary")),
        name="mem_attn",
    )(z4, mk, mv)


def _merge_kernel(x_ref, ya_ref, yb_ref, yc_ref, g0_ref, g1_ref, g2_ref, wb_ref, wo_ref, nf_ref,
                  x1_ref, h2_ref):
    merged = None
    for n, (y_ref, gate_ref) in enumerate(((ya_ref, g0_ref), (yb_ref, g1_ref), (yc_ref, g2_ref))):
        proj = jnp.dot(y_ref[...], wb_ref[n], preferred_element_type=F32)
        term = proj * _sigmoid(gate_ref[0].astype(F32))
        merged = term if merged is None else merged + term
    x1 = x_ref[...] + jnp.dot(merged.astype(BF16), wo_ref[...], preferred_element_type=F32)
    x1_ref[...] = x1
    h2_ref[...] = _rms(x1, nf_ref[...]).astype(BF16)


def _merge(x2d, ya, yb, yc, z3, wb_bf, wo_bf, nf, *, tm):
    N = x2d.shape[0]
    row = pl.BlockSpec((tm, D_MODEL), lambda i: (i, 0))
    gate = lambda n: pl.BlockSpec((1, tm, D_MODEL), lambda i: (Z_G0 + n, i, 0))
    return pl.pallas_call(
        _merge_kernel,
        grid=(N // tm,),
        in_specs=[row, row, row, row, gate(0), gate(1), gate(2),
                  pl.BlockSpec(wb_bf.shape, lambda i: (0, 0, 0)),
                  pl.BlockSpec(wo_bf.shape, lambda i: (0, 0)),
                  pl.BlockSpec((1, D_MODEL), lambda i: (0, 0))],
        out_specs=[row, row],
        out_shape=[jax.ShapeDtypeStruct((N, D_MODEL), F32), jax.ShapeDtypeStruct((N, D_MODEL), BF16)],
        compiler_params=_cparams(("arbitrary",)),
        name="merge",
    )(x2d, ya, yb, yc, z3, z3, z3, wb_bf, wo_bf, nf)


def _ffn_kernel(h_ref, x_ref, cp_ref, wu_ref, cw_ref, cb_ref, wd_ref, nf_ref, y_ref, cn_ref, carry, u_scr, act_scr,
                *, tm, cw, d_ff, nb):
    lt = pl.program_id(1)
    rows_per = tm // nb

    if nb == 1:
        @pl.when(lt == 0)
        def _():
            carry[...] = cp_ref[0]

    h = h_ref[0]
    edge = 8
    erow = lax.broadcasted_iota(jnp.int32, (edge, cw), 0)
    for jc in range(d_ff // cw):
        slot = jc % 2
        bases = (jc * cw, d_ff + jc * cw)
        for hf, base in enumerate(bases):
            u_scr[slot, hf] = jnp.dot(h, wu_ref[:, base:base + cw], preferred_element_type=F32)
        halves = []
        for hf, base in enumerate(bases):
            cols = slice(base, base + cw)
            u = u_scr[slot, hf]
            w0, w1, w2, bb = cw_ref[0:1, cols], cw_ref[1:2, cols], cw_ref[2:3, cols], cb_ref[:, cols]
            u1 = pltpu.roll(u, 1, 0)
            u2 = pltpu.roll(u, 2, 0)
            c = bb + w0 * u2 + w1 * u1 + w2 * u
            pieces = []
            for s in range(nb):
                r0 = s * rows_per
                prev = carry if nb == 1 else cp_ref.at[s]
                p2 = prev[0:1, cols]
                p1 = prev[1:2, cols]
                u1e = jnp.where(erow == 0, p1, u1[r0:r0 + edge])
                u2e = jnp.where(erow == 0, p2, jnp.where(erow == 1, p1, u2[r0:r0 + edge]))
                pieces += [bb + w0 * u2e + w1 * u1e + w2 * u[r0:r0 + edge], c[r0 + edge:r0 + rows_per]]
                tail = u[r0 + rows_per - (FFN_CONV - 1):r0 + rows_per, :]
                if nb == 1:
                    carry[:, cols] = tail
                cn_ref[s, :, cols] = tail
            halves.append(jnp.concatenate(pieces, axis=0))
        gate, val = halves
        act_scr[:, jc * cw:(jc + 1) * cw] = (gate * _sigmoid(gate) * val).astype(BF16)
    y = x_ref[0] + jnp.dot(act_scr[...], wd_ref[...], preferred_element_type=F32)
    y_ref[0] = _rms(y, nf_ref[...])


def _ffn(h2, x1, conv_prev, wu_bf, conv_w, conv_b, wd_bf, nf, *, tm, nb=1, cw=256):
    G, R, _ = x1.shape
    d_ff = wd_bf.shape[0]
    assert nb == 1 or tm == R
    row = pl.BlockSpec((1, tm, D_MODEL), lambda b, t: (b, t, 0))
    state = pl.BlockSpec((nb, FFN_CONV - 1, 2 * d_ff), lambda b, t: (b, 0, 0))
    const = lambda a: pl.BlockSpec(a.shape, lambda b, t: (0,) * a.ndim)
    return pl.pallas_call(
        functools.partial(_ffn_kernel, tm=tm, cw=cw, d_ff=d_ff, nb=nb),
        grid=(G, R // tm),
        in_specs=[row, row, state, const(wu_bf), const(conv_w), const(conv_b), const(wd_bf), const(nf)],
        out_specs=[row, state],
        out_shape=[jax.ShapeDtypeStruct((G, R, D_MODEL), F32),
                   jax.ShapeDtypeStruct((G * nb, FFN_CONV - 1, 2 * d_ff), F32)],
        scratch_shapes=[pltpu.VMEM((FFN_CONV - 1, 2 * d_ff), F32),
                        pltpu.VMEM((2, 2, tm, cw), F32),
                        pltpu.VMEM((tm, d_ff), BF16)],
        compiler_params=_cparams(("arbitrary", "arbitrary")),
        name="ffn",
    )(h2, x1, conv_prev, wu_bf, conv_w, conv_b, wd_bf, nf)


def _layer(x, W, *, hgrn_s0, conv_prev, mem_k, mem_v, attn, tm_proj, tm_merge, tm_ffn, tl_mem, head_layout):
    B, L, _ = x.shape
    N = B * L
    x2d = x.reshape(N, D_MODEL)
    z3, f2d, k_new, v_new = _in_proj(x2d, W["norm_attn"], W["w_in"], tm=tm_proj, rows_per_batch=L,
                                     head_layout=head_layout)
    z4 = z3.reshape(N_ZSLOT, B, L, D_MODEL)
    ya, s_new = _hgrn(z4, f2d.reshape(B, L, D_MODEL), W["lb_logits"], W["hgrn_onorm"], hgrn_s0, L=L,
                      tl=min(L, 512))
    yb, k_new, v_new = attn(z4, k_new, v_new)
    yc = _mem_attn(z4, mem_k, mem_v, tl=tl_mem)
    x1, h2 = _merge(x2d, ya.reshape(N, D_MODEL), yb.reshape(N, D_MODEL), yc.reshape(N, D_MODEL), z3,
                    W["w_branch"], W["w_out"], W["norm_ffn"], tm=tm_merge)
    nb = B if L < tm_ffn else 1
    groups = (B // nb, nb * L, D_MODEL)
    y, conv_new = _ffn(h2.reshape(groups), x1.reshape(groups), conv_prev, W["w_up"], W["conv_w"], W["conv_b"],
                       W["w_down"], W["norm_final"], tm=nb * L if nb > 1 else tm_ffn, nb=nb)
    return y.reshape(B, L, D_MODEL), k_new, v_new, s_new, conv_new


def kernel(x_prompt, x_sample, mem_prompt, cache_diff_k, cache_diff_v, cache_mem_k, cache_mem_v, state_hgrn, state_ffn_conv, rel_bias, hgrn_lb_logits, norm_attn, w_in, hgrn_onorm, diff_lambda, diff_subln, mem_norm, w_mem_kv, w_branch, w_out, norm_ffn, w_up, conv_w, conv_b, w_down, norm_final):
    assert w_in.shape[0] == 1, "single-layer trunk"
    Bp, Lp, _ = x_prompt.shape
    Bs, Ls, _ = x_sample.shape
    P = cache_diff_k.shape[3]
    d_ff2 = w_up.shape[2]
    row = lambda a: a.reshape(1, -1)
    W = dict(norm_attn=row(norm_attn[0]), w_in=w_in[0].astype(BF16), lb_logits=hgrn_lb_logits,
             hgrn_onorm=row(hgrn_onorm[0]), w_branch=w_branch[0].astype(BF16), w_out=w_out[0].astype(BF16),
             norm_ffn=row(norm_ffn[0]), w_up=w_up[0].astype(BF16), conv_w=conv_w[0], conv_b=row(conv_b[0]),
             w_down=w_down[0].astype(BF16), norm_final=row(norm_final))
    lp = diff_lambda[0]
    subln = row(diff_subln[0])

    T = 512
    bias_p, far_bucket = _bias_prompt(rel_bias, T)
    mk, mv = _mem_kv(mem_prompt, row(mem_norm[0]), w_mem_kv[0].astype(BF16))

    def attn_prompt(z4, k4, v4):
        yb = _diff_attn_prompt(z4, k4, v4, rel_bias, bias_p, lp, subln, T=T, G=2, far_bucket=far_bucket)
        return yb, k4, v4

    yp, pk, pv, ps, pc = _layer(
        x_prompt, W, hgrn_s0=jnp.zeros((Bp, N_HEADS, HEAD_W, HEAD_W), F32),
        conv_prev=jnp.zeros((Bp, FFN_CONV - 1, d_ff2), F32), mem_k=mk, mem_v=mv, attn=attn_prompt,
        tm_proj=256, tm_merge=512, tm_ffn=512, tl_mem=512, head_layout=True)

    bias_sp, bias_sn = _bias_sample(rel_bias, Ls, P)

    def attn_sample(z4, k2d, v2d):
        yb = _diff_attn_sample(z4, k2d.reshape(Bs, Ls, D_MODEL), v2d.reshape(Bs, Ls, D_MODEL),
                               cache_diff_k[0], cache_diff_v[0], bias_sp, bias_sn, lp, subln, hb=8)
        heads = lambda a: jnp.transpose(a.reshape(Bs, Ls, N_HEADS, HEAD_W), (0, 2, 1, 3))
        return yb, heads(k2d), heads(v2d)

    ys, sk, sv, ss, sc = _layer(
        x_sample, W, hgrn_s0=state_hgrn[0], conv_prev=state_ffn_conv[0], mem_k=cache_mem_k[0],
        mem_v=cache_mem_v[0], attn=attn_sample,
        tm_proj=Bs * Ls, tm_merge=Bs * Ls, tm_ffn=512, tl_mem=Ls, head_layout=False)

    return (yp, ys, pk[None], pv[None], ps[None], pc[None], mk[None], mv[None],
            sk[None], sv[None], ss[None], sc[None])
```

```python
import functools
import math

import numpy as np
import jax
import jax.numpy as jnp
from jax import lax
from jax.experimental import pallas as pl
from jax.experimental.pallas import tpu as pltpu

F32 = jnp.float32
BF16 = jnp.bfloat16

D_MODEL = 1024
CHUNK = 64
SUB = 16
HEAD_W = 128
N_HEADS = D_MODEL // HEAD_W
PAIR_W = 2 * HEAD_W
DIFF_DH = 64
DIFF_SCALE = DIFF_DH ** -0.5
MEM_HEADS = 4
MEM_DH = D_MODEL // MEM_HEADS
REL_BUCKETS = 32
REL_MAX_DIST = 128
N_BRANCH = 3
FFN_CONV = 3
RMS_EPS = 1e-6
LAMBDA_INIT = 0.8 - 0.6 * math.exp(-0.3 * 0)
NEG = -1e30
LOG2E = math.log2(math.e)
ONES_ROWS = 16

SEC_HQ, SEC_HF, SEC_HI, SEC_HG, SEC_DQ, SEC_DK, SEC_DV, SEC_MQ, SEC_G0 = range(9)
N_SEC = 11
Z_HQ, Z_HI, Z_HG, Z_DQ, Z_MQ, Z_G0 = 0, 1, 2, 3, 4, 5
N_ZSLOT = 8

V7X_VMEM_LIMIT = 56 * 1024 * 1024


def _cparams(sem, vmem=V7X_VMEM_LIMIT):
    return pltpu.CompilerParams(dimension_semantics=sem, vmem_limit_bytes=vmem)


def _sigmoid(x):
    return 0.5 * jnp.tanh(0.5 * x) + 0.5


def _rms(x, g):
    ms = jnp.mean(x * x, axis=-1, keepdims=True)
    return x * lax.rsqrt(ms + RMS_EPS) * g


def _np_bucket(rel):
    nb = REL_BUCKETS // 2
    ret = np.where(rel > 0, nb, 0)
    n = np.abs(rel)
    max_exact = nb // 2
    large = max_exact + (np.log(np.maximum(n, 1).astype(np.float32) / max_exact)
                         / math.log(REL_MAX_DIST / max_exact) * (nb - max_exact)).astype(np.int32)
    large = np.minimum(large, nb - 1)
    return ret + np.where(n < max_exact, n, large)


def _bucket_segments(lo, hi):
    rel = np.arange(lo, hi + 1, dtype=np.int32)
    b = _np_bucket(rel)
    change = np.nonzero(np.diff(b))[0]
    return int(b[0]), [(int(rel[i + 1]), int(b[i + 1])) for i in change]


def _bias_from_rel(rel, rb_ref, h, first_bucket, segs):
    val = jnp.full(rel.shape, rb_ref[first_bucket, h], F32)
    for lo, bk in segs:
        val = jnp.where(rel >= lo, rb_ref[bk, h], val)
    return val


def _bias_prompt_kernel(rb_ref, o_ref, *, T, first_bucket, segs):
    h = pl.program_id(0)
    kk = lax.broadcasted_iota(jnp.int32, (T, T), 0)
    qq = lax.broadcasted_iota(jnp.int32, (T, T), 1)
    o_ref[0, 0] = _bias_from_rel(kk - qq - T, rb_ref, h, first_bucket, segs) * LOG2E
    diag = _bias_from_rel(kk - qq, rb_ref, h, first_bucket, segs) * LOG2E
    o_ref[0, 1] = jnp.where(jnp.bitwise_and(kk, -CHUNK) <= qq, diag, NEG)


def _bias_prompt(rel_bias, T):
    H = rel_bias.shape[1]
    first_bucket, segs = _bucket_segments(-2 * T, T)
    assert all(lo > -T for lo, _ in segs), "bias must be constant beyond one tile"
    tiles = pl.pallas_call(
        functools.partial(_bias_prompt_kernel, T=T, first_bucket=first_bucket, segs=segs),
        grid=(H,),
        in_specs=[pl.BlockSpec(memory_space=pltpu.SMEM)],
        out_specs=pl.BlockSpec((1, 2, T, T), lambda h: (h, 0, 0, 0)),
        out_shape=jax.ShapeDtypeStruct((H, 2, T, T), F32),
        compiler_params=_cparams(("arbitrary",)),
        name="bias_prompt",
    )(rel_bias)
    return tiles, first_bucket


def _bias_sample_kernel(rb_ref, bp_ref, bn_ref, *, Ls, P, first_bucket, segs):
    h = pl.program_id(0)
    qq = jnp.bitwise_and(lax.broadcasted_iota(jnp.int32, (2 * Ls, P), 0), Ls - 1)
    kk = lax.broadcasted_iota(jnp.int32, (2 * Ls, P), 1)
    bp_ref[0] = _bias_from_rel(kk - P - qq, rb_ref, h, first_bucket, segs)
    qn = jnp.bitwise_and(lax.broadcasted_iota(jnp.int32, (2 * Ls, Ls), 0), Ls - 1)
    kn = lax.broadcasted_iota(jnp.int32, (2 * Ls, Ls), 1)
    bn_ref[0] = _bias_from_rel(kn - qn, rb_ref, h, first_bucket, segs)


def _bias_sample(rel_bias, Ls, P):
    H = rel_bias.shape[1]
    first_bucket, segs = _bucket_segments(-(P + Ls), Ls)
    return pl.pallas_call(
        functools.partial(_bias_sample_kernel, Ls=Ls, P=P, first_bucket=first_bucket, segs=segs),
        grid=(H,),
        in_specs=[pl.BlockSpec(memory_space=pltpu.SMEM)],
        out_specs=[pl.BlockSpec((1, 2 * Ls, P), lambda h: (h, 0, 0)),
                   pl.BlockSpec((1, 2 * Ls, Ls), lambda h: (h, 0, 0))],
        out_shape=[jax.ShapeDtypeStruct((H, 2 * Ls, P), F32),
                   jax.ShapeDtypeStruct((H, 2 * Ls, Ls), F32)],
        compiler_params=_cparams(("arbitrary",)),
        name="bias_sample",
    )(rel_bias)


Z_SLOT_OF_SEC = {SEC_HQ: Z_HQ, SEC_HI: Z_HI, SEC_HG: Z_HG, SEC_DQ: Z_DQ, SEC_MQ: Z_MQ,
                 SEC_G0: Z_G0, SEC_G0 + 1: Z_G0 + 1, SEC_G0 + 2: Z_G0 + 2}


def _in_proj_kernel(x_ref, g_ref, w_ref, z_ref, f_ref, k_ref, v_ref, *, head_layout):
    h = _rms(x_ref[...], g_ref[...]).astype(BF16)

    def store_heads(ref, acc):
        if head_layout:
            for hh in range(N_HEADS):
                ref[0, hh] = acc[:, hh * HEAD_W:(hh + 1) * HEAD_W]
        else:
            ref[...] = acc

    for sec in range(N_SEC):
        acc = jnp.dot(h, w_ref[:, sec * D_MODEL:(sec + 1) * D_MODEL], preferred_element_type=F32)
        if sec == SEC_HF:
            f_ref[...] = acc
        elif sec == SEC_DK:
            store_heads(k_ref, acc)
        elif sec == SEC_DV:
            store_heads(v_ref, acc)
        else:
            z_ref[Z_SLOT_OF_SEC[sec]] = acc.astype(BF16)


def _in_proj(x2d, g, w_bf, *, tm, rows_per_batch, head_layout):
    N = x2d.shape[0]
    if head_layout:
        B = N // rows_per_batch
        nlt = rows_per_batch // tm
        kv_shape = jax.ShapeDtypeStruct((B, N_HEADS, rows_per_batch, HEAD_W), F32)
        kv_spec = pl.BlockSpec((1, N_HEADS, tm, HEAD_W), lambda i: (i // nlt, 0, i % nlt, 0))
    else:
        kv_shape = jax.ShapeDtypeStruct((N, D_MODEL), F32)
        kv_spec = pl.BlockSpec((tm, D_MODEL), lambda i: (i, 0))
    return pl.pallas_call(
        functools.partial(_in_proj_kernel, head_layout=head_layout),
        grid=(N // tm,),
        in_specs=[pl.BlockSpec((tm, D_MODEL), lambda i: (i, 0)),
                  pl.BlockSpec((1, D_MODEL), lambda i: (0, 0)),
                  pl.BlockSpec(w_bf.shape, lambda i: (0, 0))],
        out_specs=[pl.BlockSpec((N_ZSLOT, tm, D_MODEL), lambda i: (0, i, 0)),
                   pl.BlockSpec((tm, D_MODEL), lambda i: (i, 0)),
                   kv_spec, kv_spec],
        out_shape=[jax.ShapeDtypeStruct((N_ZSLOT, N, D_MODEL), BF16),
                   jax.ShapeDtypeStruct((N, D_MODEL), F32),
                   kv_shape, kv_shape],
        compiler_params=_cparams(("arbitrary",)),
        name="in_proj",
    )(x2d, g, w_bf)


def _block_diag(a, b):
    za = jnp.zeros((a.shape[0], b.shape[1]), a.dtype)
    zb = jnp.zeros((b.shape[0], a.shape[1]), b.dtype)
    return jnp.concatenate([jnp.concatenate([a, za], axis=1), jnp.concatenate([zb, b], axis=1)], axis=0)


def _hgrn_kernel(q_ref, f_ref, i_ref, g_ref, lbl_ref, on_ref, s0_ref, ya_ref, sn_ref,
                 st_scr, sc_scr, o_scr, upd_scr, *, TL, C):
    lt = pl.program_id(1)
    nsub = C // SUB
    lbl = lbl_ref[...]
    e = jnp.exp(lbl - jnp.max(lbl, axis=0, keepdims=True))
    lb_all = e[0:1] / jnp.sum(e, axis=0, keepdims=True)
    fg_half = 0.5 * (1.0 - lb_all)
    fg_mid = lb_all + fg_half

    @pl.when(lt == 0)
    def _():
        st_scr[...] = s0_ref[0]

    srow = jnp.bitwise_and(lax.broadcasted_iota(jnp.int32, (C, D_MODEL), 0), SUB - 1)
    width = max(nsub * C, HEAD_W)
    wrow = lax.broadcasted_iota(jnp.int32, (C, width), 0)
    wcol = lax.broadcasted_iota(jnp.int32, (C, width), 1)
    base = jnp.bitwise_and(wrow, -SUB) * (C // SUB)
    keep = (wcol >= base) & (wcol <= base + wrow)
    keep2 = jnp.concatenate([keep, keep], axis=1)
    onorm = on_ref[...]
    nt = (((1,), (1,)), ((), ()))

    def chunk(c, carry):
        r0 = pl.multiple_of(c * C, C)
        f = f_ref[0, pl.ds(r0, C), :]
        fg = fg_mid + fg_half * jnp.tanh(0.5 * f)
        l2 = jnp.log2(fg)
        kk = 1.0 - fg
        c = l2
        sh = 1
        while sh < SUB:
            if sh % 8 == 0:
                shifted = jnp.concatenate([jnp.zeros((sh, D_MODEL), F32), c[:C - sh]], axis=0)
            else:
                shifted = pltpu.roll(c, sh, 0)
            c = c + jnp.where(srow >= sh, shifted, 0.0)
            sh *= 2
        r_rows = [jnp.zeros((1, D_MODEL), F32)]
        for i in range(1, nsub):
            r_rows.append(r_rows[-1] + c[SUB * i - 1:SUB * i, :])
        b_last = r_rows[-1] + c[C - 1:C, :]
        b = c if nsub == 1 else jnp.concatenate(
            [c[SUB * i:SUB * (i + 1)] + r_rows[i] for i in range(nsub)], axis=0)
        q = q_ref[0, 0, pl.ds(r0, C), :].astype(F32)
        q_inter = (q * jnp.exp2(b)).astype(BF16)
        q_intra = (q * jnp.exp2(c)).astype(BF16)
        k_state = kk * jnp.exp2(b_last - b)
        k_sub = []
        for i in range(nsub):
            n = SUB * (i + 1)
            ki = (kk[:n] * jnp.exp2(r_rows[i] - b[:n])).astype(BF16)
            if n < C:
                ki = jnp.concatenate([ki, jnp.zeros((C - n, D_MODEL), BF16)], axis=0)
            k_sub.append(ki)
        st_decay = jnp.exp2(b_last)
        hg = 0.5 * g_ref[0, 0, pl.ds(r0, C), :].astype(F32)
        gate = hg + hg * jnp.tanh(hg)
        pairs = range(N_HEADS // 2)
        pcs = [slice(p * PAIR_W, (p + 1) * PAIR_W) for p in pairs]
        hcs = [[slice((2 * p + a) * HEAD_W, (2 * p + a + 1) * HEAD_W) for a in range(2)] for p in pairs]
        zrows = jnp.zeros((HEAD_W - C, HEAD_W), BF16)
        vpads = []
        for p in pairs:
            k4 = []
            for a in range(2):
                pieces = [k[:, hcs[p][a]] for k in k_sub]
                if nsub * C < width:
                    pieces.append(jnp.zeros((width - nsub * C, HEAD_W), BF16))
                k4.append(jnp.concatenate(pieces, axis=0))
            vpads.append([jnp.concatenate([i_ref[0, 0, pl.ds(r0, C), hcs[p][a]], zrows], axis=0)
                          for a in range(2)])
            sc_scr[p] = lax.dot_general(q_intra[:, pcs[p]], _block_diag(k4[0], k4[1]), nt,
                                        preferred_element_type=F32)
        dcols = []
        for p in pairs:
            kT = []
            for a in range(2):
                kext = jnp.concatenate([k_state[:, hcs[p][a]],
                                        jnp.broadcast_to(st_decay[:, hcs[p][a]], (8, HEAD_W)),
                                        jnp.zeros((HEAD_W - C - 8, HEAD_W), F32)], axis=0).T
                dcols.append(kext[:, C:C + 1])
                kT.append(kext.astype(BF16))
            upd_scr[p] = jnp.dot(jnp.concatenate(kT, axis=1), _block_diag(vpads[p][0], vpads[p][1]),
                                 preferred_element_type=F32)
        for p in pairs:
            w = jnp.where(keep2, sc_scr[p], 0.0)
            lhs = []
            rhs = []
            for a in range(2):
                t = w[:, a * width:a * width + HEAD_W]
                for i in range(1, width // HEAD_W):
                    t = t + w[:, a * width + i * HEAD_W:a * width + (i + 1) * HEAD_W]
                if nsub > 1 and C < HEAD_W:
                    assert HEAD_W == 2 * C
                    t = t + pltpu.roll(t, C, 1)
                lhs += [t.astype(BF16), q_inter[:, hcs[p][a]]]
                rhs.append(jnp.concatenate([vpads[p][a], st_scr[2 * p + a].astype(BF16)], axis=0))
            o_scr[p] = jnp.dot(jnp.concatenate(lhs, axis=1), _block_diag(rhs[0], rhs[1]),
                               preferred_element_type=F32)
        for p in pairs:
            for a in range(2):
                hh = 2 * p + a
                st_scr[hh] = st_scr[hh] * dcols[hh] + upd_scr[p, :, a * HEAD_W:(a + 1) * HEAD_W]
        for p in pairs:
            for a in range(2):
                o = o_scr[p, :, a * HEAD_W:(a + 1) * HEAD_W]
                ya_ref[0, pl.ds(r0, C), hcs[p][a]] = (_rms(o, onorm) * gate[:, hcs[p][a]]).astype(BF16)
        return carry

    n_chunks = TL // C
    lax.fori_loop(0, n_chunks, chunk, 0, unroll=math.gcd(n_chunks, 4))

    @pl.when(lt == pl.num_programs(1) - 1)
    def _():
        sn_ref[0] = st_scr[...]


def _hgrn(z4, f3, lb_logits, onorm, s0, *, L, tl):
    B = z4.shape[1]
    C = min(CHUNK, L)
    slot = lambda s: pl.BlockSpec((1, 1, tl, D_MODEL), lambda b, t: (s, b, t, 0))
    row = pl.BlockSpec((1, tl, D_MODEL), lambda b, t: (b, t, 0))
    state = pl.BlockSpec((1, N_HEADS, HEAD_W, HEAD_W), lambda b, t: (b, 0, 0, 0))
    return pl.pallas_call(
        functools.partial(_hgrn_kernel, TL=tl, C=C),
        grid=(B, L // tl),
        in_specs=[slot(Z_HQ), row, slot(Z_HI), slot(Z_HG),
                  pl.BlockSpec(lb_logits.shape, lambda b, t: (0, 0)),
                  pl.BlockSpec((1, HEAD_W), lambda b, t: (0, 0)),
                  state],
        out_specs=[row, state],
        out_shape=[jax.ShapeDtypeStruct((B, L, D_MODEL), BF16),
                   jax.ShapeDtypeStruct((B, N_HEADS, HEAD_W, HEAD_W), F32)],
        scratch_shapes=[pltpu.VMEM((N_HEADS, HEAD_W, HEAD_W), F32),
                        pltpu.VMEM((N_HEADS // 2, C, 2 * max((C // SUB) * C, HEAD_W)), F32),
                        pltpu.VMEM((N_HEADS // 2, C, PAIR_W), F32),
                        pltpu.VMEM((N_HEADS // 2, HEAD_W, PAIR_W), F32)],
        compiler_params=_cparams(("arbitrary", "arbitrary")),
        name="hgrn",
    )(z4, f3, z4, z4, lb_logits, onorm, s0)


def _diff_lambda(lp_ref):
    lp = lp_ref[...]
    a = jnp.sum(lp[0:1] * lp[1:2], axis=-1, keepdims=True)
    b = jnp.sum(lp[2:3] * lp[3:4], axis=-1, keepdims=True)
    return jnp.exp(a) - jnp.exp(b) + LAMBDA_INIT


def _split_maps(q):
    lane = lax.broadcasted_iota(jnp.int32, q.shape, 1)
    return jnp.where(lane < DIFF_DH, q, 0.0), jnp.where(lane >= DIFF_DH, q, 0.0)


def _diff_attn_prompt_kernel(rb_ref, q_ref, k_ref, v_ref, bias_ref, lp_ref, sub_ref, o_ref, kb, vT, acc, m_scr, s_scr, q_scr,
                             *, T, nk, G, far_bucket):
    hg = pl.program_id(1)
    nt = (((1,), (1,)), ((), ()))

    ones = jnp.ones((ONES_ROWS, T), BF16)
    for g in range(G):
        for j in range(nk):
            kb[g, j] = (k_ref[0, g, j * T:(j + 1) * T, :] * (DIFF_SCALE * LOG2E)).astype(BF16)
            vT[g, j, :HEAD_W, :] = v_ref[0, g, j * T:(j + 1) * T, :].T.astype(BF16)
            vT[g, j, HEAD_W:, :] = ones
        qa, qb = _split_maps(q_ref[0, 0, :, g * HEAD_W:(g + 1) * HEAD_W])
        q_scr[2 * g] = qa
        q_scr[2 * g + 1] = qb
    lam = _diff_lambda(lp_ref)

    def tile(qi, j, keys, queries, bias_slot, first, slot):
        kr = slice(*keys)
        qc = slice(*queries)
        nkk, nqq = keys[1] - keys[0], queries[1] - queries[0]
        qrows = slice(qi * T + queries[0], qi * T + queries[1])
        for g in range(G):
            kt = kb[g, j, kr, :]
            for mp in range(2):
                s_scr[slot, 2 * g + mp, :nkk, :nqq] = lax.dot_general(
                    kt, q_scr[2 * g + mp, qrows, :], nt, preferred_element_type=F32)
        for g in range(G):
            vt = vT[g, j, :, kr]
            if bias_slot is None:
                shift = rb_ref[far_bucket, hg * G + g] * LOG2E
            else:
                bias = bias_ref[g, bias_slot, kr, qc]
            for mp in range(2):
                idx = 2 * g + mp
                s = s_scr[slot, idx, :nkk, :nqq]
                if bias_slot is None:
                    tmax = jnp.max(s, axis=0, keepdims=True) + shift
                else:
                    s = s + bias
                    tmax = jnp.max(s, axis=0, keepdims=True)
                if first:
                    mn = tmax
                else:
                    m = m_scr[idx, :, qc]
                    mn = jnp.maximum(m, tmax)
                p = jnp.exp2(s - (mn - shift)) if bias_slot is None else jnp.exp2(s - mn)
                m_scr[idx, :, qc] = mn
                pv = jnp.dot(vt, p.astype(BF16), preferred_element_type=F32)
                acc[idx, :, qc] = pv if first else jnp.exp2(m - mn) * acc[idx, :, qc] + pv

    n_tiles = 0
    half = T // 2
    assert half % CHUNK == 0
    for qi in range(nk):
        rows = slice(qi * T, (qi + 1) * T)
        order = [(j, (0, T), (0, T), None) for j in range(qi - 1)]
        if qi >= 1:
            order.append((qi - 1, (0, T), (0, T), 0))
        order += [(qi, (0, half), (0, T), 1), (qi, (half, T), (half, T), 1)]
        for t, (j, keys, queries, bias_slot) in enumerate(order):
            tile(qi, j, keys, queries, bias_slot, t == 0, n_tiles % 2)
            n_tiles += 1
        for g in range(G):
            a1 = acc[2 * g]
            a2 = acc[2 * g + 1]
            o = (a1[:HEAD_W] * (1.0 / a1[HEAD_W:HEAD_W + 1])
                 - a2[:HEAD_W] * (lam / a2[HEAD_W:HEAD_W + 1]))
            o = (o * lax.rsqrt(jnp.mean(o * o, axis=0, keepdims=True) + RMS_EPS)).T
            o_ref[0, rows, g * HEAD_W:(g + 1) * HEAD_W] = (o * sub_ref[...] * (1.0 - LAMBDA_INIT)).astype(BF16)


def _diff_attn_prompt(z4, k4, v4, rel_bias, bias, lp, subln, *, T, G, far_bucket):
    B, H, L, _ = k4.shape
    nk = L // T
    W = G * HEAD_W
    kv_spec = pl.BlockSpec((1, G, L, HEAD_W), lambda b, h: (b, h, 0, 0))
    return pl.pallas_call(
        functools.partial(_diff_attn_prompt_kernel, T=T, nk=nk, G=G, far_bucket=far_bucket),
        grid=(B, H // G),
        in_specs=[pl.BlockSpec(memory_space=pltpu.SMEM),
                  pl.BlockSpec((1, 1, L, W), lambda b, h: (Z_DQ, b, 0, h)),
                  kv_spec, kv_spec,
                  pl.BlockSpec((G, 2, T, T), lambda b, h: (h, 0, 0, 0)),
                  pl.BlockSpec(lp.shape, lambda b, h: (0, 0)),
                  pl.BlockSpec((1, HEAD_W), lambda b, h: (0, 0))],
        out_specs=pl.BlockSpec((1, L, W), lambda b, h: (b, 0, h)),
        out_shape=jax.ShapeDtypeStruct((B, L, D_MODEL), BF16),
        scratch_shapes=[pltpu.VMEM((G, nk, T, HEAD_W), BF16),
                        pltpu.VMEM((G, nk, HEAD_W + ONES_ROWS, T), BF16),
                        pltpu.VMEM((2 * G, HEAD_W + ONES_ROWS, T), F32),
                        pltpu.VMEM((2 * G, 1, T), F32),
                        pltpu.VMEM((2, 2 * G, T, T), F32),
                        pltpu.VMEM((2 * G, L, HEAD_W), BF16)],
        compiler_params=_cparams(("arbitrary", "arbitrary")),
        name="diff_attn_prompt",
    )(rel_bias, z4, k4, v4, bias, lp, subln)


def _diff_attn_sample_kernel(q_ref, kn_ref, vn_ref, kc_ref, vc_ref, bp_ref, bn_ref, lp_ref, sub_ref, o_ref,
                             *, Ls, hb):
    lam = _diff_lambda(lp_ref)
    nt = (((1,), (1,)), ((), ()))
    for hh in range(hb):
        cols = slice(hh * HEAD_W, (hh + 1) * HEAD_W)
        qa, qb = _split_maps(q_ref[0, 0, :, cols].astype(F32) * DIFF_SCALE)
        q2 = jnp.concatenate([qa, qb], axis=0).astype(BF16)
        kp = kc_ref[0, hh].astype(BF16)
        vp = vc_ref[0, hh].astype(BF16)
        kn = kn_ref[0, :, cols].astype(BF16)
        vn = vn_ref[0, :, cols].astype(BF16)
        s = lax.dot_general(q2, kp, nt, preferred_element_type=F32) + bp_ref[hh]
        sn = lax.dot_general(q2, kn, nt, preferred_element_type=F32) + bn_ref[hh]
        m = jnp.maximum(jnp.max(s, axis=-1, keepdims=True), jnp.max(sn, axis=-1, keepdims=True))
        p = jnp.exp(s - m)
        pn = jnp.exp(sn - m)
        l = jnp.sum(p, axis=-1, keepdims=True) + jnp.sum(pn, axis=-1, keepdims=True)
        o2 = (jnp.dot(p.astype(BF16), vp, preferred_element_type=F32)
              + jnp.dot(pn.astype(BF16), vn, preferred_element_type=F32)) / l
        o = o2[:Ls] - lam * o2[Ls:]
        o_ref[0, :, cols] = (_rms(o, sub_ref[...]) * (1.0 - LAMBDA_INIT)).astype(BF16)


def _diff_attn_sample(z4, kn3, vn3, kc, vc, bias_p, bias_n, lp, subln, *, hb):
    B, H, P, _ = kc.shape
    Ls = kn3.shape[1]
    W = hb * HEAD_W
    cache_spec = pl.BlockSpec((1, hb, P, HEAD_W), lambda b, g: (b, g, 0, 0))
    new_spec = pl.BlockSpec((1, Ls, W), lambda b, g: (b, 0, g))
    return pl.pallas_call(
        functools.partial(_diff_attn_sample_kernel, Ls=Ls, hb=hb),
        grid=(B, H // hb),
        in_specs=[pl.BlockSpec((1, 1, Ls, W), lambda b, g: (Z_DQ, b, 0, g)),
                  new_spec, new_spec, cache_spec, cache_spec,
                  pl.BlockSpec((hb, 2 * Ls, P), lambda b, g: (g, 0, 0)),
                  pl.BlockSpec((hb, 2 * Ls, Ls), lambda b, g: (g, 0, 0)),
                  pl.BlockSpec(lp.shape, lambda b, g: (0, 0)),
                  pl.BlockSpec((1, HEAD_W), lambda b, g: (0, 0))],
        out_specs=pl.BlockSpec((1, Ls, W), lambda b, g: (b, 0, g)),
        out_shape=jax.ShapeDtypeStruct((B, Ls, D_MODEL), BF16),
        compiler_params=_cparams(("arbitrary", "arbitrary")),
        name="diff_attn_sample",
    )(z4, kn3, vn3, kc, vc, bias_p, bias_n, lp, subln)


def _mem_kv_kernel(m_ref, g_ref, w_ref, k_ref, v_ref):
    h = _rms(m_ref[0], g_ref[...]).astype(BF16)
    kv = jnp.dot(h, w_ref[...], preferred_element_type=F32)
    for hh in range(MEM_HEADS):
        k_ref[0, hh] = kv[:, hh * MEM_DH:(hh + 1) * MEM_DH]
        v_ref[0, hh] = kv[:, D_MODEL + hh * MEM_DH:D_MODEL + (hh + 1) * MEM_DH]


def _mem_kv(mem, g, w_bf):
    B, M, _ = mem.shape
    out = jax.ShapeDtypeStruct((B, MEM_HEADS, M, MEM_DH), F32)
    spec = pl.BlockSpec((1, MEM_HEADS, M, MEM_DH), lambda b: (b, 0, 0, 0))
    return pl.pallas_call(
        _mem_kv_kernel,
        grid=(B,),
        in_specs=[pl.BlockSpec((1, M, D_MODEL), lambda b: (b, 0, 0)),
                  pl.BlockSpec((1, D_MODEL), lambda b: (0, 0)),
                  pl.BlockSpec(w_bf.shape, lambda b: (0, 0))],
        out_specs=[spec, spec],
        out_shape=[out, out],
        compiler_params=_cparams(("arbitrary",)),
        name="mem_kv",
    )(mem, g, w_bf)


def _mem_attn_kernel(q_ref, k_ref, v_ref, o_ref, kb, vb, s_scr):
    nt = (((1,), (1,)), ((), ()))

    @pl.when(pl.program_id(1) == 0)
    def _():
        kb[...] = k_ref[0].astype(BF16)
        vb[...] = v_ref[0].astype(BF16)

    for hh in range(MEM_HEADS):
        cols = slice(hh * MEM_DH, (hh + 1) * MEM_DH)
        s_scr[hh] = lax.dot_general(q_ref[0, 0, :, cols], kb[hh], nt, preferred_element_type=F32)
    for hh in range(MEM_HEADS):
        cols = slice(hh * MEM_DH, (hh + 1) * MEM_DH)
        s = s_scr[hh] * (MEM_DH ** -0.5)
        p = jnp.exp(s - jnp.max(s, axis=-1, keepdims=True))
        p = p / jnp.sum(p, axis=-1, keepdims=True)
        o_ref[0, :, cols] = jnp.dot(p.astype(BF16), vb[hh], preferred_element_type=F32).astype(BF16)


def _mem_attn(z4, mk, mv, *, tl):
    _, B, L, _ = z4.shape
    M = mk.shape[2]
    kv_spec = pl.BlockSpec((1, MEM_HEADS, M, MEM_DH), lambda b, t: (b, 0, 0, 0))
    return pl.pallas_call(
        _mem_attn_kernel,
        grid=(B, L // tl),
        in_specs=[pl.BlockSpec((1, 1, tl, D_MODEL), lambda b, t: (Z_MQ, b, t, 0)), kv_spec, kv_spec],
        out_specs=pl.BlockSpec((1, tl, D_MODEL), lambda b, t: (b, t, 0)),
        out_shape=jax.ShapeDtypeStruct((B, L, D_MODEL), BF16),
        scratch_shapes=[pltpu.VMEM((MEM_HEADS, M, MEM_DH), BF16),
                        pltpu.VMEM((MEM_HEADS, M, MEM_DH), BF16),
                        pltpu.VMEM((MEM_HEADS, tl, M), F32)],
        compiler_params=_cparams(("arbitrary", "arbitrary")),
        name="mem_attn",
    )(z4, mk, mv)


def _merge_kernel(x_ref, ya_ref, yb_ref, yc_ref, g0_ref, g1_ref, g2_ref, wb_ref, wo_ref, nf_ref,
                  x1_ref, h2_ref):
    merged = None
    for n, (y_ref, gate_ref) in enumerate(((ya_ref, g0_ref), (yb_ref, g1_ref), (yc_ref, g2_ref))):
        proj = jnp.dot(y_ref[...], wb_ref[n], preferred_element_type=F32)
        term = proj * _sigmoid(gate_ref[0].astype(F32))
        merged = term if merged is None else merged + term
    x1 = x_ref[...] + jnp.dot(merged.astype(BF16), wo_ref[...], preferred_element_type=F32)
    x1_ref[...] = x1
    h2_ref[...] = _rms(x1, nf_ref[...]).astype(BF16)


def _merge(x2d, ya, yb, yc, z3, wb_bf, wo_bf, nf, *, tm):
    N = x2d.shape[0]
    row = pl.BlockSpec((tm, D_MODEL), lambda i: (i, 0))
    gate = lambda n: pl.BlockSpec((1, tm, D_MODEL), lambda i: (Z_G0 + n, i, 0))
    return pl.pallas_call(
        _merge_kernel,
        grid=(N // tm,),
        in_specs=[row, row, row, row, gate(0), gate(1), gate(2),
                  pl.BlockSpec(wb_bf.shape, lambda i: (0, 0, 0)),
                  pl.BlockSpec(wo_bf.shape, lambda i: (0, 0)),
                  pl.BlockSpec((1, D_MODEL), lambda i: (0, 0))],
        out_specs=[row, row],
        out_shape=[jax.ShapeDtypeStruct((N, D_MODEL), F32), jax.ShapeDtypeStruct((N, D_MODEL), BF16)],
        compiler_params=_cparams(("arbitrary",)),
        name="merge",
    )(x2d, ya, yb, yc, z3, z3, z3, wb_bf, wo_bf, nf)


def _ffn_kernel(h_ref, x_ref, cp_ref, wu_ref, cw_ref, cb_ref, wd_ref, nf_ref, y_ref, cn_ref, carry, u_scr, act_scr,
                *, tm, cw, d_ff, nb):
    lt = pl.program_id(1)
    rows_per = tm // nb

    if nb == 1:
        @pl.when(lt == 0)
        def _():
            carry[...] = cp_ref[0]

    h = h_ref[0]
    edge = 8
    erow = lax.broadcasted_iota(jnp.int32, (edge, cw), 0)
    for jc in range(d_ff // cw):
        slot = jc % 2
        bases = (jc * cw, d_ff + jc * cw)
        for hf, base in enumerate(bases):
            u_scr[slot, hf] = jnp.dot(h, wu_ref[:, base:base + cw], preferred_element_type=F32)
        halves = []
        for hf, base in enumerate(bases):
            cols = slice(base, base + cw)
            u = u_scr[slot, hf]
            w0, w1, w2, bb = cw_ref[0:1, cols], cw_ref[1:2, cols], cw_ref[2:3, cols], cb_ref[:, cols]
            u1 = pltpu.roll(u, 1, 0)
            u2 = pltpu.roll(u, 2, 0)
            c = bb + w0 * u2 + w1 * u1 + w2 * u
            pieces = []
            for s in range(nb):
                r0 = s * rows_per
                prev = carry if nb == 1 else cp_ref.at[s]
                p2 = prev[0:1, cols]
                p1 = prev[1:2, cols]
                u1e = jnp.where(erow == 0, p1, u1[r0:r0 + edge])
                u2e = jnp.where(erow == 0, p2, jnp.where(erow == 1, p1, u2[r0:r0 + edge]))
                pieces += [bb + w0 * u2e + w1 * u1e + w2 * u[r0:r0 + edge], c[r0 + edge:r0 + rows_per]]
                tail = u[r0 + rows_per - (FFN_CONV - 1):r0 + rows_per, :]
                if nb == 1:
                    carry[:, cols] = tail
                cn_ref[s, :, cols] = tail
            halves.append(jnp.concatenate(pieces, axis=0))
        gate, val = halves
        act_scr[:, jc * cw:(jc + 1) * cw] = (gate * _sigmoid(gate) * val).astype(BF16)
    y = x_ref[0] + jnp.dot(act_scr[...], wd_ref[...], preferred_element_type=F32)
    y_ref[0] = _rms(y, nf_ref[...])


def _ffn(h2, x1, conv_prev, wu_bf, conv_w, conv_b, wd_bf, nf, *, tm, nb=1, cw=256):
    G, R, _ = x1.shape
    d_ff = wd_bf.shape[0]
    assert nb == 1 or tm == R
    row = pl.BlockSpec((1, tm, D_MODEL), lambda b, t: (b, t, 0))
    state = pl.BlockSpec((nb, FFN_CONV - 1, 2 * d_ff), lambda b, t: (b, 0, 0))
    const = lambda a: pl.BlockSpec(a.shape, lambda b, t: (0,) * a.ndim)
    return pl.pallas_call(
        functools.partial(_ffn_kernel, tm=tm, cw=cw, d_ff=d_ff, nb=nb),
        grid=(G, R // tm),
        in_specs=[row, row, state, const(wu_bf), const(conv_w), const(conv_b), const(wd_bf), const(nf)],
        out_specs=[row, state],
        out_shape=[jax.ShapeDtypeStruct((G, R, D_MODEL), F32),
                   jax.ShapeDtypeStruct((G * nb, FFN_CONV - 1, 2 * d_ff), F32)],
        scratch_shapes=[pltpu.VMEM((FFN_CONV - 1, 2 * d_ff), F32),
                        pltpu.VMEM((2, 2, tm, cw), F32),
                        pltpu.VMEM((tm, d_ff), BF16)],
        compiler_params=_cparams(("arbitrary", "arbitrary")),
        name="ffn",
    )(h2, x1, conv_prev, wu_bf, conv_w, conv_b, wd_bf, nf)


def _layer(x, W, *, hgrn_s0, conv_prev, mem_k, mem_v, attn, tm_proj, tm_merge, tm_ffn, tl_mem, head_layout):
    B, L, _ = x.shape
    N = B * L
    x2d = x.reshape(N, D_MODEL)
    z3, f2d, k_new, v_new = _in_proj(x2d, W["norm_attn"], W["w_in"], tm=tm_proj, rows_per_batch=L,
                                     head_layout=head_layout)
    z4 = z3.reshape(N_ZSLOT, B, L, D_MODEL)
    ya, s_new = _hgrn(z4, f2d.reshape(B, L, D_MODEL), W["lb_logits"], W["hgrn_onorm"], hgrn_s0, L=L,
                      tl=min(L, 512))
    yb, k_new, v_new = attn(z4, k_new, v_new)
    yc = _mem_attn(z4, mem_k, mem_v, tl=tl_mem)
    x1, h2 = _merge(x2d, ya.reshape(N, D_MODEL), yb.reshape(N, D_MODEL), yc.reshape(N, D_MODEL), z3,
                    W["w_branch"], W["w_out"], W["norm_ffn"], tm=tm_merge)
    nb = B if L < tm_ffn else 1
    groups = (B // nb, nb * L, D_MODEL)
    y, conv_new = _ffn(h2.reshape(groups), x1.reshape(groups), conv_prev, W["w_up"], W["conv_w"], W["conv_b"],
                       W["w_down"], W["norm_final"], tm=nb * L if nb > 1 else tm_ffn, nb=nb)
    return y.reshape(B, L, D_MODEL), k_new, v_new, s_new, conv_new


def kernel(x_prompt, x_sample, mem_prompt, cache_diff_k, cache_diff_v, cache_mem_k, cache_mem_v, state_hgrn, state_ffn_conv, rel_bias, hgrn_lb_logits, norm_attn, w_in, hgrn_onorm, diff_lambda, diff_subln, mem_norm, w_mem_kv, w_branch, w_out, norm_ffn, w_up, conv_w, conv_b, w_down, norm_final):
    assert w_in.shape[0] == 1, "single-layer trunk"
    Bp, Lp, _ = x_prompt.shape
    Bs, Ls, _ = x_sample.shape
    P = cache_diff_k.shape[3]
    d_ff2 = w_up.shape[2]
    row = lambda a: a.reshape(1, -1)
    W = dict(norm_attn=row(norm_attn[0]), w_in=w_in[0].astype(BF16), lb_logits=hgrn_lb_logits,
             hgrn_onorm=row(hgrn_onorm[0]), w_branch=w_branch[0].astype(BF16), w_out=w_out[0].astype(BF16),
             norm_ffn=row(norm_ffn[0]), w_up=w_up[0].astype(BF16), conv_w=conv_w[0], conv_b=row(conv_b[0]),
             w_down=w_down[0].astype(BF16), norm_final=row(norm_final))
    lp = diff_lambda[0]
    subln = row(diff_subln[0])

    T = 512
    bias_p, far_bucket = _bias_prompt(rel_bias, T)
    mk, mv = _mem_kv(mem_prompt, row(mem_norm[0]), w_mem_kv[0].astype(BF16))

    def attn_prompt(z4, k4, v4):
        yb = _diff_attn_prompt(z4, k4, v4, rel_bias, bias_p, lp, subln, T=T, G=2, far_bucket=far_bucket)
        return yb, k4, v4

    yp, pk, pv, ps, pc = _layer(
        x_prompt, W, hgrn_s0=jnp.zeros((Bp, N_HEADS, HEAD_W, HEAD_W), F32),
        conv_prev=jnp.zeros((Bp, FFN_CONV - 1, d_ff2), F32), mem_k=mk, mem_v=mv, attn=attn_prompt,
        tm_proj=256, tm_merge=512, tm_ffn=512, tl_mem=512, head_layout=True)

    bias_sp, bias_sn = _bias_sample(rel_bias, Ls, P)

    def attn_sample(z4, k2d, v2d):
        yb = _diff_attn_sample(z4, k2d.reshape(Bs, Ls, D_MODEL), v2d.reshape(Bs, Ls, D_MODEL),
                               cache_diff_k[0], cache_diff_v[0], bias_sp, bias_sn, lp, subln, hb=8)
        heads = lambda a: jnp.transpose(a.reshape(Bs, Ls, N_HEADS, HEAD_W), (0, 2, 1, 3))
        return yb, heads(k2d), heads(v2d)

    ys, sk, sv, ss, sc = _layer(
        x_sample, W, hgrn_s0=state_hgrn[0], conv_prev=state_ffn_conv[0], mem_k=cache_mem_k[0],
        mem_v=cache_mem_v[0], attn=attn_sample,
        tm_proj=Bs * Ls, tm_merge=Bs * Ls, tm_ffn=512, tl_mem=Ls, head_layout=False)

    return (yp, ys, pk[None], pv[None], ps[None], pc[None], mk[None], mv[None],
            sk[None], sv[None], ss[None], sc[None])
```

```python
import functools
import math

import numpy as np
import jax
import jax.numpy as jnp
from jax import lax
from jax.experimental import pallas as pl
from jax.experimental.pallas import tpu as pltpu

F32 = jnp.float32
BF16 = jnp.bfloat16

D_MODEL = 1024
CHUNK = 64
SUB = 16
HEAD_W = 128
N_HEADS = D_MODEL // HEAD_W
PAIR_W = 2 * HEAD_W
DIFF_DH = 64
DIFF_SCALE = DIFF_DH ** -0.5
MEM_HEADS = 4
MEM_DH = D_MODEL // MEM_HEADS
REL_BUCKETS = 32
REL_MAX_DIST = 128
FFN_CONV = 3
RMS_EPS = 1e-6
LAMBDA_INIT = 0.8 - 0.6 * math.exp(-0.3 * 0)
NEG = -1e30
LOG2E = math.log2(math.e)
ONES_ROWS = 16
ATTN_TILE = 512
PROMPT_ATTN_HEADS = 2
SAMPLE_ATTN_HEADS = 8
ROW_TILE = 512
PROJ_ROW_TILE = 256

SEC_HQ, SEC_HF, SEC_HI, SEC_HG, SEC_DQ, SEC_DK, SEC_DV, SEC_MQ, SEC_G0 = range(9)
N_SEC = 11
Z_HQ, Z_HI, Z_HG, Z_DQ, Z_MQ, Z_G0 = 0, 1, 2, 3, 4, 5
N_ZSLOT = 8

V7X_VMEM_LIMIT = 56 * 1024 * 1024


def _cparams(sem, vmem=V7X_VMEM_LIMIT):
    return pltpu.CompilerParams(dimension_semantics=sem, vmem_limit_bytes=vmem)


def _sigmoid(x):
    return 0.5 * jnp.tanh(0.5 * x) + 0.5


def _rms(x, g):
    ms = jnp.mean(x * x, axis=-1, keepdims=True)
    return x * lax.rsqrt(ms + RMS_EPS) * g


def _np_bucket(rel):
    nb = REL_BUCKETS // 2
    ret = np.where(rel > 0, nb, 0)
    n = np.abs(rel)
    max_exact = nb // 2
    large = max_exact + (np.log(np.maximum(n, 1).astype(np.float32) / max_exact)
                         / math.log(REL_MAX_DIST / max_exact) * (nb - max_exact)).astype(np.int32)
    large = np.minimum(large, nb - 1)
    return ret + np.where(n < max_exact, n, large)


def _bucket_segments(lo, hi):
    rel = np.arange(lo, hi + 1, dtype=np.int32)
    b = _np_bucket(rel)
    change = np.nonzero(np.diff(b))[0]
    return int(b[0]), [(int(rel[i + 1]), int(b[i + 1])) for i in change]


def _bias_from_rel(rel, rb_ref, h, first_bucket, segs):
    val = jnp.full(rel.shape, rb_ref[first_bucket, h], F32)
    for lo, bk in segs:
        val = jnp.where(rel >= lo, rb_ref[bk, h], val)
    return val


def _bias_prompt_kernel(rb_ref, o_ref, *, T, first_bucket, segs):
    h = pl.program_id(0)
    kk = lax.broadcasted_iota(jnp.int32, (T, T), 0)
    qq = lax.broadcasted_iota(jnp.int32, (T, T), 1)
    o_ref[0, 0] = _bias_from_rel(kk - qq - T, rb_ref, h, first_bucket, segs) * LOG2E
    diag = _bias_from_rel(kk - qq, rb_ref, h, first_bucket, segs) * LOG2E
    o_ref[0, 1] = jnp.where(jnp.bitwise_and(kk, -CHUNK) <= qq, diag, NEG)


def _bias_prompt(rel_bias, T):
    H = rel_bias.shape[1]
    first_bucket, segs = _bucket_segments(-2 * T, T)
    assert all(lo > -T for lo, _ in segs), "bias must be constant beyond one tile"
    tiles = pl.pallas_call(
        functools.partial(_bias_prompt_kernel, T=T, first_bucket=first_bucket, segs=segs),
        grid=(H,),
        in_specs=[pl.BlockSpec(memory_space=pltpu.SMEM)],
        out_specs=pl.BlockSpec((1, 2, T, T), lambda h: (h, 0, 0, 0)),
        out_shape=jax.ShapeDtypeStruct((H, 2, T, T), F32),
        compiler_params=_cparams(("arbitrary",)),
        name="bias_prompt",
    )(rel_bias)
    return tiles, first_bucket


def _bias_sample_kernel(rb_ref, bp_ref, bn_ref, *, Ls, P, first_bucket, segs):
    h = pl.program_id(0)
    qq = jnp.bitwise_and(lax.broadcasted_iota(jnp.int32, (2 * Ls, P), 0), Ls - 1)
    kk = lax.broadcasted_iota(jnp.int32, (2 * Ls, P), 1)
    bp_ref[0] = _bias_from_rel(kk - P - qq, rb_ref, h, first_bucket, segs)
    qn = jnp.bitwise_and(lax.broadcasted_iota(jnp.int32, (2 * Ls, Ls), 0), Ls - 1)
    kn = lax.broadcasted_iota(jnp.int32, (2 * Ls, Ls), 1)
    bn_ref[0] = _bias_from_rel(kn - qn, rb_ref, h, first_bucket, segs)


def _bias_sample(rel_bias, Ls, P):
    H = rel_bias.shape[1]
    first_bucket, segs = _bucket_segments(-(P + Ls), Ls)
    return pl.pallas_call(
        functools.partial(_bias_sample_kernel, Ls=Ls, P=P, first_bucket=first_bucket, segs=segs),
        grid=(H,),
        in_specs=[pl.BlockSpec(memory_space=pltpu.SMEM)],
        out_specs=[pl.BlockSpec((1, 2 * Ls, P), lambda h: (h, 0, 0)),
                   pl.BlockSpec((1, 2 * Ls, Ls), lambda h: (h, 0, 0))],
        out_shape=[jax.ShapeDtypeStruct((H, 2 * Ls, P), F32),
                   jax.ShapeDtypeStruct((H, 2 * Ls, Ls), F32)],
        compiler_params=_cparams(("arbitrary",)),
        name="bias_sample",
    )(rel_bias)


Z_SLOT_OF_SEC = {SEC_HQ: Z_HQ, SEC_HI: Z_HI, SEC_HG: Z_HG, SEC_DQ: Z_DQ, SEC_MQ: Z_MQ,
                 SEC_G0: Z_G0, SEC_G0 + 1: Z_G0 + 1, SEC_G0 + 2: Z_G0 + 2}


def _in_proj_kernel(x_ref, g_ref, w_ref, z_ref, f_ref, k_ref, v_ref, *, head_layout):
    h = _rms(x_ref[...], g_ref[...]).astype(BF16)

    def store_heads(ref, acc):
        if head_layout:
            for hh in range(N_HEADS):
                ref[0, hh] = acc[:, hh * HEAD_W:(hh + 1) * HEAD_W]
        else:
            ref[...] = acc

    for sec in range(N_SEC):
        acc = jnp.dot(h, w_ref[:, sec * D_MODEL:(sec + 1) * D_MODEL], preferred_element_type=F32)
        if sec == SEC_HF:
            f_ref[...] = acc
        elif sec == SEC_DK:
            store_heads(k_ref, acc)
        elif sec == SEC_DV:
            store_heads(v_ref, acc)
        else:
            z_ref[Z_SLOT_OF_SEC[sec]] = acc.astype(BF16)


def _in_proj(x2d, g, w_bf, *, tm, rows_per_batch, head_layout):
    N = x2d.shape[0]
    if head_layout:
        B = N // rows_per_batch
        nlt = rows_per_batch // tm
        kv_shape = jax.ShapeDtypeStruct((B, N_HEADS, rows_per_batch, HEAD_W), F32)
        kv_spec = pl.BlockSpec((1, N_HEADS, tm, HEAD_W), lambda i: (i // nlt, 0, i % nlt, 0))
    else:
        kv_shape = jax.ShapeDtypeStruct((N, D_MODEL), F32)
        kv_spec = pl.BlockSpec((tm, D_MODEL), lambda i: (i, 0))
    return pl.pallas_call(
        functools.partial(_in_proj_kernel, head_layout=head_layout),
        grid=(N // tm,),
        in_specs=[pl.BlockSpec((tm, D_MODEL), lambda i: (i, 0)),
                  pl.BlockSpec((1, D_MODEL), lambda i: (0, 0)),
                  pl.BlockSpec(w_bf.shape, lambda i: (0, 0))],
        out_specs=[pl.BlockSpec((N_ZSLOT, tm, D_MODEL), lambda i: (0, i, 0)),
                   pl.BlockSpec((tm, D_MODEL), lambda i: (i, 0)),
                   kv_spec, kv_spec],
        out_shape=[jax.ShapeDtypeStruct((N_ZSLOT, N, D_MODEL), BF16),
                   jax.ShapeDtypeStruct((N, D_MODEL), F32),
                   kv_shape, kv_shape],
        compiler_params=_cparams(("arbitrary",)),
        name="in_proj",
    )(x2d, g, w_bf)


def _block_diag(a, b):
    za = jnp.zeros((a.shape[0], b.shape[1]), a.dtype)
    zb = jnp.zeros((b.shape[0], a.shape[1]), b.dtype)
    return jnp.concatenate([jnp.concatenate([a, za], axis=1), jnp.concatenate([zb, b], axis=1)], axis=0)


def _hgrn_kernel(q_ref, f_ref, i_ref, g_ref, lbl_ref, on_ref, s0_ref, ya_ref, sn_ref,
                 st_scr, sc_scr, o_scr, upd_scr, *, TL, C):
    lt = pl.program_id(1)
    nsub = C // SUB
    lbl = lbl_ref[...]
    e = jnp.exp(lbl - jnp.max(lbl, axis=0, keepdims=True))
    lb_all = e[0:1] / jnp.sum(e, axis=0, keepdims=True)
    fg_half = 0.5 * (1.0 - lb_all)
    fg_mid = lb_all + fg_half

    @pl.when(lt == 0)
    def _():
        st_scr[...] = s0_ref[0]

    srow = jnp.bitwise_and(lax.broadcasted_iota(jnp.int32, (C, D_MODEL), 0), SUB - 1)
    width = max(nsub * C, HEAD_W)
    wrow = lax.broadcasted_iota(jnp.int32, (C, width), 0)
    wcol = lax.broadcasted_iota(jnp.int32, (C, width), 1)
    base = jnp.bitwise_and(wrow, -SUB) * (C // SUB)
    keep = (wcol >= base) & (wcol <= base + wrow)
    keep2 = jnp.concatenate([keep, keep], axis=1)
    onorm = on_ref[...]
    nt = (((1,), (1,)), ((), ()))

    def chunk(c, carry):
        r0 = pl.multiple_of(c * C, C)
        f = f_ref[0, pl.ds(r0, C), :]
        fg = fg_mid + fg_half * jnp.tanh(0.5 * f)
        l2 = jnp.log2(fg)
        kk = 1.0 - fg
        c = l2
        sh = 1
        while sh < SUB:
            if sh % 8 == 0:
                shifted = jnp.concatenate([jnp.zeros((sh, D_MODEL), F32), c[:C - sh]], axis=0)
            else:
                shifted = pltpu.roll(c, sh, 0)
            c = c + jnp.where(srow >= sh, shifted, 0.0)
            sh *= 2
        r_rows = [jnp.zeros((1, D_MODEL), F32)]
        for i in range(1, nsub):
            r_rows.append(r_rows[-1] + c[SUB * i - 1:SUB * i, :])
        b_last = r_rows[-1] + c[C - 1:C, :]
        b = c if nsub == 1 else jnp.concatenate(
            [c[SUB * i:SUB * (i + 1)] + r_rows[i] for i in range(nsub)], axis=0)
        q = q_ref[0, 0, pl.ds(r0, C), :].astype(F32)
        q_inter = (q * jnp.exp2(b)).astype(BF16)
        q_intra = (q * jnp.exp2(c)).astype(BF16)
        k_state = kk * jnp.exp2(b_last - b)
        k_sub = []
        for i in range(nsub):
            n = SUB * (i + 1)
            ki = (kk[:n] * jnp.exp2(r_rows[i] - b[:n])).astype(BF16)
            if n < C:
                ki = jnp.concatenate([ki, jnp.zeros((C - n, D_MODEL), BF16)], axis=0)
            k_sub.append(ki)
        st_decay = jnp.exp2(b_last)
        hg = 0.5 * g_ref[0, 0, pl.ds(r0, C), :].astype(F32)
        gate = hg + hg * jnp.tanh(hg)
        pairs = range(N_HEADS // 2)
        pcs = [slice(p * PAIR_W, (p + 1) * PAIR_W) for p in pairs]
        hcs = [[slice((2 * p + a) * HEAD_W, (2 * p + a + 1) * HEAD_W) for a in range(2)] for p in pairs]
        zrows = jnp.zeros((HEAD_W - C, HEAD_W), BF16)
        vpads = []
        for p in pairs:
            k4 = []
            for a in range(2):
                pieces = [k[:, hcs[p][a]] for k in k_sub]
                if nsub * C < width:
                    pieces.append(jnp.zeros((width - nsub * C, HEAD_W), BF16))
                k4.append(jnp.concatenate(pieces, axis=0))
            vpads.append([jnp.concatenate([i_ref[0, 0, pl.ds(r0, C), hcs[p][a]], zrows], axis=0)
                          for a in range(2)])
            sc_scr[p] = lax.dot_general(q_intra[:, pcs[p]], _block_diag(k4[0], k4[1]), nt,
                                        preferred_element_type=F32)
        dcols = []
        for p in pairs:
            kT = []
            for a in range(2):
                kext = jnp.concatenate([k_state[:, hcs[p][a]],
                                        jnp.broadcast_to(st_decay[:, hcs[p][a]], (8, HEAD_W)),
                                        jnp.zeros((HEAD_W - C - 8, HEAD_W), F32)], axis=0).T
                dcols.append(kext[:, C:C + 1])
                kT.append(kext.astype(BF16))
            upd_scr[p] = jnp.dot(jnp.concatenate(kT, axis=1), _block_diag(vpads[p][0], vpads[p][1]),
                                 preferred_element_type=F32)
        for p in pairs:
            w = jnp.where(keep2, sc_scr[p], 0.0)
            lhs = []
            rhs = []
            for a in range(2):
                t = w[:, a * width:a * width + HEAD_W]
                for i in range(1, width // HEAD_W):
                    t = t + w[:, a * width + i * HEAD_W:a * width + (i + 1) * HEAD_W]
                if nsub > 1 and C < HEAD_W:
                    assert HEAD_W == 2 * C
                    t = t + pltpu.roll(t, C, 1)
                lhs += [t.astype(BF16), q_inter[:, hcs[p][a]]]
                rhs.append(jnp.concatenate([vpads[p][a], st_scr[2 * p + a].astype(BF16)], axis=0))
            o_scr[p] = jnp.dot(jnp.concatenate(lhs, axis=1), _block_diag(rhs[0], rhs[1]),
                               preferred_element_type=F32)
        for p in pairs:
            for a in range(2):
                hh = 2 * p + a
                st_scr[hh] = st_scr[hh] * dcols[hh] + upd_scr[p, :, a * HEAD_W:(a + 1) * HEAD_W]
        for p in pairs:
            for a in range(2):
                o = o_scr[p, :, a * HEAD_W:(a + 1) * HEAD_W]
                ya_ref[0, pl.ds(r0, C), hcs[p][a]] = (_rms(o, onorm) * gate[:, hcs[p][a]]).astype(BF16)
        return carry

    n_chunks = TL // C
    lax.fori_loop(0, n_chunks, chunk, 0, unroll=math.gcd(n_chunks, 4))

    @pl.when(lt == pl.num_programs(1) - 1)
    def _():
        sn_ref[0] = st_scr[...]


def _hgrn(z4, f3, lb_logits, onorm, s0, *, L, tl):
    B = z4.shape[1]
    C = min(CHUNK, L)
    slot = lambda s: pl.BlockSpec((1, 1, tl, D_MODEL), lambda b, t: (s, b, t, 0))
    row = pl.BlockSpec((1, tl, D_MODEL), lambda b, t: (b, t, 0))
    state = pl.BlockSpec((1, N_HEADS, HEAD_W, HEAD_W), lambda b, t: (b, 0, 0, 0))
    return pl.pallas_call(
        functools.partial(_hgrn_kernel, TL=tl, C=C),
        grid=(B, L // tl),
        in_specs=[slot(Z_HQ), row, slot(Z_HI), slot(Z_HG),
                  pl.BlockSpec(lb_logits.shape, lambda b, t: (0, 0)),
                  pl.BlockSpec((1, HEAD_W), lambda b, t: (0, 0)),
                  state],
        out_specs=[row, state],
        out_shape=[jax.ShapeDtypeStruct((B, L, D_MODEL), BF16),
                   jax.ShapeDtypeStruct((B, N_HEADS, HEAD_W, HEAD_W), F32)],
        scratch_shapes=[pltpu.VMEM((N_HEADS, HEAD_W, HEAD_W), F32),
                        pltpu.VMEM((N_HEADS // 2, C, 2 * max((C // SUB) * C, HEAD_W)), F32),
                        pltpu.VMEM((N_HEADS // 2, C, PAIR_W), F32),
                        pltpu.VMEM((N_HEADS // 2, HEAD_W, PAIR_W), F32)],
        compiler_params=_cparams(("arbitrary", "arbitrary")),
        name="hgrn",
    )(z4, f3, z4, z4, lb_logits, onorm, s0)


def _diff_lambda(lp_ref):
    lp = lp_ref[...]
    a = jnp.sum(lp[0:1] * lp[1:2], axis=-1, keepdims=True)
    b = jnp.sum(lp[2:3] * lp[3:4], axis=-1, keepdims=True)
    return jnp.exp(a) - jnp.exp(b) + LAMBDA_INIT


def _split_maps(q):
    lane = lax.broadcasted_iota(jnp.int32, q.shape, 1)
    return jnp.where(lane < DIFF_DH, q, 0.0), jnp.where(lane >= DIFF_DH, q, 0.0)


def _diff_attn_prompt_kernel(rb_ref, q_ref, k_ref, v_ref, bias_ref, lp_ref, sub_ref, o_ref, kb, vT, acc, m_scr, s_scr, q_scr,
                             *, T, nk, G, far_bucket):
    hg = pl.program_id(1)
    nt = (((1,), (1,)), ((), ()))

    ones = jnp.ones((ONES_ROWS, T), BF16)
    for g in range(G):
        for j in range(nk):
            kb[g, j] = (k_ref[0, g, j * T:(j + 1) * T, :] * (DIFF_SCALE * LOG2E)).astype(BF16)
            vT[g, j, :HEAD_W, :] = v_ref[0, g, j * T:(j + 1) * T, :].T.astype(BF16)
            vT[g, j, HEAD_W:, :] = ones
        qa, qb = _split_maps(q_ref[0, 0, :, g * HEAD_W:(g + 1) * HEAD_W])
        q_scr[2 * g] = qa
        q_scr[2 * g + 1] = qb
    lam = _diff_lambda(lp_ref)

    def tile(qi, j, keys, queries, bias_slot, first, slot):
        kr = slice(*keys)
        qc = slice(*queries)
        nkk, nqq = keys[1] - keys[0], queries[1] - queries[0]
        qrows = slice(qi * T + queries[0], qi * T + queries[1])
        for g in range(G):
            kt = kb[g, j, kr, :]
            for mp in range(2):
                s_scr[slot, 2 * g + mp, :nkk, :nqq] = lax.dot_general(
                    kt, q_scr[2 * g + mp, qrows, :], nt, preferred_element_type=F32)
        for g in range(G):
            vt = vT[g, j, :, kr]
            if bias_slot is None:
                shift = rb_ref[far_bucket, hg * G + g] * LOG2E
            else:
                bias = bias_ref[g, bias_slot, kr, qc]
            for mp in range(2):
                idx = 2 * g + mp
                s = s_scr[slot, idx, :nkk, :nqq]
                if bias_slot is None:
                    tmax = jnp.max(s, axis=0, keepdims=True) + shift
                else:
                    s = s + bias
                    tmax = jnp.max(s, axis=0, keepdims=True)
                if first:
                    mn = tmax
                else:
                    m = m_scr[idx, :, qc]
                    mn = jnp.maximum(m, tmax)
                p = jnp.exp2(s - (mn - shift)) if bias_slot is None else jnp.exp2(s - mn)
                m_scr[idx, :, qc] = mn
                pv = jnp.dot(vt, p.astype(BF16), preferred_element_type=F32)
                acc[idx, :, qc] = pv if first else jnp.exp2(m - mn) * acc[idx, :, qc] + pv

    n_tiles = 0
    half = T // 2
    assert half % CHUNK == 0
    for qi in range(nk):
        rows = slice(qi * T, (qi + 1) * T)
        order = [(j, (0, T), (0, T), None) for j in range(qi - 1)]
        if qi >= 1:
            order.append((qi - 1, (0, T), (0, T), 0))
        order += [(qi, (0, half), (0, T), 1), (qi, (half, T), (half, T), 1)]
        for t, (j, keys, queries, bias_slot) in enumerate(order):
            tile(qi, j, keys, queries, bias_slot, t == 0, n_tiles % 2)
            n_tiles += 1
        for g in range(G):
            a1 = acc[2 * g]
            a2 = acc[2 * g + 1]
            o = (a1[:HEAD_W] * (1.0 / a1[HEAD_W:HEAD_W + 1])
                 - a2[:HEAD_W] * (lam / a2[HEAD_W:HEAD_W + 1]))
            o = (o * lax.rsqrt(jnp.mean(o * o, axis=0, keepdims=True) + RMS_EPS)).T
            o_ref[0, rows, g * HEAD_W:(g + 1) * HEAD_W] = (o * sub_ref[...] * (1.0 - LAMBDA_INIT)).astype(BF16)


def _diff_attn_prompt(z4, k4, v4, rel_bias, bias, lp, subln, *, T, G, far_bucket):
    B, H, L, _ = k4.shape
    nk = L // T
    W = G * HEAD_W
    kv_spec = pl.BlockSpec((1, G, L, HEAD_W), lambda b, h: (b, h, 0, 0))
    return pl.pallas_call(
        functools.partial(_diff_attn_prompt_kernel, T=T, nk=nk, G=G, far_bucket=far_bucket),
        grid=(B, H // G),
        in_specs=[pl.BlockSpec(memory_space=pltpu.SMEM),
                  pl.BlockSpec((1, 1, L, W), lambda b, h: (Z_DQ, b, 0, h)),
                  kv_spec, kv_spec,
                  pl.BlockSpec((G, 2, T, T), lambda b, h: (h, 0, 0, 0)),
                  pl.BlockSpec(lp.shape, lambda b, h: (0, 0)),
                  pl.BlockSpec((1, HEAD_W), lambda b, h: (0, 0))],
        out_specs=pl.BlockSpec((1, L, W), lambda b, h: (b, 0, h)),
        out_shape=jax.ShapeDtypeStruct((B, L, D_MODEL), BF16),
        scratch_shapes=[pltpu.VMEM((G, nk, T, HEAD_W), BF16),
                        pltpu.VMEM((G, nk, HEAD_W + ONES_ROWS, T), BF16),
                        pltpu.VMEM((2 * G, HEAD_W + ONES_ROWS, T), F32),
                        pltpu.VMEM((2 * G, 1, T), F32),
                        pltpu.VMEM((2, 2 * G, T, T), F32),
                        pltpu.VMEM((2 * G, L, HEAD_W), BF16)],
        compiler_params=_cparams(("arbitrary", "arbitrary")),
        name="diff_attn_prompt",
    )(rel_bias, z4, k4, v4, bias, lp, subln)


def _diff_attn_sample_kernel(q_ref, kn_ref, vn_ref, kc_ref, vc_ref, bp_ref, bn_ref, lp_ref, sub_ref, o_ref,
                             *, Ls, hb):
    lam = _diff_lambda(lp_ref)
    nt = (((1,), (1,)), ((), ()))
    for hh in range(hb):
        cols = slice(hh * HEAD_W, (hh + 1) * HEAD_W)
        qa, qb = _split_maps(q_ref[0, 0, :, cols].astype(F32) * DIFF_SCALE)
        q2 = jnp.concatenate([qa, qb], axis=0).astype(BF16)
        kp = kc_ref[0, hh].astype(BF16)
        vp = vc_ref[0, hh].astype(BF16)
        kn = kn_ref[0, :, cols].astype(BF16)
        vn = vn_ref[0, :, cols].astype(BF16)
        s = lax.dot_general(q2, kp, nt, preferred_element_type=F32) + bp_ref[hh]
        sn = lax.dot_general(q2, kn, nt, preferred_element_type=F32) + bn_ref[hh]
        m = jnp.maximum(jnp.max(s, axis=-1, keepdims=True), jnp.max(sn, axis=-1, keepdims=True))
        p = jnp.exp(s - m)
        pn = jnp.exp(sn - m)
        l = jnp.sum(p, axis=-1, keepdims=True) + jnp.sum(pn, axis=-1, keepdims=True)
        o2 = (jnp.dot(p.astype(BF16), vp, preferred_element_type=F32)
              + jnp.dot(pn.astype(BF16), vn, preferred_element_type=F32)) / l
        o = o2[:Ls] - lam * o2[Ls:]
        o_ref[0, :, cols] = (_rms(o, sub_ref[...]) * (1.0 - LAMBDA_INIT)).astype(BF16)


def _diff_attn_sample(z4, kn3, vn3, kc, vc, bias_p, bias_n, lp, subln, *, hb):
    B, H, P, _ = kc.shape
    Ls = kn3.shape[1]
    W = hb * HEAD_W
    cache_spec = pl.BlockSpec((1, hb, P, HEAD_W), lambda b, g: (b, g, 0, 0))
    new_spec = pl.BlockSpec((1, Ls, W), lambda b, g: (b, 0, g))
    return pl.pallas_call(
        functools.partial(_diff_attn_sample_kernel, Ls=Ls, hb=hb),
        grid=(B, H // hb),
        in_specs=[pl.BlockSpec((1, 1, Ls, W), lambda b, g: (Z_DQ, b, 0, g)),
                  new_spec, new_spec, cache_spec, cache_spec,
                  pl.BlockSpec((hb, 2 * Ls, P), lambda b, g: (g, 0, 0)),
                  pl.BlockSpec((hb, 2 * Ls, Ls), lambda b, g: (g, 0, 0)),
                  pl.BlockSpec(lp.shape, lambda b, g: (0, 0)),
                  pl.BlockSpec((1, HEAD_W), lambda b, g: (0, 0))],
        out_specs=pl.BlockSpec((1, Ls, W), lambda b, g: (b, 0, g)),
        out_shape=jax.ShapeDtypeStruct((B, Ls, D_MODEL), BF16),
        compiler_params=_cparams(("arbitrary", "arbitrary")),
        name="diff_attn_sample",
    )(z4, kn3, vn3, kc, vc, bias_p, bias_n, lp, subln)


def _mem_kv_kernel(m_ref, g_ref, w_ref, k_ref, v_ref):
    h = _rms(m_ref[0], g_ref[...]).astype(BF16)
    kv = jnp.dot(h, w_ref[...], preferred_element_type=F32)
    for hh in range(MEM_HEADS):
        k_ref[0, hh] = kv[:, hh * MEM_DH:(hh + 1) * MEM_DH]
        v_ref[0, hh] = kv[:, D_MODEL + hh * MEM_DH:D_MODEL + (hh + 1) * MEM_DH]


def _mem_kv(mem, g, w_bf):
    B, M, _ = mem.shape
    out = jax.ShapeDtypeStruct((B, MEM_HEADS, M, MEM_DH), F32)
    spec = pl.BlockSpec((1, MEM_HEADS, M, MEM_DH), lambda b: (b, 0, 0, 0))
    return pl.pallas_call(
        _mem_kv_kernel,
        grid=(B,),
        in_specs=[pl.BlockSpec((1, M, D_MODEL), lambda b: (b, 0, 0)),
                  pl.BlockSpec((1, D_MODEL), lambda b: (0, 0)),
                  pl.BlockSpec(w_bf.shape, lambda b: (0, 0))],
        out_specs=[spec, spec],
        out_shape=[out, out],
        compiler_params=_cparams(("arbitrary",)),
        name="mem_kv",
    )(mem, g, w_bf)


def _mem_attn_kernel(q_ref, k_ref, v_ref, o_ref, kb, vb, s_scr):
    nt = (((1,), (1,)), ((), ()))

    @pl.when(pl.program_id(1) == 0)
    def _():
        kb[...] = k_ref[0].astype(BF16)
        vb[...] = v_ref[0].astype(BF16)

    for hh in range(MEM_HEADS):
        cols = slice(hh * MEM_DH, (hh + 1) * MEM_DH)
        s_scr[hh] = lax.dot_general(q_ref[0, 0, :, cols], kb[hh], nt, preferred_element_type=F32)
    for hh in range(MEM_HEADS):
        cols = slice(hh * MEM_DH, (hh + 1) * MEM_DH)
        s = s_scr[hh] * (MEM_DH ** -0.5)
        p = jnp.exp(s - jnp.max(s, axis=-1, keepdims=True))
        p = p / jnp.sum(p, axis=-1, keepdims=True)
        o_ref[0, :, cols] = jnp.dot(p.astype(BF16), vb[hh], preferred_element_type=F32).astype(BF16)


def _mem_attn(z4, mk, mv, *, tl):
    _, B, L, _ = z4.shape
    M = mk.shape[2]
    kv_spec = pl.BlockSpec((1, MEM_HEADS, M, MEM_DH), lambda b, t: (b, 0, 0, 0))
    return pl.pallas_call(
        _mem_attn_kernel,
        grid=(B, L // tl),
        in_specs=[pl.BlockSpec((1, 1, tl, D_MODEL), lambda b, t: (Z_MQ, b, t, 0)), kv_spec, kv_spec],
        out_specs=pl.BlockSpec((1, tl, D_MODEL), lambda b, t: (b, t, 0)),
        out_shape=jax.ShapeDtypeStruct((B, L, D_MODEL), BF16),
        scratch_shapes=[pltpu.VMEM((MEM_HEADS, M, MEM_DH), BF16),
                        pltpu.VMEM((MEM_HEADS, M, MEM_DH), BF16),
                        pltpu.VMEM((MEM_HEADS, tl, M), F32)],
        compiler_params=_cparams(("arbitrary", "arbitrary")),
        name="mem_attn",
    )(z4, mk, mv)


def _merge_kernel(x_ref, ya_ref, yb_ref, yc_ref, g0_ref, g1_ref, g2_ref, wb_ref, wo_ref, nf_ref,
                  x1_ref, h2_ref):
    merged = None
    for n, (y_ref, gate_ref) in enumerate(((ya_ref, g0_ref), (yb_ref, g1_ref), (yc_ref, g2_ref))):
        proj = jnp.dot(y_ref[...], wb_ref[n], preferred_element_type=F32)
        term = proj * _sigmoid(gate_ref[0].astype(F32))
        merged = term if merged is None else merged + term
    x1 = x_ref[...] + jnp.dot(merged.astype(BF16), wo_ref[...], preferred_element_type=F32)
    x1_ref[...] = x1
    h2_ref[...] = _rms(x1, nf_ref[...]).astype(BF16)


def _merge(x2d, ya, yb, yc, z3, wb_bf, wo_bf, nf, *, tm):
    N = x2d.shape[0]
    row = pl.BlockSpec((tm, D_MODEL), lambda i: (i, 0))
    gate = lambda n: pl.BlockSpec((1, tm, D_MODEL), lambda i: (Z_G0 + n, i, 0))
    return pl.pallas_call(
        _merge_kernel,
        grid=(N // tm,),
        in_specs=[row, row, row, row, gate(0), gate(1), gate(2),
                  pl.BlockSpec(wb_bf.shape, lambda i: (0, 0, 0)),
                  pl.BlockSpec(wo_bf.shape, lambda i: (0, 0)),
                  pl.BlockSpec((1, D_MODEL), lambda i: (0, 0))],
        out_specs=[row, row],
        out_shape=[jax.ShapeDtypeStruct((N, D_MODEL), F32), jax.ShapeDtypeStruct((N, D_MODEL), BF16)],
        compiler_params=_cparams(("arbitrary",)),
        name="merge",
    )(x2d, ya, yb, yc, z3, z3, z3, wb_bf, wo_bf, nf)


def _ffn_kernel(h_ref, x_ref, cp_ref, wu_ref, cw_ref, cb_ref, wd_ref, nf_ref, y_ref, cn_ref, carry, u_scr, act_scr,
                *, tm, cw, d_ff, nb):
    lt = pl.program_id(1)
    rows_per = tm // nb

    if nb == 1:
        @pl.when(lt == 0)
        def _():
            carry[...] = cp_ref[0]

    h = h_ref[0]
    edge = 8
    erow = lax.broadcasted_iota(jnp.int32, (edge, cw), 0)
    for jc in range(d_ff // cw):
        slot = jc % 2
        bases = (jc * cw, d_ff + jc * cw)
        for hf, base in enumerate(bases):
            u_scr[slot, hf] = jnp.dot(h, wu_ref[:, base:base + cw], preferred_element_type=F32)
        halves = []
        for hf, base in enumerate(bases):
            cols = slice(base, base + cw)
            u = u_scr[slot, hf]
            w0, w1, w2, bb = cw_ref[0:1, cols], cw_ref[1:2, cols], cw_ref[2:3, cols], cb_ref[:, cols]
            u1 = pltpu.roll(u, 1, 0)
            u2 = pltpu.roll(u, 2, 0)
            c = bb + w0 * u2 + w1 * u1 + w2 * u
            pieces = []
            for s in range(nb):
                r0 = s * rows_per
                prev = carry if nb == 1 else cp_ref.at[s]
                p2 = prev[0:1, cols]
                p1 = prev[1:2, cols]
                u1e = jnp.where(erow == 0, p1, u1[r0:r0 + edge])
                u2e = jnp.where(erow == 0, p2, jnp.where(erow == 1, p1, u2[r0:r0 + edge]))
                pieces += [bb + w0 * u2e + w1 * u1e + w2 * u[r0:r0 + edge], c[r0 + edge:r0 + rows_per]]
                tail = u[r0 + rows_per - (FFN_CONV - 1):r0 + rows_per, :]
                if nb == 1:
                    carry[:, cols] = tail
                cn_ref[s, :, cols] = tail
            halves.append(jnp.concatenate(pieces, axis=0))
        gate, val = halves
        act_scr[:, jc * cw:(jc + 1) * cw] = (gate * _sigmoid(gate) * val).astype(BF16)
    y = x_ref[0] + jnp.dot(act_scr[...], wd_ref[...], preferred_element_type=F32)
    y_ref[0] = _rms(y, nf_ref[...])


def _ffn(h2, x1, conv_prev, wu_bf, conv_w, conv_b, wd_bf, nf, *, tm, nb=1, cw=256):
    G, R, _ = x1.shape
    d_ff = wd_bf.shape[0]
    assert nb == 1 or tm == R
    row = pl.BlockSpec((1, tm, D_MODEL), lambda b, t: (b, t, 0))
    state = pl.BlockSpec((nb, FFN_CONV - 1, 2 * d_ff), lambda b, t: (b, 0, 0))
    const = lambda a: pl.BlockSpec(a.shape, lambda b, t: (0,) * a.ndim)
    return pl.pallas_call(
        functools.partial(_ffn_kernel, tm=tm, cw=cw, d_ff=d_ff, nb=nb),
        grid=(G, R // tm),
        in_specs=[row, row, state, const(wu_bf), const(conv_w), const(conv_b), const(wd_bf), const(nf)],
        out_specs=[row, state],
        out_shape=[jax.ShapeDtypeStruct((G, R, D_MODEL), F32),
                   jax.ShapeDtypeStruct((G * nb, FFN_CONV - 1, 2 * d_ff), F32)],
        scratch_shapes=[pltpu.VMEM((FFN_CONV - 1, 2 * d_ff), F32),
                        pltpu.VMEM((2, 2, tm, cw), F32),
                        pltpu.VMEM((tm, d_ff), BF16)],
        compiler_params=_cparams(("arbitrary", "arbitrary")),
        name="ffn",
    )(h2, x1, conv_prev, wu_bf, conv_w, conv_b, wd_bf, nf)


def _layer(x, W, *, hgrn_s0, conv_prev, mem_k, mem_v, attn):
    B, L, _ = x.shape
    N = B * L
    long_seq = L >= ROW_TILE
    x2d = x.reshape(N, D_MODEL)
    z3, f2d, k_new, v_new = _in_proj(x2d, W["norm_attn"], W["w_in"], tm=PROJ_ROW_TILE if long_seq else N,
                                     rows_per_batch=L, head_layout=long_seq)
    z4 = z3.reshape(N_ZSLOT, B, L, D_MODEL)
    ya, s_new = _hgrn(z4, f2d.reshape(B, L, D_MODEL), W["lb_logits"], W["hgrn_onorm"], hgrn_s0, L=L,
                      tl=min(L, ROW_TILE))
    yb, k_new, v_new = attn(z4, k_new, v_new)
    yc = _mem_attn(z4, mem_k, mem_v, tl=min(L, ROW_TILE))
    x1, h2 = _merge(x2d, ya.reshape(N, D_MODEL), yb.reshape(N, D_MODEL), yc.reshape(N, D_MODEL), z3,
                    W["w_branch"], W["w_out"], W["norm_ffn"], tm=min(N, ROW_TILE))
    nb = 1 if long_seq else B
    groups = (B // nb, nb * L, D_MODEL)
    y, conv_new = _ffn(h2.reshape(groups), x1.reshape(groups), conv_prev, W["w_up"], W["conv_w"], W["conv_b"],
                       W["w_down"], W["norm_final"], tm=ROW_TILE if long_seq else nb * L, nb=nb)
    return y.reshape(B, L, D_MODEL), k_new, v_new, s_new, conv_new


def kernel(x_prompt, x_sample, mem_prompt, cache_diff_k, cache_diff_v, cache_mem_k, cache_mem_v, state_hgrn, state_ffn_conv, rel_bias, hgrn_lb_logits, norm_attn, w_in, hgrn_onorm, diff_lambda, diff_subln, mem_norm, w_mem_kv, w_branch, w_out, norm_ffn, w_up, conv_w, conv_b, w_down, norm_final):
    assert w_in.shape[0] == 1, "single-layer trunk"
    Bp, Lp, _ = x_prompt.shape
    Bs, Ls, _ = x_sample.shape
    P = cache_diff_k.shape[3]
    d_ff2 = w_up.shape[2]
    T = ATTN_TILE
    assert x_prompt.shape[2] == D_MODEL and w_in.shape[2] == N_SEC * D_MODEL
    assert Lp % T == 0 and Lp % CHUNK == 0, "prompt length must tile into attention tiles and chunks"
    assert P % CHUNK == 0 and Ls <= CHUNK and Ls % 8 == 0 and Ls & (Ls - 1) == 0
    row = lambda a: a.reshape(1, -1)
    W = dict(norm_attn=row(norm_attn[0]), w_in=w_in[0].astype(BF16), lb_logits=hgrn_lb_logits,
             hgrn_onorm=row(hgrn_onorm[0]), w_branch=w_branch[0].astype(BF16), w_out=w_out[0].astype(BF16),
             norm_ffn=row(norm_ffn[0]), w_up=w_up[0].astype(BF16), conv_w=conv_w[0], conv_b=row(conv_b[0]),
             w_down=w_down[0].astype(BF16), norm_final=row(norm_final))
    lp = diff_lambda[0]
    subln = row(diff_subln[0])

    bias_p, far_bucket = _bias_prompt(rel_bias, T)
    mk, mv = _mem_kv(mem_prompt, row(mem_norm[0]), w_mem_kv[0].astype(BF16))

    def attn_prompt(z4, k4, v4):
        yb = _diff_attn_prompt(z4, k4, v4, rel_bias, bias_p, lp, subln, T=T, G=PROMPT_ATTN_HEADS,
                               far_bucket=far_bucket)
        return yb, k4, v4

    yp, pk, pv, ps, pc = _layer(
        x_prompt, W, hgrn_s0=jnp.zeros((Bp, N_HEADS, HEAD_W, HEAD_W), F32),
        conv_prev=jnp.zeros((Bp, FFN_CONV - 1, d_ff2), F32), mem_k=mk, mem_v=mv, attn=attn_prompt)

    bias_sp, bias_sn = _bias_sample(rel_bias, Ls, P)

    def attn_sample(z4, k2d, v2d):
        yb = _diff_attn_sample(z4, k2d.reshape(Bs, Ls, D_MODEL), v2d.reshape(Bs, Ls, D_MODEL),
                               cache_diff_k[0], cache_diff_v[0], bias_sp, bias_sn, lp, subln, hb=SAMPLE_ATTN_HEADS)
        heads = lambda a: jnp.transpose(a.reshape(Bs, Ls, N_HEADS, HEAD_W), (0, 2, 1, 3))
        return yb, heads(k2d), heads(v2d)

    ys, sk, sv, ss, sc = _layer(
        x_sample, W, hgrn_s0=state_hgrn[0], conv_prev=state_ffn_conv[0], mem_k=cache_mem_k[0],
        mem_v=cache_mem_v[0], attn=attn_sample)

    return (yp, ys, pk[None], pv[None], ps[None], pc[None], mk[None], mv[None],
            sk[None], sv[None], ss[None], sc[None])
```

```python
import functools
import math

import numpy as np
import jax
import jax.numpy as jnp
from jax import lax
from jax.experimental import pallas as pl
from jax.experimental.pallas import tpu as pltpu

F32 = jnp.float32
BF16 = jnp.bfloat16

D_MODEL = 1024
CHUNK = 64
SUB = 16
HEAD_W = 128
N_HEADS = D_MODEL // HEAD_W
PAIR_W = 2 * HEAD_W
DIFF_DH = 64
DIFF_SCALE = DIFF_DH ** -0.5
MEM_HEADS = 4
MEM_DH = D_MODEL // MEM_HEADS
REL_BUCKETS = 32
REL_MAX_DIST = 128
FFN_CONV = 3
RMS_EPS = 1e-6
LAMBDA_INIT = 0.8 - 0.6 * math.exp(-0.3 * 0)
NEG = -1e30
LOG2E = math.log2(math.e)
ONES_ROWS = 16
ATTN_TILE = 512
PROMPT_ATTN_HEADS = 2
SAMPLE_ATTN_HEADS = 8
ROW_TILE = 512
MEM_ROW_TILE = 2048
PROJ_ROW_TILE = 256

SEC_HQ, SEC_HF, SEC_HI, SEC_HG, SEC_DQ, SEC_DK, SEC_DV, SEC_MQ, SEC_G0 = range(9)
N_SEC = 11
Z_HQ, Z_HI, Z_HG, Z_DQ, Z_MQ, Z_G0 = 0, 1, 2, 3, 4, 5
N_ZSLOT = 8

V7X_VMEM_LIMIT = 56 * 1024 * 1024


def _cparams(sem, vmem=V7X_VMEM_LIMIT):
    return pltpu.CompilerParams(dimension_semantics=sem, vmem_limit_bytes=vmem)


def _sigmoid(x):
    return 0.5 * jnp.tanh(0.5 * x) + 0.5


def _rms(x, g):
    ms = jnp.mean(x * x, axis=-1, keepdims=True)
    return x * lax.rsqrt(ms + RMS_EPS) * g


def _np_bucket(rel):
    nb = REL_BUCKETS // 2
    ret = np.where(rel > 0, nb, 0)
    n = np.abs(rel)
    max_exact = nb // 2
    large = max_exact + (np.log(np.maximum(n, 1).astype(np.float32) / max_exact)
                         / math.log(REL_MAX_DIST / max_exact) * (nb - max_exact)).astype(np.int32)
    large = np.minimum(large, nb - 1)
    return ret + np.where(n < max_exact, n, large)


def _bucket_segments(lo, hi):
    rel = np.arange(lo, hi + 1, dtype=np.int32)
    b = _np_bucket(rel)
    change = np.nonzero(np.diff(b))[0]
    return int(b[0]), [(int(rel[i + 1]), int(b[i + 1])) for i in change]


def _bias_from_rel(rel, rb_ref, h, first_bucket, segs):
    val = jnp.full(rel.shape, rb_ref[first_bucket, h], F32)
    for lo, bk in segs:
        val = jnp.where(rel >= lo, rb_ref[bk, h], val)
    return val


def _bias_prompt_kernel(rb_ref, o_ref, *, T, first_bucket, segs):
    h = pl.program_id(0)
    kk = lax.broadcasted_iota(jnp.int32, (T, T), 0)
    qq = lax.broadcasted_iota(jnp.int32, (T, T), 1)
    o_ref[0, 0] = _bias_from_rel(kk - qq - T, rb_ref, h, first_bucket, segs) * LOG2E
    diag = _bias_from_rel(kk - qq, rb_ref, h, first_bucket, segs) * LOG2E
    o_ref[0, 1] = jnp.where(jnp.bitwise_and(kk, -CHUNK) <= qq, diag, NEG)


def _bias_prompt(rel_bias, T):
    H = rel_bias.shape[1]
    first_bucket, segs = _bucket_segments(-2 * T, T)
    assert all(lo > -T for lo, _ in segs), "bias must be constant beyond one tile"
    tiles = pl.pallas_call(
        functools.partial(_bias_prompt_kernel, T=T, first_bucket=first_bucket, segs=segs),
        grid=(H,),
        in_specs=[pl.BlockSpec(memory_space=pltpu.SMEM)],
        out_specs=pl.BlockSpec((1, 2, T, T), lambda h: (h, 0, 0, 0)),
        out_shape=jax.ShapeDtypeStruct((H, 2, T, T), F32),
        compiler_params=_cparams(("arbitrary",)),
        name="bias_prompt",
    )(rel_bias)
    return tiles, first_bucket


def _bias_sample_kernel(rb_ref, bp_ref, bn_ref, *, Ls, P, first_bucket, segs):
    h = pl.program_id(0)
    qq = jnp.bitwise_and(lax.broadcasted_iota(jnp.int32, (2 * Ls, P), 0), Ls - 1)
    kk = lax.broadcasted_iota(jnp.int32, (2 * Ls, P), 1)
    bp_ref[0] = _bias_from_rel(kk - P - qq, rb_ref, h, first_bucket, segs)
    qn = jnp.bitwise_and(lax.broadcasted_iota(jnp.int32, (2 * Ls, Ls), 0), Ls - 1)
    kn = lax.broadcasted_iota(jnp.int32, (2 * Ls, Ls), 1)
    bn_ref[0] = _bias_from_rel(kn - qn, rb_ref, h, first_bucket, segs)


def _bias_sample(rel_bias, Ls, P):
    H = rel_bias.shape[1]
    first_bucket, segs = _bucket_segments(-(P + Ls), Ls)
    return pl.pallas_call(
        functools.partial(_bias_sample_kernel, Ls=Ls, P=P, first_bucket=first_bucket, segs=segs),
        grid=(H,),
        in_specs=[pl.BlockSpec(memory_space=pltpu.SMEM)],
        out_specs=[pl.BlockSpec((1, 2 * Ls, P), lambda h: (h, 0, 0)),
                   pl.BlockSpec((1, 2 * Ls, Ls), lambda h: (h, 0, 0))],
        out_shape=[jax.ShapeDtypeStruct((H, 2 * Ls, P), F32),
                   jax.ShapeDtypeStruct((H, 2 * Ls, Ls), F32)],
        compiler_params=_cparams(("arbitrary",)),
        name="bias_sample",
    )(rel_bias)


Z_SLOT_OF_SEC = {SEC_HQ: Z_HQ, SEC_HI: Z_HI, SEC_HG: Z_HG, SEC_DQ: Z_DQ, SEC_MQ: Z_MQ,
                 SEC_G0: Z_G0, SEC_G0 + 1: Z_G0 + 1, SEC_G0 + 2: Z_G0 + 2}


def _in_proj_kernel(x_ref, g_ref, w_ref, z_ref, f_ref, k_ref, v_ref, *, head_layout):
    h = _rms(x_ref[...], g_ref[...]).astype(BF16)

    def store_heads(ref, acc):
        if head_layout:
            for hh in range(N_HEADS):
                ref[0, hh] = acc[:, hh * HEAD_W:(hh + 1) * HEAD_W]
        else:
            ref[...] = acc

    for sec in range(N_SEC):
        acc = jnp.dot(h, w_ref[:, sec * D_MODEL:(sec + 1) * D_MODEL], preferred_element_type=F32)
        if sec == SEC_HF:
            f_ref[...] = acc
        elif sec == SEC_DK:
            store_heads(k_ref, acc)
        elif sec == SEC_DV:
            store_heads(v_ref, acc)
        else:
            z_ref[Z_SLOT_OF_SEC[sec]] = acc.astype(BF16)


def _in_proj(x2d, g, w_bf, *, tm, rows_per_batch, head_layout):
    N = x2d.shape[0]
    if head_layout:
        B = N // rows_per_batch
        nlt = rows_per_batch // tm
        kv_shape = jax.ShapeDtypeStruct((B, N_HEADS, rows_per_batch, HEAD_W), F32)
        kv_spec = pl.BlockSpec((1, N_HEADS, tm, HEAD_W), lambda i: (i // nlt, 0, i % nlt, 0))
    else:
        kv_shape = jax.ShapeDtypeStruct((N, D_MODEL), F32)
        kv_spec = pl.BlockSpec((tm, D_MODEL), lambda i: (i, 0))
    return pl.pallas_call(
        functools.partial(_in_proj_kernel, head_layout=head_layout),
        grid=(N // tm,),
        in_specs=[pl.BlockSpec((tm, D_MODEL), lambda i: (i, 0)),
                  pl.BlockSpec((1, D_MODEL), lambda i: (0, 0)),
                  pl.BlockSpec(w_bf.shape, lambda i: (0, 0))],
        out_specs=[pl.BlockSpec((N_ZSLOT, tm, D_MODEL), lambda i: (0, i, 0)),
                   pl.BlockSpec((tm, D_MODEL), lambda i: (i, 0)),
                   kv_spec, kv_spec],
        out_shape=[jax.ShapeDtypeStruct((N_ZSLOT, N, D_MODEL), BF16),
                   jax.ShapeDtypeStruct((N, D_MODEL), F32),
                   kv_shape, kv_shape],
        compiler_params=_cparams(("arbitrary",)),
        name="in_proj",
    )(x2d, g, w_bf)


def _block_diag(a, b):
    za = jnp.zeros((a.shape[0], b.shape[1]), a.dtype)
    zb = jnp.zeros((b.shape[0], a.shape[1]), b.dtype)
    return jnp.concatenate([jnp.concatenate([a, za], axis=1), jnp.concatenate([zb, b], axis=1)], axis=0)


def _hgrn_kernel(q_ref, f_ref, i_ref, g_ref, lbl_ref, on_ref, s0_ref, ya_ref, sn_ref,
                 st_scr, sc_scr, o_scr, upd_scr, *, TL, C):
    lt = pl.program_id(1)
    nsub = C // SUB
    lbl = lbl_ref[...]
    e = jnp.exp(lbl - jnp.max(lbl, axis=0, keepdims=True))
    lb_all = e[0:1] / jnp.sum(e, axis=0, keepdims=True)
    fg_half = 0.5 * (1.0 - lb_all)
    fg_mid = lb_all + fg_half

    @pl.when(lt == 0)
    def _():
        st_scr[...] = s0_ref[0]

    srow = jnp.bitwise_and(lax.broadcasted_iota(jnp.int32, (C, D_MODEL), 0), SUB - 1)
    width = max(nsub * C, HEAD_W)
    wrow = lax.broadcasted_iota(jnp.int32, (C, width), 0)
    wcol = lax.broadcasted_iota(jnp.int32, (C, width), 1)
    base = jnp.bitwise_and(wrow, -SUB) * (C // SUB)
    keep = (wcol >= base) & (wcol <= base + wrow)
    keep2 = jnp.concatenate([keep, keep], axis=1)
    onorm = on_ref[...]
    nt = (((1,), (1,)), ((), ()))

    def chunk(c, carry):
        r0 = pl.multiple_of(c * C, C)
        f = f_ref[0, pl.ds(r0, C), :]
        fg = fg_mid + fg_half * jnp.tanh(0.5 * f)
        l2 = jnp.log2(fg)
        kk = 1.0 - fg
        c = l2
        sh = 1
        while sh < SUB:
            if sh % 8 == 0:
                shifted = jnp.concatenate([jnp.zeros((sh, D_MODEL), F32), c[:C - sh]], axis=0)
            else:
                shifted = pltpu.roll(c, sh, 0)
            c = c + jnp.where(srow >= sh, shifted, 0.0)
            sh *= 2
        r_rows = [jnp.zeros((1, D_MODEL), F32)]
        for i in range(1, nsub):
            r_rows.append(r_rows[-1] + c[SUB * i - 1:SUB * i, :])
        b_last = r_rows[-1] + c[C - 1:C, :]
        b = c if nsub == 1 else jnp.concatenate(
            [c[SUB * i:SUB * (i + 1)] + r_rows[i] for i in range(nsub)], axis=0)
        q = q_ref[0, 0, pl.ds(r0, C), :].astype(F32)
        q_inter = (q * jnp.exp2(b)).astype(BF16)
        q_intra = (q * jnp.exp2(c)).astype(BF16)
        k_state = kk * jnp.exp2(b_last - b)
        k_sub = []
        for i in range(nsub):
            n = SUB * (i + 1)
            ki = (kk[:n] * jnp.exp2(r_rows[i] - b[:n])).astype(BF16)
            if n < C:
                ki = jnp.concatenate([ki, jnp.zeros((C - n, D_MODEL), BF16)], axis=0)
            k_sub.append(ki)
        st_decay = jnp.exp2(b_last)
        hg = 0.5 * g_ref[0, 0, pl.ds(r0, C), :].astype(F32)
        gate = hg + hg * jnp.tanh(hg)
        pairs = range(N_HEADS // 2)
        pcs = [slice(p * PAIR_W, (p + 1) * PAIR_W) for p in pairs]
        hcs = [[slice((2 * p + a) * HEAD_W, (2 * p + a + 1) * HEAD_W) for a in range(2)] for p in pairs]
        zrows = jnp.zeros((HEAD_W - C, HEAD_W), BF16)
        vpads = []
        for p in pairs:
            k4 = []
            for a in range(2):
                pieces = [k[:, hcs[p][a]] for k in k_sub]
                if nsub * C < width:
                    pieces.append(jnp.zeros((width - nsub * C, HEAD_W), BF16))
                k4.append(jnp.concatenate(pieces, axis=0))
            vpads.append([jnp.concatenate([i_ref[0, 0, pl.ds(r0, C), hcs[p][a]], zrows], axis=0)
                          for a in range(2)])
            sc_scr[p] = lax.dot_general(q_intra[:, pcs[p]], _block_diag(k4[0], k4[1]), nt,
                                        preferred_element_type=F32)
        dcols = []
        for p in pairs:
            kT = []
            for a in range(2):
                kext = jnp.concatenate([k_state[:, hcs[p][a]],
                                        jnp.broadcast_to(st_decay[:, hcs[p][a]], (8, HEAD_W)),
                                        jnp.zeros((HEAD_W - C - 8, HEAD_W), F32)], axis=0).T
                dcols.append(kext[:, C:C + 1])
                kT.append(kext.astype(BF16))
            upd_scr[p] = jnp.dot(jnp.concatenate(kT, axis=1), _block_diag(vpads[p][0], vpads[p][1]),
                                 preferred_element_type=F32)
        for p in pairs:
            w = jnp.where(keep2, sc_scr[p], 0.0)
            lhs = []
            rhs = []
            for a in range(2):
                t = w[:, a * width:a * width + HEAD_W]
                for i in range(1, width // HEAD_W):
                    t = t + w[:, a * width + i * HEAD_W:a * width + (i + 1) * HEAD_W]
                if nsub > 1 and C < HEAD_W:
                    assert HEAD_W == 2 * C
                    t = t + pltpu.roll(t, C, 1)
                lhs += [t.astype(BF16), q_inter[:, hcs[p][a]]]
                rhs.append(jnp.concatenate([vpads[p][a], st_scr[2 * p + a].astype(BF16)], axis=0))
            o_scr[p] = jnp.dot(jnp.concatenate(lhs, axis=1), _block_diag(rhs[0], rhs[1]),
                               preferred_element_type=F32)
        for p in pairs:
            for a in range(2):
                hh = 2 * p + a
                st_scr[hh] = st_scr[hh] * dcols[hh] + upd_scr[p, :, a * HEAD_W:(a + 1) * HEAD_W]
        for p in pairs:
            for a in range(2):
                o = o_scr[p, :, a * HEAD_W:(a + 1) * HEAD_W]
                ya_ref[0, pl.ds(r0, C), hcs[p][a]] = (_rms(o, onorm) * gate[:, hcs[p][a]]).astype(BF16)
        return carry

    n_chunks = TL // C
    lax.fori_loop(0, n_chunks, chunk, 0, unroll=math.gcd(n_chunks, 4))

    @pl.when(lt == pl.num_programs(1) - 1)
    def _():
        sn_ref[0] = st_scr[...]


def _hgrn(z4, f3, lb_logits, onorm, s0, *, L, tl):
    B = z4.shape[1]
    C = min(CHUNK, L)
    slot = lambda s: pl.BlockSpec((1, 1, tl, D_MODEL), lambda b, t: (s, b, t, 0))
    row = pl.BlockSpec((1, tl, D_MODEL), lambda b, t: (b, t, 0))
    state = pl.BlockSpec((1, N_HEADS, HEAD_W, HEAD_W), lambda b, t: (b, 0, 0, 0))
    return pl.pallas_call(
        functools.partial(_hgrn_kernel, TL=tl, C=C),
        grid=(B, L // tl),
        in_specs=[slot(Z_HQ), row, slot(Z_HI), slot(Z_HG),
                  pl.BlockSpec(lb_logits.shape, lambda b, t: (0, 0)),
                  pl.BlockSpec((1, HEAD_W), lambda b, t: (0, 0)),
                  state],
        out_specs=[row, state],
        out_shape=[jax.ShapeDtypeStruct((B, L, D_MODEL), BF16),
                   jax.ShapeDtypeStruct((B, N_HEADS, HEAD_W, HEAD_W), F32)],
        scratch_shapes=[pltpu.VMEM((N_HEADS, HEAD_W, HEAD_W), F32),
                        pltpu.VMEM((N_HEADS // 2, C, 2 * max((C // SUB) * C, HEAD_W)), F32),
                        pltpu.VMEM((N_HEADS // 2, C, PAIR_W), F32),
                        pltpu.VMEM((N_HEADS // 2, HEAD_W, PAIR_W), F32)],
        compiler_params=_cparams(("arbitrary", "arbitrary")),
        name="hgrn",
    )(z4, f3, z4, z4, lb_logits, onorm, s0)


def _diff_lambda(lp_ref):
    lp = lp_ref[...]
    a = jnp.sum(lp[0:1] * lp[1:2], axis=-1, keepdims=True)
    b = jnp.sum(lp[2:3] * lp[3:4], axis=-1, keepdims=True)
    return jnp.exp(a) - jnp.exp(b) + LAMBDA_INIT


def _split_maps(q):
    lane = lax.broadcasted_iota(jnp.int32, q.shape, 1)
    return jnp.where(lane < DIFF_DH, q, 0.0), jnp.where(lane >= DIFF_DH, q, 0.0)


def _diff_attn_prompt_kernel(rb_ref, q_ref, k_ref, v_ref, bias_ref, lp_ref, sub_ref, o_ref, kb, vT, acc, m_scr, s_scr, q_scr,
                             *, T, nk, G, far_bucket):
    hg = pl.program_id(1)
    nt = (((1,), (1,)), ((), ()))

    ones = jnp.ones((ONES_ROWS, T), BF16)
    for g in range(G):
        for j in range(nk):
            kb[g, j] = (k_ref[0, g, j * T:(j + 1) * T, :] * (DIFF_SCALE * LOG2E)).astype(BF16)
            vT[g, j, :HEAD_W, :] = v_ref[0, g, j * T:(j + 1) * T, :].T.astype(BF16)
            vT[g, j, HEAD_W:, :] = ones
        qa, qb = _split_maps(q_ref[0, 0, :, g * HEAD_W:(g + 1) * HEAD_W])
        q_scr[2 * g] = qa
        q_scr[2 * g + 1] = qb
    lam = _diff_lambda(lp_ref)

    def tile(qi, j, keys, queries, bias_slot, first, slot):
        kr = slice(*keys)
        qc = slice(*queries)
        nkk, nqq = keys[1] - keys[0], queries[1] - queries[0]
        qrows = slice(qi * T + queries[0], qi * T + queries[1])
        for g in range(G):
            kt = kb[g, j, kr, :]
            for mp in range(2):
                s_scr[slot, 2 * g + mp, :nkk, :nqq] = lax.dot_general(
                    kt, q_scr[2 * g + mp, qrows, :], nt, preferred_element_type=F32)
        for g in range(G):
            vt = vT[g, j, :, kr]
            if bias_slot is None:
                shift = rb_ref[far_bucket, hg * G + g] * LOG2E
            else:
                bias = bias_ref[g, bias_slot, kr, qc]
            for mp in range(2):
                idx = 2 * g + mp
                s = s_scr[slot, idx, :nkk, :nqq]
                if bias_slot is None:
                    tmax = jnp.max(s, axis=0, keepdims=True) + shift
                else:
                    s = s + bias
                    tmax = jnp.max(s, axis=0, keepdims=True)
                if first:
                    mn = tmax
                else:
                    m = m_scr[idx, :, qc]
                    mn = jnp.maximum(m, tmax)
                p = jnp.exp2(s - (mn - shift)) if bias_slot is None else jnp.exp2(s - mn)
                m_scr[idx, :, qc] = mn
                pv = jnp.dot(vt, p.astype(BF16), preferred_element_type=F32)
                acc[idx, :, qc] = pv if first else jnp.exp2(m - mn) * acc[idx, :, qc] + pv

    n_tiles = 0
    half = T // 2
    assert half % CHUNK == 0
    for qi in range(nk):
        rows = slice(qi * T, (qi + 1) * T)
        order = [(j, (0, T), (0, T), None) for j in range(qi - 1)]
        if qi >= 1:
            order.append((qi - 1, (0, T), (0, T), 0))
        order += [(qi, (0, half), (0, T), 1), (qi, (half, T), (half, T), 1)]
        for t, (j, keys, queries, bias_slot) in enumerate(order):
            tile(qi, j, keys, queries, bias_slot, t == 0, n_tiles % 2)
            n_tiles += 1
        for g in range(G):
            a1 = acc[2 * g]
            a2 = acc[2 * g + 1]
            o = (a1[:HEAD_W] * (1.0 / a1[HEAD_W:HEAD_W + 1])
                 - a2[:HEAD_W] * (lam / a2[HEAD_W:HEAD_W + 1]))
            o = (o * lax.rsqrt(jnp.mean(o * o, axis=0, keepdims=True) + RMS_EPS)).T
            o_ref[0, rows, g * HEAD_W:(g + 1) * HEAD_W] = (o * sub_ref[...] * (1.0 - LAMBDA_INIT)).astype(BF16)


def _diff_attn_prompt(z4, k4, v4, rel_bias, bias, lp, subln, *, T, G, far_bucket):
    B, H, L, _ = k4.shape
    nk = L // T
    W = G * HEAD_W
    kv_spec = pl.BlockSpec((1, G, L, HEAD_W), lambda b, h: (b, h, 0, 0))
    return pl.pallas_call(
        functools.partial(_diff_attn_prompt_kernel, T=T, nk=nk, G=G, far_bucket=far_bucket),
        grid=(B, H // G),
        in_specs=[pl.BlockSpec(memory_space=pltpu.SMEM),
                  pl.BlockSpec((1, 1, L, W), lambda b, h: (Z_DQ, b, 0, h)),
                  kv_spec, kv_spec,
                  pl.BlockSpec((G, 2, T, T), lambda b, h: (h, 0, 0, 0)),
                  pl.BlockSpec(lp.shape, lambda b, h: (0, 0)),
                  pl.BlockSpec((1, HEAD_W), lambda b, h: (0, 0))],
        out_specs=pl.BlockSpec((1, L, W), lambda b, h: (b, 0, h)),
        out_shape=jax.ShapeDtypeStruct((B, L, D_MODEL), BF16),
        scratch_shapes=[pltpu.VMEM((G, nk, T, HEAD_W), BF16),
                        pltpu.VMEM((G, nk, HEAD_W + ONES_ROWS, T), BF16),
                        pltpu.VMEM((2 * G, HEAD_W + ONES_ROWS, T), F32),
                        pltpu.VMEM((2 * G, 1, T), F32),
                        pltpu.VMEM((2, 2 * G, T, T), F32),
                        pltpu.VMEM((2 * G, L, HEAD_W), BF16)],
        compiler_params=_cparams(("arbitrary", "arbitrary")),
        name="diff_attn_prompt",
    )(rel_bias, z4, k4, v4, bias, lp, subln)


def _diff_attn_sample_kernel(q_ref, kn_ref, vn_ref, kc_ref, vc_ref, bp_ref, bn_ref, lp_ref, sub_ref, o_ref,
                             *, Ls, hb):
    lam = _diff_lambda(lp_ref)
    nt = (((1,), (1,)), ((), ()))
    for hh in range(hb):
        cols = slice(hh * HEAD_W, (hh + 1) * HEAD_W)
        qa, qb = _split_maps(q_ref[0, 0, :, cols].astype(F32) * DIFF_SCALE)
        q2 = jnp.concatenate([qa, qb], axis=0).astype(BF16)
        kp = kc_ref[0, hh].astype(BF16)
        vp = vc_ref[0, hh].astype(BF16)
        kn = kn_ref[0, :, cols].astype(BF16)
        vn = vn_ref[0, :, cols].astype(BF16)
        s = lax.dot_general(q2, kp, nt, preferred_element_type=F32) + bp_ref[hh]
        sn = lax.dot_general(q2, kn, nt, preferred_element_type=F32) + bn_ref[hh]
        m = jnp.maximum(jnp.max(s, axis=-1, keepdims=True), jnp.max(sn, axis=-1, keepdims=True))
        p = jnp.exp(s - m)
        pn = jnp.exp(sn - m)
        l = jnp.sum(p, axis=-1, keepdims=True) + jnp.sum(pn, axis=-1, keepdims=True)
        o2 = (jnp.dot(p.astype(BF16), vp, preferred_element_type=F32)
              + jnp.dot(pn.astype(BF16), vn, preferred_element_type=F32)) / l
        o = o2[:Ls] - lam * o2[Ls:]
        o_ref[0, :, cols] = (_rms(o, sub_ref[...]) * (1.0 - LAMBDA_INIT)).astype(BF16)


def _diff_attn_sample(z4, kn3, vn3, kc, vc, bias_p, bias_n, lp, subln, *, hb):
    B, H, P, _ = kc.shape
    Ls = kn3.shape[1]
    W = hb * HEAD_W
    cache_spec = pl.BlockSpec((1, hb, P, HEAD_W), lambda b, g: (b, g, 0, 0))
    new_spec = pl.BlockSpec((1, Ls, W), lambda b, g: (b, 0, g))
    return pl.pallas_call(
        functools.partial(_diff_attn_sample_kernel, Ls=Ls, hb=hb),
        grid=(B, H // hb),
        in_specs=[pl.BlockSpec((1, 1, Ls, W), lambda b, g: (Z_DQ, b, 0, g)),
                  new_spec, new_spec, cache_spec, cache_spec,
                  pl.BlockSpec((hb, 2 * Ls, P), lambda b, g: (g, 0, 0)),
                  pl.BlockSpec((hb, 2 * Ls, Ls), lambda b, g: (g, 0, 0)),
                  pl.BlockSpec(lp.shape, lambda b, g: (0, 0)),
                  pl.BlockSpec((1, HEAD_W), lambda b, g: (0, 0))],
        out_specs=pl.BlockSpec((1, Ls, W), lambda b, g: (b, 0, g)),
        out_shape=jax.ShapeDtypeStruct((B, Ls, D_MODEL), BF16),
        compiler_params=_cparams(("arbitrary", "arbitrary")),
        name="diff_attn_sample",
    )(z4, kn3, vn3, kc, vc, bias_p, bias_n, lp, subln)


def _mem_kv_kernel(m_ref, g_ref, w_ref, k_ref, v_ref):
    h = _rms(m_ref[0], g_ref[...]).astype(BF16)
    kv = jnp.dot(h, w_ref[...], preferred_element_type=F32)
    for hh in range(MEM_HEADS):
        k_ref[0, hh] = kv[:, hh * MEM_DH:(hh + 1) * MEM_DH]
        v_ref[0, hh] = kv[:, D_MODEL + hh * MEM_DH:D_MODEL + (hh + 1) * MEM_DH]


def _mem_kv(mem, g, w_bf):
    B, M, _ = mem.shape
    out = jax.ShapeDtypeStruct((B, MEM_HEADS, M, MEM_DH), F32)
    spec = pl.BlockSpec((1, MEM_HEADS, M, MEM_DH), lambda b: (b, 0, 0, 0))
    return pl.pallas_call(
        _mem_kv_kernel,
        grid=(B,),
        in_specs=[pl.BlockSpec((1, M, D_MODEL), lambda b: (b, 0, 0)),
                  pl.BlockSpec((1, D_MODEL), lambda b: (0, 0)),
                  pl.BlockSpec(w_bf.shape, lambda b: (0, 0))],
        out_specs=[spec, spec],
        out_shape=[out, out],
        compiler_params=_cparams(("arbitrary",)),
        name="mem_kv",
    )(mem, g, w_bf)


def _mem_attn_kernel(q_ref, k_ref, v_ref, o_ref, kb, vb, s_scr):
    nt = (((1,), (1,)), ((), ()))

    @pl.when(pl.program_id(1) == 0)
    def _():
        kb[...] = k_ref[0].astype(BF16)
        vb[...] = v_ref[0].astype(BF16)

    for hh in range(MEM_HEADS):
        cols = slice(hh * MEM_DH, (hh + 1) * MEM_DH)
        s_scr[hh] = lax.dot_general(q_ref[0, 0, :, cols], kb[hh], nt, preferred_element_type=F32)
    for hh in range(MEM_HEADS):
        cols = slice(hh * MEM_DH, (hh + 1) * MEM_DH)
        s = s_scr[hh] * (MEM_DH ** -0.5)
        p = jnp.exp(s - jnp.max(s, axis=-1, keepdims=True))
        p = p / jnp.sum(p, axis=-1, keepdims=True)
        o_ref[0, :, cols] = jnp.dot(p.astype(BF16), vb[hh], preferred_element_type=F32).astype(BF16)


def _mem_attn(z4, mk, mv, *, tl):
    _, B, L, _ = z4.shape
    M = mk.shape[2]
    kv_spec = pl.BlockSpec((1, MEM_HEADS, M, MEM_DH), lambda b, t: (b, 0, 0, 0))
    return pl.pallas_call(
        _mem_attn_kernel,
        grid=(B, L // tl),
        in_specs=[pl.BlockSpec((1, 1, tl, D_MODEL), lambda b, t: (Z_MQ, b, t, 0)), kv_spec, kv_spec],
        out_specs=pl.BlockSpec((1, tl, D_MODEL), lambda b, t: (b, t, 0)),
        out_shape=jax.ShapeDtypeStruct((B, L, D_MODEL), BF16),
        scratch_shapes=[pltpu.VMEM((MEM_HEADS, M, MEM_DH), BF16),
                        pltpu.VMEM((MEM_HEADS, M, MEM_DH), BF16),
                        pltpu.VMEM((MEM_HEADS, tl, M), F32)],
        compiler_params=_cparams(("arbitrary", "arbitrary")),
        name="mem_attn",
    )(z4, mk, mv)


def _merge_kernel(x_ref, ya_ref, yb_ref, yc_ref, g0_ref, g1_ref, g2_ref, wb_ref, wo_ref, nf_ref,
                  x1_ref, h2_ref):
    merged = None
    for n, (y_ref, gate_ref) in enumerate(((ya_ref, g0_ref), (yb_ref, g1_ref), (yc_ref, g2_ref))):
        proj = jnp.dot(y_ref[...], wb_ref[n], preferred_element_type=F32)
        term = proj * _sigmoid(gate_ref[0].astype(F32))
        merged = term if merged is None else merged + term
    x1 = x_ref[...] + jnp.dot(merged.astype(BF16), wo_ref[...], preferred_element_type=F32)
    x1_ref[...] = x1
    h2_ref[...] = _rms(x1, nf_ref[...]).astype(BF16)


def _merge(x2d, ya, yb, yc, z3, wb_bf, wo_bf, nf, *, tm):
    N = x2d.shape[0]
    row = pl.BlockSpec((tm, D_MODEL), lambda i: (i, 0))
    gate = lambda n: pl.BlockSpec((1, tm, D_MODEL), lambda i: (Z_G0 + n, i, 0))
    return pl.pallas_call(
        _merge_kernel,
        grid=(N // tm,),
        in_specs=[row, row, row, row, gate(0), gate(1), gate(2),
                  pl.BlockSpec(wb_bf.shape, lambda i: (0, 0, 0)),
                  pl.BlockSpec(wo_bf.shape, lambda i: (0, 0)),
                  pl.BlockSpec((1, D_MODEL), lambda i: (0, 0))],
        out_specs=[row, row],
        out_shape=[jax.ShapeDtypeStruct((N, D_MODEL), F32), jax.ShapeDtypeStruct((N, D_MODEL), BF16)],
        compiler_params=_cparams(("arbitrary",)),
        name="merge",
    )(x2d, ya, yb, yc, z3, z3, z3, wb_bf, wo_bf, nf)


def _ffn_kernel(h_ref, x_ref, cp_ref, wu_ref, cw_ref, cb_ref, wd_ref, nf_ref, y_ref, cn_ref, carry, u_scr, act_scr,
                *, tm, cw, d_ff, nb):
    lt = pl.program_id(1)
    rows_per = tm // nb

    if nb == 1:
        @pl.when(lt == 0)
        def _():
            carry[...] = cp_ref[0]

    h = h_ref[0]
    edge = 8
    erow = lax.broadcasted_iota(jnp.int32, (edge, cw), 0)
    for jc in range(d_ff // cw):
        slot = jc % 2
        bases = (jc * cw, d_ff + jc * cw)
        for hf, base in enumerate(bases):
            u_scr[slot, hf] = jnp.dot(h, wu_ref[:, base:base + cw], preferred_element_type=F32)
        halves = []
        for hf, base in enumerate(bases):
            cols = slice(base, base + cw)
            u = u_scr[slot, hf]
            w0, w1, w2, bb = cw_ref[0:1, cols], cw_ref[1:2, cols], cw_ref[2:3, cols], cb_ref[:, cols]
            u1 = pltpu.roll(u, 1, 0)
            u2 = pltpu.roll(u, 2, 0)
            c = bb + w0 * u2 + w1 * u1 + w2 * u
            pieces = []
            for s in range(nb):
                r0 = s * rows_per
                prev = carry if nb == 1 else cp_ref.at[s]
                p2 = prev[0:1, cols]
                p1 = prev[1:2, cols]
                u1e = jnp.where(erow == 0, p1, u1[r0:r0 + edge])
                u2e = jnp.where(erow == 0, p2, jnp.where(erow == 1, p1, u2[r0:r0 + edge]))
                pieces += [bb + w0 * u2e + w1 * u1e + w2 * u[r0:r0 + edge], c[r0 + edge:r0 + rows_per]]
                tail = u[r0 + rows_per - (FFN_CONV - 1):r0 + rows_per, :]
                if nb == 1:
                    carry[:, cols] = tail
                cn_ref[s, :, cols] = tail
            halves.append(jnp.concatenate(pieces, axis=0))
        gate, val = halves
        act_scr[:, jc * cw:(jc + 1) * cw] = (gate * _sigmoid(gate) * val).astype(BF16)
    y = x_ref[0] + jnp.dot(act_scr[...], wd_ref[...], preferred_element_type=F32)
    y_ref[0] = _rms(y, nf_ref[...])


def _ffn(h2, x1, conv_prev, wu_bf, conv_w, conv_b, wd_bf, nf, *, tm, nb=1, cw=256):
    G, R, _ = x1.shape
    d_ff = wd_bf.shape[0]
    assert nb == 1 or tm == R
    row = pl.BlockSpec((1, tm, D_MODEL), lambda b, t: (b, t, 0))
    state = pl.BlockSpec((nb, FFN_CONV - 1, 2 * d_ff), lambda b, t: (b, 0, 0))
    const = lambda a: pl.BlockSpec(a.shape, lambda b, t: (0,) * a.ndim)
    return pl.pallas_call(
        functools.partial(_ffn_kernel, tm=tm, cw=cw, d_ff=d_ff, nb=nb),
        grid=(G, R // tm),
        in_specs=[row, row, state, const(wu_bf), const(conv_w), const(conv_b), const(wd_bf), const(nf)],
        out_specs=[row, state],
        out_shape=[jax.ShapeDtypeStruct((G, R, D_MODEL), F32),
                   jax.ShapeDtypeStruct((G * nb, FFN_CONV - 1, 2 * d_ff), F32)],
        scratch_shapes=[pltpu.VMEM((FFN_CONV - 1, 2 * d_ff), F32),
                        pltpu.VMEM((2, 2, tm, cw), F32),
                        pltpu.VMEM((tm, d_ff), BF16)],
        compiler_params=_cparams(("arbitrary", "arbitrary")),
        name="ffn",
    )(h2, x1, conv_prev, wu_bf, conv_w, conv_b, wd_bf, nf)


def _layer(x, W, *, hgrn_s0, conv_prev, mem_k, mem_v, attn):
    B, L, _ = x.shape
    N = B * L
    long_seq = L >= ROW_TILE
    x2d = x.reshape(N, D_MODEL)
    z3, f2d, k_new, v_new = _in_proj(x2d, W["norm_attn"], W["w_in"], tm=PROJ_ROW_TILE if long_seq else N,
                                     rows_per_batch=L, head_layout=long_seq)
    z4 = z3.reshape(N_ZSLOT, B, L, D_MODEL)
    ya, s_new = _hgrn(z4, f2d.reshape(B, L, D_MODEL), W["lb_logits"], W["hgrn_onorm"], hgrn_s0, L=L,
                      tl=min(L, ROW_TILE))
    yb, k_new, v_new = attn(z4, k_new, v_new)
    yc = _mem_attn(z4, mem_k, mem_v, tl=min(L, MEM_ROW_TILE))
    x1, h2 = _merge(x2d, ya.reshape(N, D_MODEL), yb.reshape(N, D_MODEL), yc.reshape(N, D_MODEL), z3,
                    W["w_branch"], W["w_out"], W["norm_ffn"], tm=min(N, ROW_TILE))
    nb = 1 if long_seq else B
    groups = (B // nb, nb * L, D_MODEL)
    y, conv_new = _ffn(h2.reshape(groups), x1.reshape(groups), conv_prev, W["w_up"], W["conv_w"], W["conv_b"],
                       W["w_down"], W["norm_final"], tm=ROW_TILE if long_seq else nb * L, nb=nb)
    return y.reshape(B, L, D_MODEL), k_new, v_new, s_new, conv_new


def kernel(x_prompt, x_sample, mem_prompt, cache_diff_k, cache_diff_v, cache_mem_k, cache_mem_v, state_hgrn, state_ffn_conv, rel_bias, hgrn_lb_logits, norm_attn, w_in, hgrn_onorm, diff_lambda, diff_subln, mem_norm, w_mem_kv, w_branch, w_out, norm_ffn, w_up, conv_w, conv_b, w_down, norm_final):
    assert w_in.shape[0] == 1, "single-layer trunk"
    Bp, Lp, _ = x_prompt.shape
    Bs, Ls, _ = x_sample.shape
    P = cache_diff_k.shape[3]
    d_ff2 = w_up.shape[2]
    T = ATTN_TILE
    assert x_prompt.shape[2] == D_MODEL and w_in.shape[2] == N_SEC * D_MODEL
    assert Lp % T == 0 and Lp % CHUNK == 0, "prompt length must tile into attention tiles and chunks"
    assert P % CHUNK == 0 and Ls <= CHUNK and Ls % 8 == 0 and Ls & (Ls - 1) == 0
    row = lambda a: a.reshape(1, -1)
    W = dict(norm_attn=row(norm_attn[0]), w_in=w_in[0].astype(BF16), lb_logits=hgrn_lb_logits,
             hgrn_onorm=row(hgrn_onorm[0]), w_branch=w_branch[0].astype(BF16), w_out=w_out[0].astype(BF16),
             norm_ffn=row(norm_ffn[0]), w_up=w_up[0].astype(BF16), conv_w=conv_w[0], conv_b=row(conv_b[0]),
             w_down=w_down[0].astype(BF16), norm_final=row(norm_final))
    lp = diff_lambda[0]
    subln = row(diff_subln[0])

    bias_p, far_bucket = _bias_prompt(rel_bias, T)
    mk, mv = _mem_kv(mem_prompt, row(mem_norm[0]), w_mem_kv[0].astype(BF16))

    def attn_prompt(z4, k4, v4):
        yb = _diff_attn_prompt(z4, k4, v4, rel_bias, bias_p, lp, subln, T=T, G=PROMPT_ATTN_HEADS,
                               far_bucket=far_bucket)
        return yb, k4, v4

    yp, pk, pv, ps, pc = _layer(
        x_prompt, W, hgrn_s0=jnp.zeros((Bp, N_HEADS, HEAD_W, HEAD_W), F32),
        conv_prev=jnp.zeros((Bp, FFN_CONV - 1, d_ff2), F32), mem_k=mk, mem_v=mv, attn=attn_prompt)

    bias_sp, bias_sn = _bias_sample(rel_bias, Ls, P)

    def attn_sample(z4, k2d, v2d):
        yb = _diff_attn_sample(z4, k2d.reshape(Bs, Ls, D_MODEL), v2d.reshape(Bs, Ls, D_MODEL),
                               cache_diff_k[0], cache_diff_v[0], bias_sp, bias_sn, lp, subln, hb=SAMPLE_ATTN_HEADS)
        heads = lambda a: jnp.transpose(a.reshape(Bs, Ls, N_HEADS, HEAD_W), (0, 2, 1, 3))
        return yb, heads(k2d), heads(v2d)

    ys, sk, sv, ss, sc = _layer(
        x_sample, W, hgrn_s0=state_hgrn[0], conv_prev=state_ffn_conv[0], mem_k=cache_mem_k[0],
        mem_v=cache_mem_v[0], attn=attn_sample)

    return (yp, ys, pk[None], pv[None], ps[None], pc[None], mk[None], mv[None],
            sk[None], sv[None], ss[None], sc[None])
```
